```python
import jax, jax.numpy as jnp
from jax import lax
import numpy as np

D_MODEL = 1024
BATCH = 8
SEQ = 4096
DEPTH = 4
DEC_BATCH = 8
DEC_SEQ = 64
PAST_LEN = 4096

CHUNK = 64
N_MIXERS = 3
N_LAYERS_A = (DEPTH + 2) // 3
N_LAYERS_B = (DEPTH + 1) // 3
N_LAYERS_C = DEPTH // 3
N_DENSE = (DEPTH + 1) // 2
N_MOE = DEPTH // 2
A_CHUNK = 128
A_HALF = 2 * D_MODEL
A_GROUPS = 8
A_HEAD_DIM = A_HALF // A_GROUPS
B_WIDTH = 3
C_WIDTH = 31
D_FF = 2816
N_EXPERTS = 8
TOP_K = 2
D_EXPERT = 2816
RMS_EPS = 1e-6
LN_EPS = 1e-5

kernel_name = "hybrid_stream_gmlp_shortconv_conformer_moe_step"


def _rmsnorm(x, g):
    xf = x.astype(jnp.float32)
    y = xf * lax.rsqrt(jnp.mean(xf * xf, axis=-1, keepdims=True) + RMS_EPS)
    return (y * g.astype(jnp.float32)).astype(x.dtype)


def _layernorm(x, g, b):
    xf = x.astype(jnp.float32)
    mu = jnp.mean(xf, axis=-1, keepdims=True)
    var = jnp.mean(jnp.square(xf - mu), axis=-1, keepdims=True)
    y = (xf - mu) * lax.rsqrt(var + LN_EPS) * g.astype(jnp.float32) + b.astype(jnp.float32)
    return y.astype(x.dtype)


def _causal_depthwise_conv(x, prev, w):
    width = w.shape[0]
    full = jnp.concatenate([prev.astype(x.dtype), x], axis=1)
    y = lax.conv_general_dilated(full, w[:, None, :].astype(x.dtype), window_strides=(1,), padding="VALID",
                                 dimension_numbers=("NWC", "WIO", "NWC"), feature_group_count=x.shape[-1])
    return y, full[:, full.shape[1] - (width - 1):]


def _gmlp_mix(h, w_in, ln_g, ln_b, w_s, b_s, w_out):
    bsz, length, _ = h.shape
    lc = min(length, A_CHUNK)
    nc = length // lc
    z = jax.nn.gelu(h @ w_in)
    u, v = jnp.split(z, 2, axis=-1)
    v = _layernorm(v, ln_g, ln_b)
    mask = jnp.tril(jnp.ones((lc, lc), dtype=bool))
    ws = jnp.where(mask[None], w_s[:, :lc, :lc], 0).astype(v.dtype)
    vb = v.reshape(bsz, nc, lc, A_GROUPS, A_HEAD_DIM)
    bias = jnp.transpose(b_s[:, :lc]).astype(v.dtype)[None, None, :, :, None]
    s = jnp.einsum("gts,bcsgd->bctgd", ws, vb) + bias
    y = u * s.reshape(bsz, length, A_HALF)
    return y @ w_out, v


def _short_conv_mix(h, prev, w_in, w_conv, w_out):
    bg, cg, xin = jnp.split(h @ w_in, 3, axis=-1)
    y, new_prev = _causal_depthwise_conv(cg * xin, prev, w_conv)
    return (bg * y) @ w_out, new_prev


def _conformer_conv(h, prev, w_pw1, b_pw1, w_dw, b_dw, ln_g, ln_b, w_pw2, b_pw2):
    a, gate = jnp.split(h @ w_pw1 + b_pw1, 2, axis=-1)
    g = a * jax.nn.sigmoid(gate)
    y, new_prev = _causal_depthwise_conv(g, prev, w_dw)
    y = jax.nn.silu(_layernorm(y + b_dw, ln_g, ln_b))
    return y @ w_pw2 + b_pw2, new_prev


def _swiglu(h, w_gate, w_up, w_down):
    return (jax.nn.silu(h @ w_gate) * (h @ w_up)) @ w_down


def _moe_swiglu(h, w_router, w_gate, w_up, w_down):
    n_tok = h.shape[0]
    logits = (h @ w_router).astype(jnp.float32)
    probs = jax.nn.softmax(logits, axis=-1)
    top_p, top_i = lax.top_k(probs, TOP_K)
    top_p = top_p / jnp.sum(top_p, axis=-1, keepdims=True)
    flat_e = top_i.reshape(-1)
    order = jnp.argsort(flat_e)
    tok = order // TOP_K
    sizes = jnp.bincount(flat_e, length=N_EXPERTS).astype(jnp.int32)
    xs = h[tok]
    hid = jax.nn.silu(lax.ragged_dot(xs, w_gate, sizes)) * lax.ragged_dot(xs, w_up, sizes)
    ys = lax.ragged_dot(hid, w_down, sizes)
    ys = ys * top_p.reshape(-1)[order][:, None].astype(ys.dtype)
    return jax.ops.segment_sum(ys, tok, num_segments=n_tok)


def setup_inputs(seed: int = 0) -> dict:
    key = jax.random.key(seed)
    ks = jax.random.split(key, 40)

    def nrm(k, shape, scale=1.0):
        return jax.random.normal(k, shape, dtype=jnp.float32) * scale

    D = D_MODEL
    return {
        "x_prompt": nrm(ks[0], (BATCH, SEQ, D)),
        "x_sample": nrm(ks[1], (DEC_BATCH, DEC_SEQ, D)),
        "state_conv_b": nrm(ks[2], (N_LAYERS_B, DEC_BATCH, B_WIDTH - 1, D), 0.5),
        "state_conv_c": nrm(ks[3], (N_LAYERS_C, DEC_BATCH, C_WIDTH - 1, D), 0.5),
        "norm_mix": 1.0 + nrm(ks[4], (DEPTH, D), 0.02),
        "norm_ffn": 1.0 + nrm(ks[5], (DEPTH, D), 0.02),
        "norm_final": 1.0 + nrm(ks[6], (D,), 0.02),
        "a_w_in": nrm(ks[7], (N_LAYERS_A, D, 2 * A_HALF), D ** -0.5),
        "a_ln_g": 1.0 + nrm(ks[8], (N_LAYERS_A, A_HALF), 0.02),
        "a_ln_b": nrm(ks[9], (N_LAYERS_A, A_HALF), 0.02),
        "a_w_s": nrm(ks[10], (N_LAYERS_A, A_GROUPS, A_CHUNK, A_CHUNK), 0.5 * A_CHUNK ** -0.5),
        "a_b_s": 1.0 + nrm(ks[11], (N_LAYERS_A, A_GROUPS, A_CHUNK), 0.02),
        "a_w_out": nrm(ks[12], (N_LAYERS_A, A_HALF, D), A_HALF ** -0.5),
        "b_w_in": nrm(ks[13], (N_LAYERS_B, D, 3 * D), D ** -0.5),
        "b_conv": nrm(ks[14], (N_LAYERS_B, B_WIDTH, D), B_WIDTH ** -0.5),
        "b_w_out": nrm(ks[15], (N_LAYERS_B, D, D), D ** -0.5),
        "c_w_pw1": nrm(ks[16], (N_LAYERS_C, D, 2 * D), D ** -0.5),
        "c_b_pw1": nrm(ks[17], (N_LAYERS_C, 2 * D), 0.02),
        "c_dw": nrm(ks[18], (N_LAYERS_C, C_WIDTH, D), C_WIDTH ** -0.5),
        "c_b_dw": nrm(ks[19], (N_LAYERS_C, D), 0.02),
        "c_ln_g": 1.0 + nrm(ks[20], (N_LAYERS_C, D), 0.02),
        "c_ln_b": nrm(ks[21], (N_LAYERS_C, D), 0.02),
        "c_w_pw2": nrm(ks[22], (N_LAYERS_C, D, D), D ** -0.5),
        "c_b_pw2": nrm(ks[23], (N_LAYERS_C, D), 0.02),
        "f_w_gate": nrm(ks[24], (N_DENSE, D, D_FF), D ** -0.5),
        "f_w_up": nrm(ks[25], (N_DENSE, D, D_FF), D ** -0.5),
        "f_w_down": nrm(ks[26], (N_DENSE, D_FF, D), D_FF ** -0.5),
        "m_router": nrm(ks[27], (N_MOE, D, N_EXPERTS), D ** -0.5),
        "m_w_gate": nrm(ks[28], (N_MOE, N_EXPERTS, D, D_EXPERT), D ** -0.5),
        "m_w_up": nrm(ks[29], (N_MOE, N_EXPERTS, D, D_EXPERT), D ** -0.5),
        "m_w_down": nrm(ks[30], (N_MOE, N_EXPERTS, D_EXPERT, D), D_EXPERT ** -0.5),
    }


def reference(x_prompt, x_sample, state_conv_b, state_conv_c, norm_mix, norm_ffn, norm_final,
              a_w_in, a_ln_g, a_ln_b, a_w_s, a_b_s, a_w_out,
              b_w_in, b_conv, b_w_out,
              c_w_pw1, c_b_pw1, c_dw, c_b_dw, c_ln_g, c_ln_b, c_w_pw2, c_b_pw2,
              f_w_gate, f_w_up, f_w_down,
              m_router, m_w_gate, m_w_up, m_w_down):
    xp, xs = x_prompt, x_sample
    bp, sp = xp.shape[0], xp.shape[1]
    n_prompt_tok = bp * sp
    new_b_p, new_b_s, new_c_p, new_c_s, new_a_s = [], [], [], [], []
    for i in range(DEPTH):
        kind = i % N_MIXERS
        j = i // N_MIXERS
        hp = _rmsnorm(xp, norm_mix[i])
        hs = _rmsnorm(xs, norm_mix[i])
        if kind == 0:
            mp, _ = _gmlp_mix(hp, a_w_in[j], a_ln_g[j], a_ln_b[j], a_w_s[j], a_b_s[j], a_w_out[j])
            ms, v_new = _gmlp_mix(hs, a_w_in[j], a_ln_g[j], a_ln_b[j], a_w_s[j], a_b_s[j], a_w_out[j])
            new_a_s.append(v_new)
        elif kind == 1:
            zero_prev = jnp.zeros((bp, B_WIDTH - 1, D_MODEL), dtype=xp.dtype)
            mp, st_p = _short_conv_mix(hp, zero_prev, b_w_in[j], b_conv[j], b_w_out[j])
            ms, st_s = _short_conv_mix(hs, state_conv_b[j], b_w_in[j], b_conv[j], b_w_out[j])
            new_b_p.append(st_p)
            new_b_s.append(st_s)
        else:
            zero_prev = jnp.zeros((bp, C_WIDTH - 1, D_MODEL), dtype=xp.dtype)
            mp, st_p = _conformer_conv(hp, zero_prev, c_w_pw1[j], c_b_pw1[j], c_dw[j], c_b_dw[j],
                                       c_ln_g[j], c_ln_b[j], c_w_pw2[j], c_b_pw2[j])
            ms, st_s = _conformer_conv(hs, state_conv_c[j], c_w_pw1[j], c_b_pw1[j], c_dw[j], c_b_dw[j],
                                       c_ln_g[j], c_ln_b[j], c_w_pw2[j], c_b_pw2[j])
            new_c_p.append(st_p)
            new_c_s.append(st_s)
        xp = xp + mp
        xs = xs + ms
        tokens = jnp.concatenate([xp.reshape(-1, D_MODEL), xs.reshape(-1, D_MODEL)], axis=0)
        h = _rmsnorm(tokens, norm_ffn[i])
        if i % 2 == 0:
            f = _swiglu(h, f_w_gate[i // 2], f_w_up[i // 2], f_w_down[i // 2])
        else:
            f = _moe_swiglu(h, m_router[i // 2], m_w_gate[i // 2], m_w_up[i // 2], m_w_down[i // 2])
        xp = xp + f[:n_prompt_tok].reshape(xp.shape)
        xs = xs + f[n_prompt_tok:].reshape(xs.shape)
    y_prompt = _rmsnorm(xp, norm_final)
    y_sample = _rmsnorm(xs, norm_final)
    new_conv_b_prompt = jnp.stack(new_b_p, axis=0)
    new_conv_b_sample = jnp.stack(new_b_s, axis=0)
    new_conv_c_prompt = jnp.stack(new_c_p, axis=0)
    new_conv_c_sample = jnp.stack(new_c_s, axis=0)
    new_v_a_sample = jnp.stack(new_a_s, axis=0)
    return (y_prompt, y_sample, new_conv_b_prompt, new_conv_b_sample, new_conv_c_prompt, new_conv_c_sample, new_v_a_sample)
```

```python
import functools

import jax
import jax.numpy as jnp
from jax import lax
from jax.experimental import pallas as pl
from jax.experimental.pallas import tpu as pltpu

F32 = jnp.float32
BF16 = jnp.bfloat16

D = 1024
TM = 512
SEQ = 4096
SAMPLE_SEQ = 64
N_SEQ = 8
TILES_PER_SEQ = SEQ // TM
N_PROMPT_TOK = N_SEQ * SEQ
N_SAMPLE_TOK = N_SEQ * SAMPLE_SEQ
N_TOK = N_PROMPT_TOK + N_SAMPLE_TOK
N_PROMPT_TILES = N_PROMPT_TOK // TM
N_TILES = N_TOK // TM
assert N_SAMPLE_TOK == TM and N_TILES == N_PROMPT_TILES + 1

A_HALF = 2 * D
A_GROUPS = 8
A_HEAD = A_HALF // A_GROUPS
A_CHUNK = 128
B_WIDTH = 3
C_WIDTH = 31
B_HALO = 8
C_HALO = 32
D_FF = 2816
N_EXPERTS = 8
LANES = 128
SUBLANES = 8
MXU_COLS = 256
RMS_EPS = 1e-6
LN_EPS = 1e-5

BM = 512
N_ROWS = 2 * N_TOK + N_EXPERTS * BM
N_BLOCKS = N_ROWS // BM
FF_CHUNKS = ((0, 1024), (1024, 1024), (2048, 768))

VMEM_LIMIT = 56 * 1024 * 1024


def _const_spec(shape):
    return pl.BlockSpec(shape, lambda *_: (0,) * len(shape), pipeline_mode=pl.Buffered(1))


def _rmsnorm(x, g):
    return x * lax.rsqrt(jnp.mean(x * x, axis=-1, keepdims=True) + RMS_EPS) * g


def _layernorm(x, g, b):
    mu = jnp.mean(x, axis=-1, keepdims=True)
    xc = x - mu
    var = jnp.mean(xc * xc, axis=-1, keepdims=True)
    return xc * lax.rsqrt(var + LN_EPS) * g + b


def _dot(a, b):
    return jnp.dot(a, b, preferred_element_type=F32)


def _gmlp_kernel(x_ref, g_ref, win_ref, lng_ref, lnb_ref, ws_ref, bs_ref, wout_ref,
                 o_ref, v_ref, vn_s, y_s):
    is_sample = pl.program_id(0) >= N_PROMPT_TILES
    x = x_ref[...]
    h = _rmsnorm(x, g_ref[...]).astype(BF16)
    v = jax.nn.gelu(_dot(h, win_ref[:, A_HALF:]))
    vn = _layernorm(v, lng_ref[...], lnb_ref[...])

    @pl.when(is_sample)
    def _():
        v_ref[...] = vn

    vn_s[...] = vn.astype(BF16)
    r = lax.broadcasted_iota(jnp.int32, (A_CHUNK, A_CHUNK), 0)
    c = lax.broadcasted_iota(jnp.int32, (A_CHUNK, A_CHUNK), 1)
    seg_shift = jnp.where(is_sample, 6, 7)
    mask = (r >= c) & ((r >> seg_shift) == (c >> seg_shift))
    for g in range(A_GROUPS):
        cols = slice(g * A_HEAD, (g + 1) * A_HEAD)
        u_g = jax.nn.gelu(_dot(h, win_ref[:, cols]))
        ws = jnp.where(mask, ws_ref[0, g], 0.0).astype(BF16)
        bias = jnp.concatenate([bs_ref[0, g]] * (A_HEAD // LANES), axis=1)
        for ch in range(TM // A_CHUNK):
            rows = slice(ch * A_CHUNK, (ch + 1) * A_CHUNK)
            s = _dot(ws, vn_s[rows, cols]) + bias
            y_s[rows, cols] = (u_g[rows] * s).astype(BF16)
    o_ref[...] = x + _dot(y_s[...], wout_ref[...])


def _gmlp_mixer(x, g, w_in, ln_g, ln_b, w_s, b_s, w_out):
    ws2 = jnp.stack([w_s, jnp.tile(w_s[:, :SAMPLE_SEQ, :SAMPLE_SEQ], (1, 2, 2))])
    b2 = jnp.stack([b_s, jnp.tile(b_s[:, :SAMPLE_SEQ], (1, 2))])
    b2 = jnp.broadcast_to(b2[..., None], (2, A_GROUPS, A_CHUNK, LANES))
    return pl.pallas_call(
        _gmlp_kernel,
        grid=(N_TILES,),
        in_specs=[
            pl.BlockSpec((TM, D), lambda i: (i, 0)),
            _const_spec((1, D)),
            _const_spec((D, 2 * A_HALF)),
            _const_spec((1, A_HALF)),
            _const_spec((1, A_HALF)),
            pl.BlockSpec((1, A_GROUPS, A_CHUNK, A_CHUNK), lambda i: (i // N_PROMPT_TILES, 0, 0, 0)),
            pl.BlockSpec((1, A_GROUPS, A_CHUNK, LANES), lambda i: (i // N_PROMPT_TILES, 0, 0, 0)),
            _const_spec((A_HALF, D)),
        ],
        out_specs=[
            pl.BlockSpec((TM, D), lambda i: (i, 0)),
            pl.BlockSpec((TM, A_HALF), lambda i: (0, 0)),
        ],
        out_shape=[
            jax.ShapeDtypeStruct((N_TOK, D), F32),
            jax.ShapeDtypeStruct((N_SAMPLE_TOK, A_HALF), F32),
        ],
        scratch_shapes=[pltpu.VMEM((TM, A_HALF), BF16), pltpu.VMEM((TM, A_HALF), BF16)],
        compiler_params=pltpu.CompilerParams(dimension_semantics=("arbitrary",), vmem_limit_bytes=VMEM_LIMIT),
        name="gmlp_mixer",
    )(x, g.reshape(1, D), w_in.astype(BF16), ln_g.reshape(1, A_HALF), ln_b.reshape(1, A_HALF),
      ws2, b2, w_out.astype(BF16))


def _sconv_kernel(x_ref, g_ref, win_ref, cw_ref, wout_ref, st_ref,
                  o_ref, tailp_ref, tails_ref, full_s, y_s):
    i = pl.program_id(0)
    is_sample = i >= N_PROMPT_TILES
    x = x_ref[...]
    h = _rmsnorm(x, g_ref[...]).astype(BF16)
    p = _dot(h, win_ref[...])
    bg = p[:, :D]
    u = p[:, D:2 * D] * p[:, 2 * D:]
    w = [cw_ref[k:k + 1, :] for k in range(B_WIDTH)]

    def conv(n):
        return sum(w[k] * full_s[pl.ds(B_HALO - (B_WIDTH - 1) + k, n), :] for k in range(B_WIDTH))

    @pl.when(jnp.logical_not(is_sample))
    def _():
        @pl.when(i % TILES_PER_SEQ == 0)
        def _():
            full_s[0:B_HALO, :] = jnp.zeros((B_HALO, D), F32)

        full_s[B_HALO:, :] = u
        y_s[...] = (bg * conv(TM)).astype(BF16)
        tail = full_s[TM:TM + B_HALO, :]
        tailp_ref[0] = tail
        full_s[0:B_HALO, :] = tail

    @pl.when(is_sample)
    def _():
        for b in range(N_SEQ):
            rows = slice(b * SAMPLE_SEQ, (b + 1) * SAMPLE_SEQ)
            full_s[0:B_HALO, :] = st_ref[b]
            full_s[B_HALO:B_HALO + SAMPLE_SEQ, :] = u[rows]
            y_s[rows, :] = (bg[rows] * conv(SAMPLE_SEQ)).astype(BF16)
            tails_ref[b] = full_s[SAMPLE_SEQ:SAMPLE_SEQ + B_HALO, :]
        tailp_ref[0] = jnp.zeros((B_HALO, D), F32)

    o_ref[...] = x + _dot(y_s[...], wout_ref[...])


def _sconv_mixer(x, g, w_in, w_conv, w_out, state):
    st = jnp.pad(state, ((0, 0), (B_HALO - (B_WIDTH - 1), 0), (0, 0)))
    return pl.pallas_call(
        _sconv_kernel,
        grid=(N_TILES,),
        in_specs=[
            pl.BlockSpec((TM, D), lambda i: (i, 0)),
            _const_spec((1, D)),
            _const_spec((D, 3 * D)),
            _const_spec((B_WIDTH, D)),
            _const_spec((D, D)),
            _const_spec((N_SEQ, B_HALO, D)),
        ],
        out_specs=[
            pl.BlockSpec((TM, D), lambda i: (i, 0)),
            pl.BlockSpec((1, B_HALO, D), lambda i: (i, 0, 0)),
            pl.BlockSpec((N_SEQ, B_HALO, D), lambda i: (0, 0, 0)),
        ],
        out_shape=[
            jax.ShapeDtypeStruct((N_TOK, D), F32),
            jax.ShapeDtypeStruct((N_TILES, B_HALO, D), F32),
            jax.ShapeDtypeStruct((N_SEQ, B_HALO, D), F32),
        ],
        scratch_shapes=[pltpu.VMEM((B_HALO + TM, D), F32), pltpu.VMEM((TM, D), BF16)],
        compiler_params=pltpu.CompilerParams(dimension_semantics=("arbitrary",), vmem_limit_bytes=VMEM_LIMIT),
        name="sconv_mixer",
    )(x, g.reshape(1, D), w_in.astype(BF16), w_conv, w_out.astype(BF16), st)


C_ROW_BLOCK = 16


def _conf_kernel(x_ref, g_ref, w1_ref, b1_ref, dw_ref, bdw_ref, lng_ref, lnb_ref, w2_ref, b2_ref, st_ref,
                 o_ref, tailp_ref, tails_ref, full_s, shift_s, conv_s, y_s):
    i = pl.program_id(0)
    is_sample = i >= N_PROMPT_TILES
    x = x_ref[...]
    h = _rmsnorm(x, g_ref[...]).astype(BF16)
    p = _dot(h, w1_ref[...]) + b1_ref[...]
    gl = p[:, :D] * jax.nn.sigmoid(p[:, D:])

    def conv(n, out_row0):
        n_in = C_HALO + n
        for r in range(1, SUBLANES):
            shift_s[r - 1, 0:n_in - SUBLANES, :] = full_s[r:r + n_in - SUBLANES, :]

        def block(rb, carry):
            base = pl.multiple_of(rb * C_ROW_BLOCK, C_ROW_BLOCK)
            acc = [jnp.zeros((SUBLANES, D), F32) for _ in range(C_ROW_BLOCK // SUBLANES)]
            for k in range(C_WIDTH):
                off = k + (C_HALO - (C_WIDTH - 1))
                row0 = base + (off // SUBLANES) * SUBLANES
                if off % SUBLANES == 0:
                    tap = full_s[pl.ds(row0, C_ROW_BLOCK), :]
                else:
                    tap = shift_s[off % SUBLANES - 1, pl.ds(row0, C_ROW_BLOCK), :]
                wk = dw_ref[k]
                for a in range(len(acc)):
                    acc[a] = acc[a] + wk * tap[a * SUBLANES:(a + 1) * SUBLANES]
            conv_s[pl.ds(out_row0 + base, C_ROW_BLOCK), :] = jnp.concatenate(acc, axis=0)
            return carry

        lax.fori_loop(0, n // C_ROW_BLOCK, block, 0)

    @pl.when(jnp.logical_not(is_sample))
    def _():
        @pl.when(i % TILES_PER_SEQ == 0)
        def _():
            full_s[0:C_HALO, :] = jnp.zeros((C_HALO, D), F32)

        full_s[C_HALO:, :] = gl
        conv(TM, 0)
        tail = full_s[TM:TM + C_HALO, :]
        tailp_ref[0] = tail
        full_s[0:C_HALO, :] = tail

    @pl.when(is_sample)
    def _():
        for b in range(N_SEQ):
            full_s[0:C_HALO, :] = st_ref[b]
            full_s[C_HALO:C_HALO + SAMPLE_SEQ, :] = gl[b * SAMPLE_SEQ:(b + 1) * SAMPLE_SEQ]
            conv(SAMPLE_SEQ, b * SAMPLE_SEQ)
            tails_ref[b] = full_s[SAMPLE_SEQ:SAMPLE_SEQ + C_HALO, :]
        tailp_ref[0] = jnp.zeros((C_HALO, D), F32)

    y = _layernorm(conv_s[...] + bdw_ref[...], lng_ref[...], lnb_ref[...])
    y_s[...] = (y * jax.nn.sigmoid(y)).astype(BF16)
    o_ref[...] = x + _dot(y_s[...], w2_ref[...]) + b2_ref[...]


def _conf_mixer(x, g, w_pw1, b_pw1, w_dw, b_dw, ln_g, ln_b, w_pw2, b_pw2, state):
    st = jnp.pad(state, ((0, 0), (C_HALO - (C_WIDTH - 1), 0), (0, 0)))
    dw = jnp.broadcast_to(w_dw[:, None, :], (C_WIDTH, SUBLANES, D))
    return pl.pallas_call(
        _conf_kernel,
        grid=(N_TILES,),
        in_specs=[
            pl.BlockSpec((TM, D), lambda i: (i, 0)),
            _const_spec((1, D)),
            _const_spec((D, 2 * D)),
            _const_spec((1, 2 * D)),
            _const_spec((C_WIDTH, SUBLANES, D)),
            _const_spec((1, D)),
            _const_spec((1, D)),
            _const_spec((1, D)),
            _const_spec((D, D)),
            _const_spec((1, D)),
            _const_spec((N_SEQ, C_HALO, D)),
        ],
        out_specs=[
            pl.BlockSpec((TM, D), lambda i: (i, 0)),
            pl.BlockSpec((1, C_HALO, D), lambda i: (i, 0, 0)),
            pl.BlockSpec((N_SEQ, C_HALO, D), lambda i: (0, 0, 0)),
        ],
        out_shape=[
            jax.ShapeDtypeStruct((N_TOK, D), F32),
            jax.ShapeDtypeStruct((N_TILES, C_HALO, D), F32),
            jax.ShapeDtypeStruct((N_SEQ, C_HALO, D), F32),
        ],
        scratch_shapes=[
            pltpu.VMEM((C_HALO + TM, D), F32),
            pltpu.VMEM((SUBLANES - 1, C_HALO + TM, D), F32),
            pltpu.VMEM((TM, D), F32),
            pltpu.VMEM((TM, D), BF16),
        ],
        compiler_params=pltpu.CompilerParams(dimension_semantics=("arbitrary",), vmem_limit_bytes=VMEM_LIMIT),
        name="conf_mixer",
    )(x, g.reshape(1, D), w_pw1.astype(BF16), b_pw1.reshape(1, 2 * D), dw, b_dw.reshape(1, D),
      ln_g.reshape(1, D), ln_b.reshape(1, D), w_pw2.astype(BF16), b_pw2.reshape(1, D), st)


def _swiglu_hidden(h, wg_ref, wu_ref, hid_s, lead=()):
    for c0, cn in FF_CHUNKS:
        a = _dot(h, wg_ref[lead + (slice(None), slice(c0, c0 + cn))])
        b = _dot(h, wu_ref[lead + (slice(None), slice(c0, c0 + cn))])
        hid_s[:, c0:c0 + cn] = (a * jax.nn.sigmoid(a) * b).astype(BF16)


def _ffn_kernel(x_ref, g_ref, wg_ref, wu_ref, wd_ref, o_ref, hid_s):
    x = x_ref[...]
    h = _rmsnorm(x, g_ref[...]).astype(BF16)
    _swiglu_hidden(h, wg_ref, wu_ref, hid_s)
    o_ref[...] = x + _dot(hid_s[...], wd_ref[...])


def _dense_ffn(x, g, w_gate, w_up, w_down):
    return pl.pallas_call(
        _ffn_kernel,
        grid=(N_TILES,),
        in_specs=[
            pl.BlockSpec((TM, D), lambda i: (i, 0)),
            _const_spec((1, D)),
            _const_spec((D, D_FF)),
            _const_spec((D, D_FF)),
            _const_spec((D_FF, D)),
        ],
        out_specs=pl.BlockSpec((TM, D), lambda i: (i, 0)),
        out_shape=jax.ShapeDtypeStruct((N_TOK, D), F32),
        scratch_shapes=[pltpu.VMEM((TM, D_FF), BF16)],
        compiler_params=pltpu.CompilerParams(dimension_semantics=("arbitrary",), vmem_limit_bytes=VMEM_LIMIT),
        name="dense_ffn",
    )(x, g.reshape(1, D), w_gate.astype(BF16), w_up.astype(BF16), w_down.astype(BF16))


def _router_kernel(x_ref, g_ref, wr_ref, meta_ref, wts_ref, cnt_ref, run_s):
    i = pl.program_id(0)

    @pl.when(i == 0)
    def _():
        run_s[...] = jnp.zeros((1, LANES), F32)

    h = _rmsnorm(x_ref[...], g_ref[...])
    logits = jnp.dot(h, wr_ref[...], preferred_element_type=F32, precision=lax.Precision.HIGHEST)
    lane = lax.broadcasted_iota(jnp.int32, (TM, LANES), 1)
    lane_f = lane.astype(F32)
    neg = jnp.float32(-jnp.inf)
    logits = jnp.where(lane < N_EXPERTS, logits, neg)
    l1 = jnp.max(logits, axis=-1, keepdims=True)
    e1 = jnp.min(jnp.where(logits == l1, lane_f, float(LANES)), axis=-1, keepdims=True).astype(jnp.int32)
    rest = jnp.where(lane == e1, neg, logits)
    l2 = jnp.max(rest, axis=-1, keepdims=True)
    e2 = jnp.min(jnp.where(rest == l2, lane_f, float(LANES)), axis=-1, keepdims=True).astype(jnp.int32)
    t = jnp.exp(l2 - l1)
    w1 = 1.0 / (1.0 + t)
    w2 = t * w1
    sel = jnp.logical_or(lane == e1, lane == e2)
    onehot = jnp.where(sel, 1.0, 0.0)
    r = lax.broadcasted_iota(jnp.int32, (TM, TM), 0)
    c = lax.broadcasted_iota(jnp.int32, (TM, TM), 1)
    below = jnp.where(r > c, 1.0, 0.0).astype(BF16)
    before = _dot(below, onehot.astype(BF16)) + run_s[...]
    r1 = jnp.sum(jnp.where(lane == e1, before, 0.0), axis=-1, keepdims=True).astype(jnp.int32)
    r2 = jnp.sum(jnp.where(lane == e2, before, 0.0), axis=-1, keepdims=True).astype(jnp.int32)
    run_s[...] = run_s[...] + jnp.sum(onehot, axis=0, keepdims=True)
    meta = jnp.where(lane == 0, e1, jnp.where(lane == 1, e2, jnp.where(lane == 2, r1, jnp.where(lane == 3, r2, 0))))
    meta_ref[...] = meta
    wts_ref[...] = jnp.where(lane == 0, w1, jnp.where(lane == 1, w2, 0.0))
    cnt_ref[...] = run_s[...].astype(jnp.int32)


def _router(x, g, w_router):
    wr = jnp.pad(w_router, ((0, 0), (0, LANES - N_EXPERTS)))
    return pl.pallas_call(
        _router_kernel,
        grid=(N_TILES,),
        in_specs=[
            pl.BlockSpec((TM, D), lambda i: (i, 0)),
            _const_spec((1, D)),
            _const_spec((D, LANES)),
        ],
        out_specs=[
            pl.BlockSpec((TM, LANES), lambda i: (i, 0)),
            pl.BlockSpec((TM, LANES), lambda i: (i, 0)),
            pl.BlockSpec((1, LANES), lambda i: (0, 0)),
        ],
        out_shape=[
            jax.ShapeDtypeStruct((N_TOK, LANES), jnp.int32),
            jax.ShapeDtypeStruct((N_TOK, LANES), F32),
            jax.ShapeDtypeStruct((1, LANES), jnp.int32),
        ],
        scratch_shapes=[pltpu.VMEM((1, LANES), F32)],
        compiler_params=pltpu.CompilerParams(dimension_semantics=("arbitrary",), vmem_limit_bytes=VMEM_LIMIT),
        name="moe_router",
    )(x, g.reshape(1, D), wr)


def _row_copy(src_hbm, src_row, dst_hbm, dst_row, sem):
    return pltpu.make_async_copy(src_hbm.at[pl.ds(src_row, 1), :], dst_hbm.at[pl.ds(dst_row, 1), :], sem)


def _dispatch_kernel(dest_ref, x_hbm, zeros_hbm, xs_hbm, sem):
    del zeros_hbm
    base = pl.program_id(0) * TM

    def issue(t, carry):
        _row_copy(x_hbm, base + t, xs_hbm, dest_ref[0, 0, 2 * t], sem).start()
        _row_copy(x_hbm, base + t, xs_hbm, dest_ref[0, 0, 2 * t + 1], sem).start()
        return carry

    lax.fori_loop(0, TM, issue, 0)

    def drain(t, carry):
        _row_copy(x_hbm, 0, xs_hbm, 0, sem).wait()
        _row_copy(x_hbm, 0, xs_hbm, 0, sem).wait()
        return carry

    lax.fori_loop(0, TM, drain, 0)


def _dispatch(x, dest):
    return pl.pallas_call(
        _dispatch_kernel,
        grid=(N_TILES,),
        in_specs=[
            pl.BlockSpec((1, 1, 2 * TM), lambda i: (i, 0, 0), memory_space=pltpu.SMEM),
            pl.BlockSpec(memory_space=pl.ANY),
            pl.BlockSpec(memory_space=pl.ANY),
        ],
        out_specs=pl.BlockSpec(memory_space=pl.ANY),
        out_shape=jax.ShapeDtypeStruct((N_ROWS, D), F32),
        scratch_shapes=[pltpu.SemaphoreType.DMA(())],
        input_output_aliases={2: 0},
        compiler_params=pltpu.CompilerParams(dimension_semantics=("arbitrary",), has_side_effects=True),
        name="moe_dispatch",
    )(dest.reshape(N_TILES, 1, 2 * TM), x, jnp.zeros((N_ROWS, D), F32))


def _gmm_kernel(be_ref, bu_ref, xs_ref, g_ref, wg_ref, wu_ref, wd_ref, ys_ref, hid_s):
    i = pl.program_id(0)

    @pl.when(bu_ref[i] > 0)
    def _():
        h = _rmsnorm(xs_ref[...], g_ref[...]).astype(BF16)
        _swiglu_hidden(h, wg_ref, wu_ref, hid_s, lead=(0,))
        ys_ref[...] = _dot(hid_s[...], wd_ref[0])

    @pl.when(bu_ref[i] == 0)
    def _():
        ys_ref[...] = jnp.zeros((BM, D), F32)


def _gmm(xs, g, block_expert, block_used, w_gate, w_up, w_down):
    wspec = lambda shape: pl.BlockSpec((1,) + shape, lambda i, be, bu: (be[i], 0, 0))
    return pl.pallas_call(
        _gmm_kernel,
        grid_spec=pltpu.PrefetchScalarGridSpec(
            num_scalar_prefetch=2,
            grid=(N_BLOCKS,),
            in_specs=[
                pl.BlockSpec((BM, D), lambda i, be, bu: (i, 0)),
                pl.BlockSpec((1, D), lambda i, be, bu: (0, 0)),
                wspec((D, D_FF)),
                wspec((D, D_FF)),
                wspec((D_FF, D)),
            ],
            out_specs=pl.BlockSpec((BM, D), lambda i, be, bu: (i, 0)),
            scratch_shapes=[pltpu.VMEM((BM, D_FF), BF16)],
        ),
        out_shape=jax.ShapeDtypeStruct((N_ROWS, D), F32),
        compiler_params=pltpu.CompilerParams(dimension_semantics=("arbitrary",), vmem_limit_bytes=VMEM_LIMIT),
        name="moe_gmm",
    )(block_expert, block_used, xs, g.reshape(1, D), w_gate.astype(BF16), w_up.astype(BF16), w_down.astype(BF16))


def _combine_kernel(dest_ref, x_ref, wts_ref, gf_ref, ys_hbm, o_ref, buf, sem, *, final_norm):
    def issue(t, carry):
        _row_copy(ys_hbm, dest_ref[0, 0, 2 * t], buf.at[0], t, sem).start()
        _row_copy(ys_hbm, dest_ref[0, 0, 2 * t + 1], buf.at[1], t, sem).start()
        return carry

    lax.fori_loop(0, TM, issue, 0)

    def drain(t, carry):
        _row_copy(ys_hbm, 0, buf.at[0], 0, sem).wait()
        _row_copy(ys_hbm, 0, buf.at[1], 0, sem).wait()
        return carry

    lax.fori_loop(0, TM, drain, 0)
    w = wts_ref[...]
    y = x_ref[...] + w[:, 0:1] * buf[0] + w[:, 1:2] * buf[1]
    if final_norm:
        y = _rmsnorm(y, gf_ref[...])
    o_ref[...] = y


def _combine(x, dest, wts, ys, g_final, final_norm):
    return pl.pallas_call(
        functools.partial(_combine_kernel, final_norm=final_norm),
        grid=(N_TILES,),
        in_specs=[
            pl.BlockSpec((1, 1, 2 * TM), lambda i: (i, 0, 0), memory_space=pltpu.SMEM),
            pl.BlockSpec((TM, D), lambda i: (i, 0)),
            pl.BlockSpec((TM, LANES), lambda i: (i, 0)),
            _const_spec((1, D)),
            pl.BlockSpec(memory_space=pl.ANY),
        ],
        out_specs=pl.BlockSpec((TM, D), lambda i: (i, 0)),
        out_shape=jax.ShapeDtypeStruct((N_TOK, D), F32),
        scratch_shapes=[pltpu.VMEM((2, TM, D), F32), pltpu.SemaphoreType.DMA(())],
        compiler_params=pltpu.CompilerParams(dimension_semantics=("arbitrary",), vmem_limit_bytes=VMEM_LIMIT),
        name="moe_combine",
    )(dest.reshape(N_TILES, 1, 2 * TM), x, wts, g_final.reshape(1, D), ys)


def _moe_ffn(x, g, w_router, w_gate, w_up, w_down, g_final, final_norm):
    meta, wts, counts = _router(x, g, w_router)
    counts = counts[0, :N_EXPERTS]
    blocks = (counts + BM - 1) // BM
    block_end = jnp.cumsum(blocks)
    seg_start = (block_end - blocks) * BM
    dest = seg_start[meta[:, 0:2]] + meta[:, 2:4]
    bidx = jnp.arange(N_BLOCKS, dtype=jnp.int32)
    block_expert = jnp.minimum(jnp.sum(bidx[:, None] >= block_end[None, :], axis=1), N_EXPERTS - 1).astype(jnp.int32)
    block_used = (bidx < block_end[-1]).astype(jnp.int32)
    xs = _dispatch(x, dest)
    ys = _gmm(xs, g, block_expert, block_used, w_gate, w_up, w_down)
    return _combine(x, dest, wts, ys, g_final, final_norm)


def kernel(x_prompt, x_sample, state_conv_b, state_conv_c, norm_mix, norm_ffn, norm_final, a_w_in, a_ln_g, a_ln_b, a_w_s, a_b_s, a_w_out, b_w_in, b_conv, b_w_out, c_w_pw1, c_b_pw1, c_dw, c_b_dw, c_ln_g, c_ln_b, c_w_pw2, c_b_pw2, f_w_gate, f_w_up, f_w_down, m_router, m_w_gate, m_w_up, m_w_down):
    x = jnp.concatenate([x_prompt.reshape(N_PROMPT_TOK, D), x_sample.reshape(N_SAMPLE_TOK, D)], axis=0)
    last_tile = jnp.arange(N_SEQ) * TILES_PER_SEQ + TILES_PER_SEQ - 1

    x, v0 = _gmlp_mixer(x, norm_mix[0], a_w_in[0], a_ln_g[0], a_ln_b[0], a_w_s[0], a_b_s[0], a_w_out[0])
    x = _dense_ffn(x, norm_ffn[0], f_w_gate[0], f_w_up[0], f_w_down[0])
    x, tb_p, tb_s = _sconv_mixer(x, norm_mix[1], b_w_in[0], b_conv[0], b_w_out[0], state_conv_b[0])
    x = _moe_ffn(x, norm_ffn[1], m_router[0], m_w_gate[0], m_w_up[0], m_w_down[0], norm_final, False)
    x, tc_p, tc_s = _conf_mixer(x, norm_mix[2], c_w_pw1[0], c_b_pw1[0], c_dw[0], c_b_dw[0], c_ln_g[0], c_ln_b[0],
                                c_w_pw2[0], c_b_pw2[0], state_conv_c[0])
    x = _dense_ffn(x, norm_ffn[2], f_w_gate[1], f_w_up[1], f_w_down[1])
    x, v1 = _gmlp_mixer(x, norm_mix[3], a_w_in[1], a_ln_g[1], a_ln_b[1], a_w_s[1], a_b_s[1], a_w_out[1])
    y = _moe_ffn(x, norm_ffn[3], m_router[1], m_w_gate[1], m_w_up[1], m_w_down[1], norm_final, True)

    y_prompt = y[:N_PROMPT_TOK].reshape(N_SEQ, SEQ, D)
    y_sample = y[N_PROMPT_TOK:].reshape(N_SEQ, SAMPLE_SEQ, D)
    nb = B_WIDTH - 1
    nc = C_WIDTH - 1
    new_b_p = tb_p[last_tile, B_HALO - nb:, :][None]
    new_b_s = tb_s[:, B_HALO - nb:, :][None]
    new_c_p = tc_p[last_tile, C_HALO - nc:, :][None]
    new_c_s = tc_s[:, C_HALO - nc:, :][None]
    new_v = jnp.stack([v0, v1]).reshape(2, N_SEQ, SAMPLE_SEQ, A_HALF)
    return (y_prompt, y_sample, new_b_p, new_b_s, new_c_p, new_c_s, new_v)
```

```python
import functools

import jax
import jax.numpy as jnp
from jax import lax
from jax.experimental import pallas as pl
from jax.experimental.pallas import tpu as pltpu

F32 = jnp.float32
BF16 = jnp.bfloat16

D = 1024
TM = 512
SEQ = 4096
SAMPLE_SEQ = 64
N_SEQ = 8
TILES_PER_SEQ = SEQ // TM
N_PROMPT_TOK = N_SEQ * SEQ
N_SAMPLE_TOK = N_SEQ * SAMPLE_SEQ
N_TOK = N_PROMPT_TOK + N_SAMPLE_TOK
N_PROMPT_TILES = N_PROMPT_TOK // TM
N_TILES = N_TOK // TM
assert N_SAMPLE_TOK == TM and N_TILES == N_PROMPT_TILES + 1

A_HALF = 2 * D
A_GROUPS = 8
A_HEAD = A_HALF // A_GROUPS
A_CHUNK = 128
B_WIDTH = 3
C_WIDTH = 31
B_HALO = 8
C_HALO = 32
D_FF = 2816
N_EXPERTS = 8
LANES = 128
SUBLANES = 8
MXU_COLS = 256
RMS_EPS = 1e-6
LN_EPS = 1e-5

BM = 512
N_ROWS = 2 * N_TOK + N_EXPERTS * BM
N_BLOCKS = N_ROWS // BM
FF_CHUNKS = ((0, 1024), (1024, 1024), (2048, 768))

VMEM_LIMIT = 56 * 1024 * 1024


def _const_spec(shape):
    return pl.BlockSpec(shape, lambda *_: (0,) * len(shape), pipeline_mode=pl.Buffered(1))


def _rmsnorm(x, g):
    return x * lax.rsqrt(jnp.mean(x * x, axis=-1, keepdims=True) + RMS_EPS) * g


def _layernorm(x, g, b):
    mu = jnp.mean(x, axis=-1, keepdims=True)
    xc = x - mu
    var = jnp.mean(xc * xc, axis=-1, keepdims=True)
    return xc * lax.rsqrt(var + LN_EPS) * g + b


def _dot(a, b):
    return jnp.dot(a, b, preferred_element_type=F32)


def _gmlp_kernel(xp_ref, xs_ref, g_ref, win_ref, lng_ref, lnb_ref, ws_ref, bs_ref, wout_ref,
                 o_ref, v_ref, vn_s, y_s):
    is_sample = pl.program_id(0) >= N_PROMPT_TILES
    x = jnp.where(is_sample, xs_ref[...], xp_ref[...])
    h = _rmsnorm(x, g_ref[...]).astype(BF16)
    v = jax.nn.gelu(_dot(h, win_ref[:, A_HALF:]))
    vn = _layernorm(v, lng_ref[...], lnb_ref[...])

    @pl.when(is_sample)
    def _():
        v_ref[...] = vn

    vn_s[...] = vn.astype(BF16)
    r = lax.broadcasted_iota(jnp.int32, (A_CHUNK, A_CHUNK), 0)
    c = lax.broadcasted_iota(jnp.int32, (A_CHUNK, A_CHUNK), 1)
    seg_shift = jnp.where(is_sample, 6, 7)
    mask = (r >= c) & ((r >> seg_shift) == (c >> seg_shift))
    for g in range(A_GROUPS):
        cols = slice(g * A_HEAD, (g + 1) * A_HEAD)
        u_g = jax.nn.gelu(_dot(h, win_ref[:, cols]))
        ws = jnp.where(mask, ws_ref[0, g], 0.0).astype(BF16)
        bias = jnp.concatenate([bs_ref[0, g]] * (A_HEAD // LANES), axis=1)
        for ch in range(TM // A_CHUNK):
            rows = slice(ch * A_CHUNK, (ch + 1) * A_CHUNK)
            s = _dot(ws, vn_s[rows, cols]) + bias
            y_s[rows, cols] = (u_g[rows] * s).astype(BF16)
    o_ref[...] = x + _dot(y_s[...], wout_ref[...])


def _gmlp_mixer(x_prompt, x_sample, sample_block, g, w_in, ln_g, ln_b, w_s, b_s, w_out):
    ws2 = jnp.stack([w_s, jnp.tile(w_s[:, :SAMPLE_SEQ, :SAMPLE_SEQ], (1, 2, 2))])
    b2 = jnp.stack([b_s, jnp.tile(b_s[:, :SAMPLE_SEQ], (1, 2))])
    b2 = jnp.broadcast_to(b2[..., None], (2, A_GROUPS, A_CHUNK, LANES))
    return pl.pallas_call(
        _gmlp_kernel,
        grid=(N_TILES,),
        in_specs=[
            pl.BlockSpec((TM, D), lambda i: (jnp.minimum(i, N_PROMPT_TILES - 1), 0)),
            pl.BlockSpec((TM, D), lambda i: (sample_block, 0)),
            _const_spec((1, D)),
            _const_spec((D, 2 * A_HALF)),
            _const_spec((1, A_HALF)),
            _const_spec((1, A_HALF)),
            pl.BlockSpec((1, A_GROUPS, A_CHUNK, A_CHUNK), lambda i: (i // N_PROMPT_TILES, 0, 0, 0)),
            pl.BlockSpec((1, A_GROUPS, A_CHUNK, LANES), lambda i: (i // N_PROMPT_TILES, 0, 0, 0)),
            _const_spec((A_HALF, D)),
        ],
        out_specs=[
            pl.BlockSpec((TM, D), lambda i: (i, 0)),
            pl.BlockSpec((TM, A_HALF), lambda i: (0, 0)),
        ],
        out_shape=[
            jax.ShapeDtypeStruct((N_TOK, D), F32),
            jax.ShapeDtypeStruct((N_SAMPLE_TOK, A_HALF), F32),
        ],
        scratch_shapes=[pltpu.VMEM((TM, A_HALF), BF16), pltpu.VMEM((TM, A_HALF), BF16)],
        compiler_params=pltpu.CompilerParams(dimension_semantics=("arbitrary",), vmem_limit_bytes=VMEM_LIMIT),
        name="gmlp_mixer",
    )(x_prompt, x_sample, g.reshape(1, D), w_in.astype(BF16), ln_g.reshape(1, A_HALF), ln_b.reshape(1, A_HALF),
      ws2, b2, w_out.astype(BF16))


def _sconv_kernel(x_ref, g_ref, win_ref, cw_ref, wout_ref, st_ref,
                  o_ref, tailp_ref, tails_ref, full_s, y_s):
    i = pl.program_id(0)
    is_sample = i >= N_PROMPT_TILES
    x = x_ref[...]
    h = _rmsnorm(x, g_ref[...]).astype(BF16)
    p = _dot(h, win_ref[...])
    bg = p[:, :D]
    u = p[:, D:2 * D] * p[:, 2 * D:]
    w = [cw_ref[k:k + 1, :] for k in range(B_WIDTH)]

    def conv(n):
        return sum(w[k] * full_s[pl.ds(B_HALO - (B_WIDTH - 1) + k, n), :] for k in range(B_WIDTH))

    @pl.when(jnp.logical_not(is_sample))
    def _():
        @pl.when(i % TILES_PER_SEQ == 0)
        def _():
            full_s[0:B_HALO, :] = jnp.zeros((B_HALO, D), F32)

        full_s[B_HALO:, :] = u
        y_s[...] = (bg * conv(TM)).astype(BF16)
        tail = full_s[TM:TM + B_HALO, :]
        tailp_ref[0] = tail
        full_s[0:B_HALO, :] = tail

    @pl.when(is_sample)
    def _():
        for b in range(N_SEQ):
            rows = slice(b * SAMPLE_SEQ, (b + 1) * SAMPLE_SEQ)
            full_s[0:B_HALO, :] = st_ref[b]
            full_s[B_HALO:B_HALO + SAMPLE_SEQ, :] = u[rows]
            y_s[rows, :] = (bg[rows] * conv(SAMPLE_SEQ)).astype(BF16)
            tails_ref[b] = full_s[SAMPLE_SEQ:SAMPLE_SEQ + B_HALO, :]
        tailp_ref[0] = jnp.zeros((B_HALO, D), F32)

    o_ref[...] = x + _dot(y_s[...], wout_ref[...])


def _sconv_mixer(x, g, w_in, w_conv, w_out, state):
    st = jnp.pad(state, ((0, 0), (B_HALO - (B_WIDTH - 1), 0), (0, 0)))
    return pl.pallas_call(
        _sconv_kernel,
        grid=(N_TILES,),
        in_specs=[
            pl.BlockSpec((TM, D), lambda i: (i, 0)),
            _const_spec((1, D)),
            _const_spec((D, 3 * D)),
            _const_spec((B_WIDTH, D)),
            _const_spec((D, D)),
            _const_spec((N_SEQ, B_HALO, D)),
        ],
        out_specs=[
            pl.BlockSpec((TM, D), lambda i: (i, 0)),
            pl.BlockSpec((1, B_HALO, D), lambda i: (i, 0, 0)),
            pl.BlockSpec((N_SEQ, B_HALO, D), lambda i: (0, 0, 0)),
        ],
        out_shape=[
            jax.ShapeDtypeStruct((N_TOK, D), F32),
            jax.ShapeDtypeStruct((N_TILES, B_HALO, D), F32),
            jax.ShapeDtypeStruct((N_SEQ, B_HALO, D), F32),
        ],
        scratch_shapes=[pltpu.VMEM((B_HALO + TM, D), F32), pltpu.VMEM((TM, D), BF16)],
        compiler_params=pltpu.CompilerParams(dimension_semantics=("arbitrary",), vmem_limit_bytes=VMEM_LIMIT),
        name="sconv_mixer",
    )(x, g.reshape(1, D), w_in.astype(BF16), w_conv, w_out.astype(BF16), st)


C_ROW_BLOCK = 16


def _conf_kernel(x_ref, g_ref, w1_ref, b1_ref, dw_ref, bdw_ref, lng_ref, lnb_ref, w2_ref, b2_ref, st_ref,
                 o_ref, tailp_ref, tails_ref, full_s, shift_s, conv_s, y_s):
    i = pl.program_id(0)
    is_sample = i >= N_PROMPT_TILES
    x = x_ref[...]
    h = _rmsnorm(x, g_ref[...]).astype(BF16)
    p = _dot(h, w1_ref[...]) + b1_ref[...]
    gl = p[:, :D] * jax.nn.sigmoid(p[:, D:])

    def conv(n, out_row0):
        n_in = C_HALO + n
        for r in range(1, SUBLANES):
            shift_s[r - 1, 0:n_in - SUBLANES, :] = full_s[r:r + n_in - SUBLANES, :]

        def block(rb, carry):
            base = pl.multiple_of(rb * C_ROW_BLOCK, C_ROW_BLOCK)
            acc = [jnp.zeros((SUBLANES, D), F32) for _ in range(C_ROW_BLOCK // SUBLANES)]
            for k in range(C_WIDTH):
                off = k + (C_HALO - (C_WIDTH - 1))
                row0 = base + (off // SUBLANES) * SUBLANES
                if off % SUBLANES == 0:
                    tap = full_s[pl.ds(row0, C_ROW_BLOCK), :]
                else:
                    tap = shift_s[off % SUBLANES - 1, pl.ds(row0, C_ROW_BLOCK), :]
                wk = dw_ref[k]
                for a in range(len(acc)):
                    acc[a] = acc[a] + wk * tap[a * SUBLANES:(a + 1) * SUBLANES]
            conv_s[pl.ds(out_row0 + base, C_ROW_BLOCK), :] = jnp.concatenate(acc, axis=0)
            return carry

        lax.fori_loop(0, n // C_ROW_BLOCK, block, 0)

    @pl.when(jnp.logical_not(is_sample))
    def _():
        @pl.when(i % TILES_PER_SEQ == 0)
        def _():
            full_s[0:C_HALO, :] = jnp.zeros((C_HALO, D), F32)

        full_s[C_HALO:, :] = gl
        conv(TM, 0)
        tail = full_s[TM:TM + C_HALO, :]
        tailp_ref[0] = tail
        full_s[0:C_HALO, :] = tail

    @pl.when(is_sample)
    def _():
        for b in range(N_SEQ):
            full_s[0:C_HALO, :] = st_ref[b]
            full_s[C_HALO:C_HALO + SAMPLE_SEQ, :] = gl[b * SAMPLE_SEQ:(b + 1) * SAMPLE_SEQ]
            conv(SAMPLE_SEQ, b * SAMPLE_SEQ)
            tails_ref[b] = full_s[SAMPLE_SEQ:SAMPLE_SEQ + C_HALO, :]
        tailp_ref[0] = jnp.zeros((C_HALO, D), F32)

    y = _layernorm(conv_s[...] + bdw_ref[...], lng_ref[...], lnb_ref[...])
    y_s[...] = (y * jax.nn.sigmoid(y)).astype(BF16)
    o_ref[...] = x + _dot(y_s[...], w2_ref[...]) + b2_ref[...]


def _conf_mixer(x, g, w_pw1, b_pw1, w_dw, b_dw, ln_g, ln_b, w_pw2, b_pw2, state):
    st = jnp.pad(state, ((0, 0), (C_HALO - (C_WIDTH - 1), 0), (0, 0)))
    dw = jnp.broadcast_to(w_dw[:, None, :], (C_WIDTH, SUBLANES, D))
    return pl.pallas_call(
        _conf_kernel,
        grid=(N_TILES,),
        in_specs=[
            pl.BlockSpec((TM, D), lambda i: (i, 0)),
            _const_spec((1, D)),
            _const_spec((D, 2 * D)),
            _const_spec((1, 2 * D)),
            _const_spec((C_WIDTH, SUBLANES, D)),
            _const_spec((1, D)),
            _const_spec((1, D)),
            _const_spec((1, D)),
            _const_spec((D, D)),
            _const_spec((1, D)),
            _const_spec((N_SEQ, C_HALO, D)),
        ],
        out_specs=[
            pl.BlockSpec((TM, D), lambda i: (i, 0)),
            pl.BlockSpec((1, C_HALO, D), lambda i: (i, 0, 0)),
            pl.BlockSpec((N_SEQ, C_HALO, D), lambda i: (0, 0, 0)),
        ],
        out_shape=[
            jax.ShapeDtypeStruct((N_TOK, D), F32),
            jax.ShapeDtypeStruct((N_TILES, C_HALO, D), F32),
            jax.ShapeDtypeStruct((N_SEQ, C_HALO, D), F32),
        ],
        scratch_shapes=[
            pltpu.VMEM((C_HALO + TM, D), F32),
            pltpu.VMEM((SUBLANES - 1, C_HALO + TM, D), F32),
            pltpu.VMEM((TM, D), F32),
            pltpu.VMEM((TM, D), BF16),
        ],
        compiler_params=pltpu.CompilerParams(dimension_semantics=("arbitrary",), vmem_limit_bytes=VMEM_LIMIT),
        name="conf_mixer",
    )(x, g.reshape(1, D), w_pw1.astype(BF16), b_pw1.reshape(1, 2 * D), dw, b_dw.reshape(1, D),
      ln_g.reshape(1, D), ln_b.reshape(1, D), w_pw2.astype(BF16), b_pw2.reshape(1, D), st)


def _swiglu_hidden(h, wg_ref, wu_ref, hid_s, lead=()):
    for c0, cn in FF_CHUNKS:
        a = _dot(h, wg_ref[lead + (slice(None), slice(c0, c0 + cn))])
        b = _dot(h, wu_ref[lead + (slice(None), slice(c0, c0 + cn))])
        hid_s[:, c0:c0 + cn] = (a * jax.nn.sigmoid(a) * b).astype(BF16)


def _ffn_kernel(x_ref, g_ref, wg_ref, wu_ref, wd_ref, o_ref, hid_s):
    x = x_ref[...]
    h = _rmsnorm(x, g_ref[...]).astype(BF16)
    _swiglu_hidden(h, wg_ref, wu_ref, hid_s)
    o_ref[...] = x + _dot(hid_s[...], wd_ref[...])


def _dense_ffn(x, g, w_gate, w_up, w_down):
    return pl.pallas_call(
        _ffn_kernel,
        grid=(N_TILES,),
        in_specs=[
            pl.BlockSpec((TM, D), lambda i: (i, 0)),
            _const_spec((1, D)),
            _const_spec((D, D_FF)),
            _const_spec((D, D_FF)),
            _const_spec((D_FF, D)),
        ],
        out_specs=pl.BlockSpec((TM, D), lambda i: (i, 0)),
        out_shape=jax.ShapeDtypeStruct((N_TOK, D), F32),
        scratch_shapes=[pltpu.VMEM((TM, D_FF), BF16)],
        compiler_params=pltpu.CompilerParams(dimension_semantics=("arbitrary",), vmem_limit_bytes=VMEM_LIMIT),
        name="dense_ffn",
    )(x, g.reshape(1, D), w_gate.astype(BF16), w_up.astype(BF16), w_down.astype(BF16))


def _router_kernel(x_ref, g_ref, wr_ref, meta_ref, wts_ref, cnt_ref, run_s):
    i = pl.program_id(0)

    @pl.when(i == 0)
    def _():
        run_s[...] = jnp.zeros((1, LANES), F32)

    h = _rmsnorm(x_ref[...], g_ref[...])
    logits = jnp.dot(h, wr_ref[...], preferred_element_type=F32, precision=lax.Precision.HIGHEST)
    lane = lax.broadcasted_iota(jnp.int32, (TM, LANES), 1)
    lane_f = lane.astype(F32)
    neg = jnp.float32(-jnp.inf)
    logits = jnp.where(lane < N_EXPERTS, logits, neg)
    l1 = jnp.max(logits, axis=-1, keepdims=True)
    e1 = jnp.min(jnp.where(logits == l1, lane_f, float(LANES)), axis=-1, keepdims=True).astype(jnp.int32)
    rest = jnp.where(lane == e1, neg, logits)
    l2 = jnp.max(rest, axis=-1, keepdims=True)
    e2 = jnp.min(jnp.where(rest == l2, lane_f, float(LANES)), axis=-1, keepdims=True).astype(jnp.int32)
    t = jnp.exp(l2 - l1)
    w1 = 1.0 / (1.0 + t)
    w2 = t * w1
    sel = jnp.logical_or(lane == e1, lane == e2)
    onehot = jnp.where(sel, 1.0, 0.0)
    r = lax.broadcasted_iota(jnp.int32, (TM, TM), 0)
    c = lax.broadcasted_iota(jnp.int32, (TM, TM), 1)
    below = jnp.where(r > c, 1.0, 0.0).astype(BF16)
    before = _dot(below, onehot.astype(BF16)) + run_s[...]
    r1 = jnp.sum(jnp.where(lane == e1, before, 0.0), axis=-1, keepdims=True).astype(jnp.int32)
    r2 = jnp.sum(jnp.where(lane == e2, before, 0.0), axis=-1, keepdims=True).astype(jnp.int32)
    run_s[...] = run_s[...] + jnp.sum(onehot, axis=0, keepdims=True)
    meta = jnp.where(lane == 0, e1, jnp.where(lane == 1, e2, jnp.where(lane == 2, r1, jnp.where(lane == 3, r2, 0))))
    meta_ref[...] = meta
    wts_ref[...] = jnp.where(lane == 0, w1, jnp.where(lane == 1, w2, 0.0))
    cnt_ref[...] = run_s[...].astype(jnp.int32)


def _router(x, g, w_router):
    wr = jnp.pad(w_router, ((0, 0), (0, LANES - N_EXPERTS)))
    return pl.pallas_call(
        _router_kernel,
        grid=(N_TILES,),
        in_specs=[
            pl.BlockSpec((TM, D), lambda i: (i, 0)),
            _const_spec((1, D)),
            _const_spec((D, LANES)),
        ],
        out_specs=[
            pl.BlockSpec((TM, LANES), lambda i: (i, 0)),
            pl.BlockSpec((TM, LANES), lambda i: (i, 0)),
            pl.BlockSpec((1, LANES), lambda i: (0, 0)),
        ],
        out_shape=[
            jax.ShapeDtypeStruct((N_TOK, LANES), jnp.int32),
            jax.ShapeDtypeStruct((N_TOK, LANES), F32),
            jax.ShapeDtypeStruct((1, LANES), jnp.int32),
        ],
        scratch_shapes=[pltpu.VMEM((1, LANES), F32)],
        compiler_params=pltpu.CompilerParams(dimension_semantics=("arbitrary",), vmem_limit_bytes=VMEM_LIMIT),
        name="moe_router",
    )(x, g.reshape(1, D), wr)


def _row_copy(src_hbm, src_row, dst_hbm, dst_row, sem):
    return pltpu.make_async_copy(src_hbm.at[pl.ds(src_row, 1), :], dst_hbm.at[pl.ds(dst_row, 1), :], sem)


DMA_UNROLL = 8


def _dispatch_kernel(last_ref, nblk_ref, nu_ref, dest_ref, x_ref, xs_hbm, zero_s, sem, zsem):
    def zero_block(b):
        return pltpu.make_async_copy(zero_s, xs_hbm.at[pl.ds(b * BM, BM), :], zsem)

    @pl.when(pl.program_id(0) == 0)
    def _():
        zero_s[...] = jnp.zeros((BM, D), F32)
        for start in (True, False):
            for e in range(N_EXPERTS):
                tail = N_BLOCKS - 1 - e
                for cond, b in ((nblk_ref[e] > 0, last_ref[e]), (tail >= nu_ref[0], tail)):
                    cp = zero_block(b)
                    pl.when(cond)(cp.start if start else cp.wait)

    def issue(j, carry):
        for u in range(DMA_UNROLL):
            t = j * DMA_UNROLL + u
            _row_copy(x_ref, t, xs_hbm, dest_ref[0, 0, 2 * t], sem).start()
            _row_copy(x_ref, t, xs_hbm, dest_ref[0, 0, 2 * t + 1], sem).start()
        return carry

    lax.fori_loop(0, TM // DMA_UNROLL, issue, 0)

    def drain(j, carry):
        for _ in range(2 * DMA_UNROLL):
            _row_copy(x_ref, 0, xs_hbm, 0, sem).wait()
        return carry

    lax.fori_loop(0, TM // DMA_UNROLL, drain, 0)


def _dispatch(x, dest, last_block, n_blocks, n_used):
    return pl.pallas_call(
        _dispatch_kernel,
        grid_spec=pltpu.PrefetchScalarGridSpec(
            num_scalar_prefetch=3,
            grid=(N_TILES,),
            in_specs=[
                pl.BlockSpec((1, 1, 2 * TM), lambda i, *_: (i, 0, 0), memory_space=pltpu.SMEM),
                pl.BlockSpec((TM, D), lambda i, *_: (i, 0)),
            ],
            out_specs=pl.BlockSpec(memory_space=pl.ANY),
            scratch_shapes=[pltpu.VMEM((BM, D), F32), pltpu.SemaphoreType.DMA(()), pltpu.SemaphoreType.DMA(())],
        ),
        out_shape=jax.ShapeDtypeStruct((N_ROWS, D), F32),
        compiler_params=pltpu.CompilerParams(dimension_semantics=("arbitrary",), vmem_limit_bytes=VMEM_LIMIT),
        name="moe_dispatch",
    )(last_block, n_blocks, n_used, dest.reshape(N_TILES, 1, 2 * TM), x)


def _gmm_kernel(be_ref, nu_ref, xs_ref, g_ref, wg_ref, wu_ref, wd_ref, ys_ref, hid_s):
    used = pl.program_id(0) < nu_ref[0]

    @pl.when(used)
    def _():
        h = _rmsnorm(xs_ref[...], g_ref[...]).astype(BF16)
        _swiglu_hidden(h, wg_ref, wu_ref, hid_s, lead=(0,))
        ys_ref[...] = _dot(hid_s[...], wd_ref[0])

    @pl.when(jnp.logical_not(used))
    def _():
        ys_ref[...] = jnp.zeros((BM, D), F32)


def _gmm(xs, g, block_expert, n_used, w_gate, w_up, w_down):
    row_map = lambda i, be, nu: (jnp.minimum(i, nu[0] - 1), 0)
    wspec = lambda shape: pl.BlockSpec((1,) + shape, lambda i, be, nu: (be[i], 0, 0))
    return pl.pallas_call(
        _gmm_kernel,
        grid_spec=pltpu.PrefetchScalarGridSpec(
            num_scalar_prefetch=2,
            grid=(N_BLOCKS,),
            in_specs=[
                pl.BlockSpec((BM, D), row_map),
                pl.BlockSpec((1, D), lambda i, be, nu: (0, 0)),
                wspec((D, D_FF)),
                wspec((D, D_FF)),
                wspec((D_FF, D)),
            ],
            out_specs=pl.BlockSpec((BM, D), lambda i, be, nu: (i, 0)),
            scratch_shapes=[pltpu.VMEM((BM, D_FF), BF16)],
        ),
        out_shape=jax.ShapeDtypeStruct((N_ROWS, D), F32),
        compiler_params=pltpu.CompilerParams(dimension_semantics=("arbitrary",), vmem_limit_bytes=VMEM_LIMIT),
        name="moe_gmm",
    )(block_expert, n_used, xs, g.reshape(1, D), w_gate.astype(BF16), w_up.astype(BF16), w_down.astype(BF16))


def _combine_kernel(dest_ref, x_ref, wts_ref, gf_ref, ys_hbm, *rest, final):
    out_refs, (buf, sem) = rest[:-2], rest[-2:]

    def issue(j, carry):
        for u in range(DMA_UNROLL):
            t = j * DMA_UNROLL + u
            _row_copy(ys_hbm, dest_ref[0, 0, 2 * t], buf.at[0], t, sem).start()
            _row_copy(ys_hbm, dest_ref[0, 0, 2 * t + 1], buf.at[1], t, sem).start()
        return carry

    lax.fori_loop(0, TM // DMA_UNROLL, issue, 0)

    def drain(j, carry):
        for _ in range(DMA_UNROLL):
            _row_copy(ys_hbm, 0, buf.at[0], 0, sem).wait()
            _row_copy(ys_hbm, 0, buf.at[1], 0, sem).wait()
        return carry

    lax.fori_loop(0, TM // DMA_UNROLL, drain, 0)
    w = wts_ref[...]
    y = x_ref[...] + w[:, 0:1] * buf[0] + w[:, 1:2] * buf[1]
    if final:
        y = _rmsnorm(y, gf_ref[...])
        yp_ref, ysm_ref = out_refs
        is_sample = pl.program_id(0) >= N_PROMPT_TILES

        @pl.when(jnp.logical_not(is_sample))
        def _():
            yp_ref[...] = y

        @pl.when(is_sample)
        def _():
            ysm_ref[...] = y
    else:
        out_refs[0][...] = y


def _combine(x, dest, wts, ys, g_final, final):
    if final:
        out_specs = [pl.BlockSpec((TM, D), lambda i: (jnp.minimum(i, N_PROMPT_TILES - 1), 0)),
                     pl.BlockSpec((TM, D), lambda i: (0, 0))]
        out_shape = [jax.ShapeDtypeStruct((N_PROMPT_TOK, D), F32), jax.ShapeDtypeStruct((N_SAMPLE_TOK, D), F32)]
    else:
        out_specs = [pl.BlockSpec((TM, D), lambda i: (i, 0))]
        out_shape = [jax.ShapeDtypeStruct((N_TOK, D), F32)]
    outs = pl.pallas_call(
        functools.partial(_combine_kernel, final=final),
        grid=(N_TILES,),
        in_specs=[
            pl.BlockSpec((1, 1, 2 * TM), lambda i: (i, 0, 0), memory_space=pltpu.SMEM),
            pl.BlockSpec((TM, D), lambda i: (i, 0)),
            pl.BlockSpec((TM, LANES), lambda i: (i, 0)),
            _const_spec((1, D)),
            pl.BlockSpec(memory_space=pl.ANY),
        ],
        out_specs=out_specs,
        out_shape=out_shape,
        scratch_shapes=[pltpu.VMEM((2, TM, D), F32), pltpu.SemaphoreType.DMA(())],
        compiler_params=pltpu.CompilerParams(dimension_semantics=("arbitrary",), vmem_limit_bytes=VMEM_LIMIT),
        name="moe_combine",
    )(dest.reshape(N_TILES, 1, 2 * TM), x, wts, g_final.reshape(1, D), ys)
    return outs if final else outs[0]


def _moe_ffn(x, g, w_router, w_gate, w_up, w_down, g_final, final):
    meta, wts, counts = _router(x, g, w_router)
    counts = counts[0, :N_EXPERTS]
    blocks = (counts + BM - 1) // BM
    block_end = jnp.cumsum(blocks)
    seg_start = (block_end - blocks) * BM
    dest = seg_start[meta[:, 0:2]] + meta[:, 2:4]
    n_used = block_end[-1:]
    bidx = jnp.minimum(jnp.arange(N_BLOCKS, dtype=jnp.int32), n_used - 1)
    block_expert = jnp.sum(bidx[:, None] >= block_end[None, :], axis=1).astype(jnp.int32)
    n_used = n_used.astype(jnp.int32)
    xs = _dispatch(x, dest, (block_end - 1).astype(jnp.int32), blocks.astype(jnp.int32), n_used)
    ys = _gmm(xs, g, block_expert, n_used, w_gate, w_up, w_down)
    return _combine(x, dest, wts, ys, g_final, final)


def kernel(x_prompt, x_sample, state_conv_b, state_conv_c, norm_mix, norm_ffn, norm_final, a_w_in, a_ln_g, a_ln_b, a_w_s, a_b_s, a_w_out, b_w_in, b_conv, b_w_out, c_w_pw1, c_b_pw1, c_dw, c_b_dw, c_ln_g, c_ln_b, c_w_pw2, c_b_pw2, f_w_gate, f_w_up, f_w_down, m_router, m_w_gate, m_w_up, m_w_down):
    last_tile = jnp.arange(N_SEQ) * TILES_PER_SEQ + TILES_PER_SEQ - 1

    x, v0 = _gmlp_mixer(x_prompt.reshape(N_PROMPT_TOK, D), x_sample.reshape(N_SAMPLE_TOK, D), 0,
                        norm_mix[0], a_w_in[0], a_ln_g[0], a_ln_b[0], a_w_s[0], a_b_s[0], a_w_out[0])
    x = _dense_ffn(x, norm_ffn[0], f_w_gate[0], f_w_up[0], f_w_down[0])
    x, tb_p, tb_s = _sconv_mixer(x, norm_mix[1], b_w_in[0], b_conv[0], b_w_out[0], state_conv_b[0])
    x = _moe_ffn(x, norm_ffn[1], m_router[0], m_w_gate[0], m_w_up[0], m_w_down[0], norm_final, False)
    x, tc_p, tc_s = _conf_mixer(x, norm_mix[2], c_w_pw1[0], c_b_pw1[0], c_dw[0], c_b_dw[0], c_ln_g[0], c_ln_b[0],
                                c_w_pw2[0], c_b_pw2[0], state_conv_c[0])
    x = _dense_ffn(x, norm_ffn[2], f_w_gate[1], f_w_up[1], f_w_down[1])
    x, v1 = _gmlp_mixer(x, x, N_PROMPT_TILES,
                        norm_mix[3], a_w_in[1], a_ln_g[1], a_ln_b[1], a_w_s[1], a_b_s[1], a_w_out[1])
    y_prompt, y_sample = _moe_ffn(x, norm_ffn[3], m_router[1], m_w_gate[1], m_w_up[1], m_w_down[1], norm_final, True)

    y_prompt = y_prompt.reshape(N_SEQ, SEQ, D)
    y_sample = y_sample.reshape(N_SEQ, SAMPLE_SEQ, D)
    nb = B_WIDTH - 1
    nc = C_WIDTH - 1
    new_b_p = tb_p[last_tile, B_HALO - nb:, :][None]
    new_b_s = tb_s[:, B_HALO - nb:, :][None]
    new_c_p = tc_p[last_tile, C_HALO - nc:, :][None]
    new_c_s = tc_s[:, C_HALO - nc:, :][None]
    new_v = jnp.stack([v0, v1]).reshape(2, N_SEQ, SAMPLE_SEQ, A_HALF)
    return (y_prompt, y_sample, new_b_p, new_b_s, new_c_p, new_c_s, new_v)
```

```python
import functools

import jax
import jax.numpy as jnp
from jax import lax
from jax.experimental import pallas as pl
from jax.experimental.pallas import tpu as pltpu

F32 = jnp.float32
BF16 = jnp.bfloat16

D = 1024
TM = 512
SEQ = 4096
SAMPLE_SEQ = 64
N_SEQ = 8
TILES_PER_SEQ = SEQ // TM
N_PROMPT_TOK = N_SEQ * SEQ
N_SAMPLE_TOK = N_SEQ * SAMPLE_SEQ
N_TOK = N_PROMPT_TOK + N_SAMPLE_TOK
N_PROMPT_TILES = N_PROMPT_TOK // TM
N_TILES = N_TOK // TM
assert N_SAMPLE_TOK == TM and N_TILES == N_PROMPT_TILES + 1

A_HALF = 2 * D
A_GROUPS = 8
A_HEAD = A_HALF // A_GROUPS
A_CHUNK = 128
B_WIDTH = 3
C_WIDTH = 31
B_HALO = 8
C_HALO = 32
D_FF = 2816
N_EXPERTS = 8
LANES = 128
SUBLANES = 8
MXU_COLS = 256
RMS_EPS = 1e-6
LN_EPS = 1e-5

BM = TM
N_ROWS = 2 * N_TOK + N_EXPERTS * BM
N_BLOCKS = N_ROWS // BM
FF_CHUNKS = ((0, 1024), (1024, 1024), (2048, 768))

VMEM_LIMIT = 56 * 1024 * 1024


def _const_spec(shape):
    return pl.BlockSpec(shape, lambda *_: (0,) * len(shape), pipeline_mode=pl.Buffered(1))


def _rmsnorm(x, g):
    return x * lax.rsqrt(jnp.mean(x * x, axis=-1, keepdims=True) + RMS_EPS) * g


def _layernorm(x, g, b):
    mu = jnp.mean(x, axis=-1, keepdims=True)
    xc = x - mu
    var = jnp.mean(xc * xc, axis=-1, keepdims=True)
    return xc * lax.rsqrt(var + LN_EPS) * g + b


def _dot(a, b):
    return jnp.dot(a, b, preferred_element_type=F32)


def _gmlp_kernel(xp_ref, xs_ref, g_ref, win_ref, lng_ref, lnb_ref, ws_ref, bs_ref, wout_ref,
                 o_ref, v_ref, vn_s, y_s):
    is_sample = pl.program_id(0) >= N_PROMPT_TILES
    x = jnp.where(is_sample, xs_ref[...], xp_ref[...])
    h = _rmsnorm(x, g_ref[...]).astype(BF16)
    v = jax.nn.gelu(_dot(h, win_ref[:, A_HALF:]))
    vn = _layernorm(v, lng_ref[...], lnb_ref[...])

    @pl.when(is_sample)
    def _():
        v_ref[...] = vn

    vn_s[...] = vn.astype(BF16)
    r = lax.broadcasted_iota(jnp.int32, (A_CHUNK, A_CHUNK), 0)
    c = lax.broadcasted_iota(jnp.int32, (A_CHUNK, A_CHUNK), 1)
    seg_shift = jnp.where(is_sample, 6, 7)
    mask = (r >= c) & ((r >> seg_shift) == (c >> seg_shift))
    for g in range(A_GROUPS):
        cols = slice(g * A_HEAD, (g + 1) * A_HEAD)
        u_g = jax.nn.gelu(_dot(h, win_ref[:, cols]))
        ws = jnp.where(mask, ws_ref[0, g], 0.0).astype(BF16)
        bias = jnp.concatenate([bs_ref[0, g]] * (A_HEAD // LANES), axis=1)
        for ch in range(TM // A_CHUNK):
            rows = slice(ch * A_CHUNK, (ch + 1) * A_CHUNK)
            s = _dot(ws, vn_s[rows, cols]) + bias
            y_s[rows, cols] = (u_g[rows] * s).astype(BF16)
    o_ref[...] = x + _dot(y_s[...], wout_ref[...])


def _gmlp_mixer(x_prompt, x_sample, sample_block, g, w_in, ln_g, ln_b, w_s, b_s, w_out):
    ws2 = jnp.stack([w_s, jnp.tile(w_s[:, :SAMPLE_SEQ, :SAMPLE_SEQ], (1, 2, 2))])
    b2 = jnp.stack([b_s, jnp.tile(b_s[:, :SAMPLE_SEQ], (1, 2))])
    b2 = jnp.broadcast_to(b2[..., None], (2, A_GROUPS, A_CHUNK, LANES))
    return pl.pallas_call(
        _gmlp_kernel,
        grid=(N_TILES,),
        in_specs=[
            pl.BlockSpec((TM, D), lambda i: (jnp.minimum(i, N_PROMPT_TILES - 1), 0)),
            pl.BlockSpec((TM, D), lambda i: (sample_block, 0)),
            _const_spec((1, D)),
            _const_spec((D, 2 * A_HALF)),
            _const_spec((1, A_HALF)),
            _const_spec((1, A_HALF)),
            pl.BlockSpec((1, A_GROUPS, A_CHUNK, A_CHUNK), lambda i: (i // N_PROMPT_TILES, 0, 0, 0)),
            pl.BlockSpec((1, A_GROUPS, A_CHUNK, LANES), lambda i: (i // N_PROMPT_TILES, 0, 0, 0)),
            _const_spec((A_HALF, D)),
        ],
        out_specs=[
            pl.BlockSpec((TM, D), lambda i: (i, 0)),
            pl.BlockSpec((TM, A_HALF), lambda i: (0, 0)),
        ],
        out_shape=[
            jax.ShapeDtypeStruct((N_TOK, D), F32),
            jax.ShapeDtypeStruct((N_SAMPLE_TOK, A_HALF), F32),
        ],
        scratch_shapes=[pltpu.VMEM((TM, A_HALF), BF16), pltpu.VMEM((TM, A_HALF), BF16)],
        compiler_params=pltpu.CompilerParams(dimension_semantics=("arbitrary",), vmem_limit_bytes=VMEM_LIMIT),
        name="gmlp_mixer",
    )(x_prompt, x_sample, g.reshape(1, D), w_in.astype(BF16), ln_g.reshape(1, A_HALF), ln_b.reshape(1, A_HALF),
      ws2, b2, w_out.astype(BF16))


def _sconv_kernel(x_ref, g_ref, win_ref, cw_ref, wout_ref, st_ref,
                  o_ref, tailp_ref, tails_ref, full_s, y_s):
    i = pl.program_id(0)
    is_sample = i >= N_PROMPT_TILES
    x = x_ref[...]
    h = _rmsnorm(x, g_ref[...]).astype(BF16)
    p = _dot(h, win_ref[...])
    bg = p[:, :D]
    u = p[:, D:2 * D] * p[:, 2 * D:]
    w = [cw_ref[k:k + 1, :] for k in range(B_WIDTH)]

    def conv(n):
        return sum(w[k] * full_s[pl.ds(B_HALO - (B_WIDTH - 1) + k, n), :] for k in range(B_WIDTH))

    @pl.when(jnp.logical_not(is_sample))
    def _():
        @pl.when(i % TILES_PER_SEQ == 0)
        def _():
            full_s[0:B_HALO, :] = jnp.zeros((B_HALO, D), F32)

        full_s[B_HALO:, :] = u
        y_s[...] = (bg * conv(TM)).astype(BF16)
        tail = full_s[TM:TM + B_HALO, :]
        tailp_ref[0] = tail
        full_s[0:B_HALO, :] = tail

    @pl.when(is_sample)
    def _():
        for b in range(N_SEQ):
            rows = slice(b * SAMPLE_SEQ, (b + 1) * SAMPLE_SEQ)
            full_s[0:B_HALO, :] = st_ref[b]
            full_s[B_HALO:B_HALO + SAMPLE_SEQ, :] = u[rows]
            y_s[rows, :] = (bg[rows] * conv(SAMPLE_SEQ)).astype(BF16)
            tails_ref[b] = full_s[SAMPLE_SEQ:SAMPLE_SEQ + B_HALO, :]
        tailp_ref[0] = jnp.zeros((B_HALO, D), F32)

    o_ref[...] = x + _dot(y_s[...], wout_ref[...])


def _sconv_mixer(x, g, w_in, w_conv, w_out, state):
    st = jnp.pad(state, ((0, 0), (B_HALO - (B_WIDTH - 1), 0), (0, 0)))
    return pl.pallas_call(
        _sconv_kernel,
        grid=(N_TILES,),
        in_specs=[
            pl.BlockSpec((TM, D), lambda i: (i, 0)),
            _const_spec((1, D)),
            _const_spec((D, 3 * D)),
            _const_spec((B_WIDTH, D)),
            _const_spec((D, D)),
            _const_spec((N_SEQ, B_HALO, D)),
        ],
        out_specs=[
            pl.BlockSpec((TM, D), lambda i: (i, 0)),
            pl.BlockSpec((1, B_HALO, D), lambda i: (i, 0, 0)),
            pl.BlockSpec((N_SEQ, B_HALO, D), lambda i: (0, 0, 0)),
        ],
        out_shape=[
            jax.ShapeDtypeStruct((N_TOK, D), F32),
            jax.ShapeDtypeStruct((N_TILES, B_HALO, D), F32),
            jax.ShapeDtypeStruct((N_SEQ, B_HALO, D), F32),
        ],
        scratch_shapes=[pltpu.VMEM((B_HALO + TM, D), F32), pltpu.VMEM((TM, D), BF16)],
        compiler_params=pltpu.CompilerParams(dimension_semantics=("arbitrary",), vmem_limit_bytes=VMEM_LIMIT),
        name="sconv_mixer",
    )(x, g.reshape(1, D), w_in.astype(BF16), w_conv, w_out.astype(BF16), st)


C_ROW_BLOCK = 16


def _conf_kernel(x_ref, g_ref, w1_ref, b1_ref, dw_ref, bdw_ref, lng_ref, lnb_ref, w2_ref, b2_ref, st_ref,
                 o_ref, tailp_ref, tails_ref, full_s, shift_s, conv_s, y_s):
    i = pl.program_id(0)
    is_sample = i >= N_PROMPT_TILES
    x = x_ref[...]
    h = _rmsnorm(x, g_ref[...]).astype(BF16)
    p = _dot(h, w1_ref[...]) + b1_ref[...]
    gl = p[:, :D] * jax.nn.sigmoid(p[:, D:])

    def conv(n, out_row0):
        n_in = C_HALO + n
        for r in range(1, SUBLANES):
            shift_s[r - 1, 0:n_in - SUBLANES, :] = full_s[r:r + n_in - SUBLANES, :]

        def block(rb, carry):
            base = pl.multiple_of(rb * C_ROW_BLOCK, C_ROW_BLOCK)
            acc = [jnp.zeros((SUBLANES, D), F32) for _ in range(C_ROW_BLOCK // SUBLANES)]
            for k in range(C_WIDTH):
                off = k + (C_HALO - (C_WIDTH - 1))
                row0 = base + (off // SUBLANES) * SUBLANES
                if off % SUBLANES == 0:
                    tap = full_s[pl.ds(row0, C_ROW_BLOCK), :]
                else:
                    tap = shift_s[off % SUBLANES - 1, pl.ds(row0, C_ROW_BLOCK), :]
                wk = dw_ref[k]
                for a in range(len(acc)):
                    acc[a] = acc[a] + wk * tap[a * SUBLANES:(a + 1) * SUBLANES]
            conv_s[pl.ds(out_row0 + base, C_ROW_BLOCK), :] = jnp.concatenate(acc, axis=0)
            return carry

        lax.fori_loop(0, n // C_ROW_BLOCK, block, 0)

    @pl.when(jnp.logical_not(is_sample))
    def _():
        @pl.when(i % TILES_PER_SEQ == 0)
        def _():
            full_s[0:C_HALO, :] = jnp.zeros((C_HALO, D), F32)

        full_s[C_HALO:, :] = gl
        conv(TM, 0)
        tail = full_s[TM:TM + C_HALO, :]
        tailp_ref[0] = tail
        full_s[0:C_HALO, :] = tail

    @pl.when(is_sample)
    def _():
        for b in range(N_SEQ):
            full_s[0:C_HALO, :] = st_ref[b]
            full_s[C_HALO:C_HALO + SAMPLE_SEQ, :] = gl[b * SAMPLE_SEQ:(b + 1) * SAMPLE_SEQ]
            conv(SAMPLE_SEQ, b * SAMPLE_SEQ)
            tails_ref[b] = full_s[SAMPLE_SEQ:SAMPLE_SEQ + C_HALO, :]
        tailp_ref[0] = jnp.zeros((C_HALO, D), F32)

    y = _layernorm(conv_s[...] + bdw_ref[...], lng_ref[...], lnb_ref[...])
    y_s[...] = (y * jax.nn.sigmoid(y)).astype(BF16)
    o_ref[...] = x + _dot(y_s[...], w2_ref[...]) + b2_ref[...]


def _conf_mixer(x, g, w_pw1, b_pw1, w_dw, b_dw, ln_g, ln_b, w_pw2, b_pw2, state):
    st = jnp.pad(state, ((0, 0), (C_HALO - (C_WIDTH - 1), 0), (0, 0)))
    dw = jnp.broadcast_to(w_dw[:, None, :], (C_WIDTH, SUBLANES, D))
    return pl.pallas_call(
        _conf_kernel,
        grid=(N_TILES,),
        in_specs=[
            pl.BlockSpec((TM, D), lambda i: (i, 0)),
            _const_spec((1, D)),
            _const_spec((D, 2 * D)),
            _const_spec((1, 2 * D)),
            _const_spec((C_WIDTH, SUBLANES, D)),
            _const_spec((1, D)),
            _const_spec((1, D)),
            _const_spec((1, D)),
            _const_spec((D, D)),
            _const_spec((1, D)),
            _const_spec((N_SEQ, C_HALO, D)),
        ],
        out_specs=[
            pl.BlockSpec((TM, D), lambda i: (i, 0)),
            pl.BlockSpec((1, C_HALO, D), lambda i: (i, 0, 0)),
            pl.BlockSpec((N_SEQ, C_HALO, D), lambda i: (0, 0, 0)),
        ],
        out_shape=[
            jax.ShapeDtypeStruct((N_TOK, D), F32),
            jax.ShapeDtypeStruct((N_TILES, C_HALO, D), F32),
            jax.ShapeDtypeStruct((N_SEQ, C_HALO, D), F32),
        ],
        scratch_shapes=[
            pltpu.VMEM((C_HALO + TM, D), F32),
            pltpu.VMEM((SUBLANES - 1, C_HALO + TM, D), F32),
            pltpu.VMEM((TM, D), F32),
            pltpu.VMEM((TM, D), BF16),
        ],
        compiler_params=pltpu.CompilerParams(dimension_semantics=("arbitrary",), vmem_limit_bytes=VMEM_LIMIT),
        name="conf_mixer",
    )(x, g.reshape(1, D), w_pw1.astype(BF16), b_pw1.reshape(1, 2 * D), dw, b_dw.reshape(1, D),
      ln_g.reshape(1, D), ln_b.reshape(1, D), w_pw2.astype(BF16), b_pw2.reshape(1, D), st)


def _swiglu_hidden(h, wg_ref, wu_ref, hid_s, lead=()):
    for c0, cn in FF_CHUNKS:
        a = _dot(h, wg_ref[lead + (slice(None), slice(c0, c0 + cn))])
        b = _dot(h, wu_ref[lead + (slice(None), slice(c0, c0 + cn))])
        hid_s[:, c0:c0 + cn] = (a * jax.nn.sigmoid(a) * b).astype(BF16)


def _ffn_kernel(x_ref, g_ref, wg_ref, wu_ref, wd_ref, o_ref, hid_s):
    x = x_ref[...]
    h = _rmsnorm(x, g_ref[...]).astype(BF16)
    _swiglu_hidden(h, wg_ref, wu_ref, hid_s)
    o_ref[...] = x + _dot(hid_s[...], wd_ref[...])


def _dense_ffn(x, g, w_gate, w_up, w_down):
    return pl.pallas_call(
        _ffn_kernel,
        grid=(N_TILES,),
        in_specs=[
            pl.BlockSpec((TM, D), lambda i: (i, 0)),
            _const_spec((1, D)),
            _const_spec((D, D_FF)),
            _const_spec((D, D_FF)),
            _const_spec((D_FF, D)),
        ],
        out_specs=pl.BlockSpec((TM, D), lambda i: (i, 0)),
        out_shape=jax.ShapeDtypeStruct((N_TOK, D), F32),
        scratch_shapes=[pltpu.VMEM((TM, D_FF), BF16)],
        compiler_params=pltpu.CompilerParams(dimension_semantics=("arbitrary",), vmem_limit_bytes=VMEM_LIMIT),
        name="dense_ffn",
    )(x, g.reshape(1, D), w_gate.astype(BF16), w_up.astype(BF16), w_down.astype(BF16))


def _router_kernel(x_ref, g_ref, wr_ref, meta_ref, wts_ref, cnt_ref, run_s):
    i = pl.program_id(0)

    @pl.when(i == 0)
    def _():
        run_s[...] = jnp.zeros((1, LANES), F32)

    h = _rmsnorm(x_ref[...], g_ref[...])
    logits = jnp.dot(h, wr_ref[...], preferred_element_type=F32, precision=lax.Precision.HIGHEST)
    lane = lax.broadcasted_iota(jnp.int32, (TM, LANES), 1)
    lane_f = lane.astype(F32)
    neg = jnp.float32(-jnp.inf)
    logits = jnp.where(lane < N_EXPERTS, logits, neg)
    l1 = jnp.max(logits, axis=-1, keepdims=True)
    e1 = jnp.min(jnp.where(logits == l1, lane_f, float(LANES)), axis=-1, keepdims=True).astype(jnp.int32)
    rest = jnp.where(lane == e1, neg, logits)
    l2 = jnp.max(rest, axis=-1, keepdims=True)
    e2 = jnp.min(jnp.where(rest == l2, lane_f, float(LANES)), axis=-1, keepdims=True).astype(jnp.int32)
    t = jnp.exp(l2 - l1)
    w1 = 1.0 / (1.0 + t)
    w2 = t * w1
    sel = jnp.logical_or(lane == e1, lane == e2)
    onehot = jnp.where(sel, 1.0, 0.0)
    r = lax.broadcasted_iota(jnp.int32, (TM, TM), 0)
    c = lax.broadcasted_iota(jnp.int32, (TM, TM), 1)
    below = jnp.where(r > c, 1.0, 0.0).astype(BF16)
    before = _dot(below, onehot.astype(BF16)) + run_s[...]
    r1 = jnp.sum(jnp.where(lane == e1, before, 0.0), axis=-1, keepdims=True).astype(jnp.int32)
    r2 = jnp.sum(jnp.where(lane == e2, before, 0.0), axis=-1, keepdims=True).astype(jnp.int32)
    run_s[...] = run_s[...] + jnp.sum(onehot, axis=0, keepdims=True)
    meta = jnp.where(lane == 0, e1, jnp.where(lane == 1, e2, jnp.where(lane == 2, r1, jnp.where(lane == 3, r2, 0))))
    meta_ref[...] = meta
    wts_ref[...] = jnp.where(lane == 0, w1, jnp.where(lane == 1, w2, 0.0))
    cnt_ref[...] = run_s[...].astype(jnp.int32)


def _router(x, g, w_router):
    wr = jnp.pad(w_router, ((0, 0), (0, LANES - N_EXPERTS)))
    return pl.pallas_call(
        _router_kernel,
        grid=(N_TILES,),
        in_specs=[
            pl.BlockSpec((TM, D), lambda i: (i, 0)),
            _const_spec((1, D)),
            _const_spec((D, LANES)),
        ],
        out_specs=[
            pl.BlockSpec((TM, LANES), lambda i: (i, 0)),
            pl.BlockSpec((TM, LANES), lambda i: (i, 0)),
            pl.BlockSpec((1, LANES), lambda i: (0, 0)),
        ],
        out_shape=[
            jax.ShapeDtypeStruct((N_TOK, LANES), jnp.int32),
            jax.ShapeDtypeStruct((N_TOK, LANES), F32),
            jax.ShapeDtypeStruct((1, LANES), jnp.int32),
        ],
        scratch_shapes=[pltpu.VMEM((1, LANES), F32)],
        compiler_params=pltpu.CompilerParams(dimension_semantics=("arbitrary",), vmem_limit_bytes=VMEM_LIMIT),
        name="moe_router",
    )(x, g.reshape(1, D), wr)


ROW_TILE = D // LANES
assert ROW_TILE == SUBLANES


def _from_token_major(ref, n):
    return jnp.concatenate([ref[pl.ds(s, n, stride=ROW_TILE), :] for s in range(ROW_TILE)], axis=-1)


def _to_token_major(ref, val, n):
    for s in range(ROW_TILE):
        ref[pl.ds(s, n, stride=ROW_TILE), :] = val[:, s * LANES:(s + 1) * LANES]


def _row_copy(src, src_row, dst, dst_row, sem):
    s0 = pl.multiple_of(src_row * ROW_TILE, ROW_TILE)
    d0 = pl.multiple_of(dst_row * ROW_TILE, ROW_TILE)
    return pltpu.make_async_copy(src.at[pl.ds(s0, ROW_TILE), :], dst.at[pl.ds(d0, ROW_TILE), :], sem)


DMA_UNROLL = 8


def _dispatch_kernel(last_ref, nblk_ref, nu_ref, dest_ref, x_ref, xs_hbm, tok_s, sem, zsem):
    def zero_block(b):
        return pltpu.make_async_copy(tok_s, xs_hbm.at[pl.ds(b * (BM * ROW_TILE), BM * ROW_TILE), :], zsem)

    @pl.when(pl.program_id(0) == 0)
    def _():
        tok_s[...] = jnp.zeros((BM * ROW_TILE, LANES), F32)
        for start in (True, False):
            for e in range(N_EXPERTS):
                tail = N_BLOCKS - 1 - e
                for cond, b in ((nblk_ref[e] > 0, last_ref[e]), (tail >= nu_ref[0], tail)):
                    cp = zero_block(b)
                    pl.when(cond)(cp.start if start else cp.wait)

    _to_token_major(tok_s, x_ref[...], TM)

    def issue(j, carry):
        for u in range(DMA_UNROLL):
            t = j * DMA_UNROLL + u
            _row_copy(tok_s, t, xs_hbm, dest_ref[0, 0, 2 * t], sem).start()
            _row_copy(tok_s, t, xs_hbm, dest_ref[0, 0, 2 * t + 1], sem).start()
        return carry

    lax.fori_loop(0, TM // DMA_UNROLL, issue, 0)

    def drain(j, carry):
        for _ in range(2 * DMA_UNROLL):
            _row_copy(tok_s, 0, xs_hbm, 0, sem).wait()
        return carry

    lax.fori_loop(0, TM // DMA_UNROLL, drain, 0)


def _dispatch(x, dest, last_block, n_blocks, n_used):
    return pl.pallas_call(
        _dispatch_kernel,
        grid_spec=pltpu.PrefetchScalarGridSpec(
            num_scalar_prefetch=3,
            grid=(N_TILES,),
            in_specs=[
                pl.BlockSpec((1, 1, 2 * TM), lambda i, *_: (i, 0, 0), memory_space=pltpu.SMEM),
                pl.BlockSpec((TM, D), lambda i, *_: (i, 0)),
            ],
            out_specs=pl.BlockSpec(memory_space=pl.ANY),
            scratch_shapes=[pltpu.VMEM((TM * ROW_TILE, LANES), F32), pltpu.SemaphoreType.DMA(()),
                            pltpu.SemaphoreType.DMA(())],
        ),
        out_shape=jax.ShapeDtypeStruct((N_ROWS * ROW_TILE, LANES), F32),
        compiler_params=pltpu.CompilerParams(dimension_semantics=("arbitrary",), vmem_limit_bytes=VMEM_LIMIT),
        name="moe_dispatch",
    )(last_block, n_blocks, n_used, dest.reshape(N_TILES, 1, 2 * TM), x)


def _gmm_kernel(be_ref, nu_ref, xs_ref, g_ref, wg_ref, wu_ref, wd_ref, ys_ref, hid_s):
    used = pl.program_id(0) < nu_ref[0]

    @pl.when(used)
    def _():
        h = _rmsnorm(_from_token_major(xs_ref, BM), g_ref[...]).astype(BF16)
        _swiglu_hidden(h, wg_ref, wu_ref, hid_s, lead=(0,))
        _to_token_major(ys_ref, _dot(hid_s[...], wd_ref[0]), BM)

    @pl.when(jnp.logical_not(used))
    def _():
        ys_ref[...] = jnp.zeros((BM * ROW_TILE, LANES), F32)


def _gmm(xs, g, block_expert, n_used, w_gate, w_up, w_down):
    row_map = lambda i, be, nu: (jnp.minimum(i, nu[0] - 1), 0)
    wspec = lambda shape: pl.BlockSpec((1,) + shape, lambda i, be, nu: (be[i], 0, 0))
    return pl.pallas_call(
        _gmm_kernel,
        grid_spec=pltpu.PrefetchScalarGridSpec(
            num_scalar_prefetch=2,
            grid=(N_BLOCKS,),
            in_specs=[
                pl.BlockSpec((BM * ROW_TILE, LANES), row_map),
                pl.BlockSpec((1, D), lambda i, be, nu: (0, 0)),
                wspec((D, D_FF)),
                wspec((D, D_FF)),
                wspec((D_FF, D)),
            ],
            out_specs=pl.BlockSpec((BM * ROW_TILE, LANES), lambda i, be, nu: (i, 0)),
            scratch_shapes=[pltpu.VMEM((BM, D_FF), BF16)],
        ),
        out_shape=jax.ShapeDtypeStruct((N_ROWS * ROW_TILE, LANES), F32),
        compiler_params=pltpu.CompilerParams(dimension_semantics=("arbitrary",), vmem_limit_bytes=VMEM_LIMIT),
        name="moe_gmm",
    )(block_expert, n_used, xs, g.reshape(1, D), w_gate.astype(BF16), w_up.astype(BF16), w_down.astype(BF16))


def _combine_kernel(dest_ref, x_ref, wts_ref, gf_ref, ys_hbm, *rest, final):
    out_refs, (buf0, buf1, sem) = rest[:-3], rest[-3:]

    def issue(j, carry):
        for u in range(DMA_UNROLL):
            t = j * DMA_UNROLL + u
            _row_copy(ys_hbm, dest_ref[0, 0, 2 * t], buf0, t, sem).start()
            _row_copy(ys_hbm, dest_ref[0, 0, 2 * t + 1], buf1, t, sem).start()
        return carry

    lax.fori_loop(0, TM // DMA_UNROLL, issue, 0)

    def drain(j, carry):
        for _ in range(DMA_UNROLL):
            _row_copy(ys_hbm, 0, buf0, 0, sem).wait()
            _row_copy(ys_hbm, 0, buf1, 0, sem).wait()
        return carry

    lax.fori_loop(0, TM // DMA_UNROLL, drain, 0)
    w = wts_ref[...]
    y = x_ref[...] + w[:, 0:1] * _from_token_major(buf0, TM) + w[:, 1:2] * _from_token_major(buf1, TM)
    if final:
        y = _rmsnorm(y, gf_ref[...])
        yp_ref, ysm_ref = out_refs
        is_sample = pl.program_id(0) >= N_PROMPT_TILES

        @pl.when(jnp.logical_not(is_sample))
        def _():
            yp_ref[...] = y

        @pl.when(is_sample)
        def _():
            ysm_ref[...] = y
    else:
        out_refs[0][...] = y


def _combine(x, dest, wts, ys, g_final, final):
    if final:
        out_specs = [pl.BlockSpec((TM, D), lambda i: (jnp.minimum(i, N_PROMPT_TILES - 1), 0)),
                     pl.BlockSpec((TM, D), lambda i: (0, 0))]
        out_shape = [jax.ShapeDtypeStruct((N_PROMPT_TOK, D), F32), jax.ShapeDtypeStruct((N_SAMPLE_TOK, D), F32)]
    else:
        out_specs = [pl.BlockSpec((TM, D), lambda i: (i, 0))]
        out_shape = [jax.ShapeDtypeStruct((N_TOK, D), F32)]
    outs = pl.pallas_call(
        functools.partial(_combine_kernel, final=final),
        grid=(N_TILES,),
        in_specs=[
            pl.BlockSpec((1, 1, 2 * TM), lambda i: (i, 0, 0), memory_space=pltpu.SMEM),
            pl.BlockSpec((TM, D), lambda i: (i, 0)),
            pl.BlockSpec((TM, LANES), lambda i: (i, 0)),
            _const_spec((1, D)),
            pl.BlockSpec(memory_space=pl.ANY),
        ],
        out_specs=out_specs,
        out_shape=out_shape,
        scratch_shapes=[pltpu.VMEM((TM * ROW_TILE, LANES), F32), pltpu.VMEM((TM * ROW_TILE, LANES), F32),
                        pltpu.SemaphoreType.DMA(())],
        compiler_params=pltpu.CompilerParams(dimension_semantics=("arbitrary",), vmem_limit_bytes=VMEM_LIMIT),
        name="moe_combine",
    )(dest.reshape(N_TILES, 1, 2 * TM), x, wts, g_final.reshape(1, D), ys)
    return outs if final else outs[0]


def _moe_ffn(x, g, w_router, w_gate, w_up, w_down, g_final, final):
    meta, wts, counts = _router(x, g, w_router)
    counts = counts[0, :N_EXPERTS]
    blocks = (counts + BM - 1) // BM
    block_end = jnp.cumsum(blocks)
    seg_start = (block_end - blocks) * BM
    dest = seg_start[meta[:, 0:2]] + meta[:, 2:4]
    n_used = block_end[-1:]
    bidx = jnp.minimum(jnp.arange(N_BLOCKS, dtype=jnp.int32), n_used - 1)
    block_expert = jnp.sum(bidx[:, None] >= block_end[None, :], axis=1).astype(jnp.int32)
    n_used = n_used.astype(jnp.int32)
    xs = _dispatch(x, dest, (block_end - 1).astype(jnp.int32), blocks.astype(jnp.int32), n_used)
    ys = _gmm(xs, g, block_expert, n_used, w_gate, w_up, w_down)
    return _combine(x, dest, wts, ys, g_final, final)


def kernel(x_prompt, x_sample, state_conv_b, state_conv_c, norm_mix, norm_ffn, norm_final, a_w_in, a_ln_g, a_ln_b, a_w_s, a_b_s, a_w_out, b_w_in, b_conv, b_w_out, c_w_pw1, c_b_pw1, c_dw, c_b_dw, c_ln_g, c_ln_b, c_w_pw2, c_b_pw2, f_w_gate, f_w_up, f_w_down, m_router, m_w_gate, m_w_up, m_w_down):
    last_tile = jnp.arange(N_SEQ) * TILES_PER_SEQ + TILES_PER_SEQ - 1

    x, v0 = _gmlp_mixer(x_prompt.reshape(N_PROMPT_TOK, D), x_sample.reshape(N_SAMPLE_TOK, D), 0,
                        norm_mix[0], a_w_in[0], a_ln_g[0], a_ln_b[0], a_w_s[0], a_b_s[0], a_w_out[0])
    x = _dense_ffn(x, norm_ffn[0], f_w_gate[0], f_w_up[0], f_w_down[0])
    x, tb_p, tb_s = _sconv_mixer(x, norm_mix[1], b_w_in[0], b_conv[0], b_w_out[0], state_conv_b[0])
    x = _moe_ffn(x, norm_ffn[1], m_router[0], m_w_gate[0], m_w_up[0], m_w_down[0], norm_final, False)
    x, tc_p, tc_s = _conf_mixer(x, norm_mix[2], c_w_pw1[0], c_b_pw1[0], c_dw[0], c_b_dw[0], c_ln_g[0], c_ln_b[0],
                                c_w_pw2[0], c_b_pw2[0], state_conv_c[0])
    x = _dense_ffn(x, norm_ffn[2], f_w_gate[1], f_w_up[1], f_w_down[1])
    x, v1 = _gmlp_mixer(x, x, N_PROMPT_TILES,
                        norm_mix[3], a_w_in[1], a_ln_g[1], a_ln_b[1], a_w_s[1], a_b_s[1], a_w_out[1])
    y_prompt, y_sample = _moe_ffn(x, norm_ffn[3], m_router[1], m_w_gate[1], m_w_up[1], m_w_down[1], norm_final, True)

    y_prompt = y_prompt.reshape(N_SEQ, SEQ, D)
    y_sample = y_sample.reshape(N_SEQ, SAMPLE_SEQ, D)
    nb = B_WIDTH - 1
    nc = C_WIDTH - 1
    new_b_p = tb_p[last_tile, B_HALO - nb:, :][None]
    new_b_s = tb_s[:, B_HALO - nb:, :][None]
    new_c_p = tc_p[last_tile, C_HALO - nc:, :][None]
    new_c_s = tc_s[:, C_HALO - nc:, :][None]
    new_v = jnp.stack([v0, v1]).reshape(2, N_SEQ, SAMPLE_SEQ, A_HALF)
    return (y_prompt, y_sample, new_b_p, new_b_s, new_c_p, new_c_s, new_v)
```

```python
import functools

import jax
import jax.numpy as jnp
from jax import lax
from jax.experimental import pallas as pl
from jax.experimental.pallas import tpu as pltpu

F32 = jnp.float32
BF16 = jnp.bfloat16

D = 1024
TM = 512
SEQ = 4096
SAMPLE_SEQ = 64
N_SEQ = 8
TILES_PER_SEQ = SEQ // TM
N_PROMPT_TOK = N_SEQ * SEQ
N_SAMPLE_TOK = N_SEQ * SAMPLE_SEQ
N_TOK = N_PROMPT_TOK + N_SAMPLE_TOK
N_PROMPT_TILES = N_PROMPT_TOK // TM
N_TILES = N_TOK // TM
assert N_SAMPLE_TOK == TM and N_TILES == N_PROMPT_TILES + 1

A_HALF = 2 * D
A_GROUPS = 8
A_HEAD = A_HALF // A_GROUPS
A_CHUNK = 128
B_WIDTH = 3
C_WIDTH = 31
B_HALO = 8
C_HALO = 32
D_FF = 2816
N_EXPERTS = 8
LANES = 128
SUBLANES = 8
MXU_COLS = 256
RMS_EPS = 1e-6
LN_EPS = 1e-5

BM = TM
N_ROWS = 2 * N_TOK + N_EXPERTS * BM
N_BLOCKS = N_ROWS // BM
FF_CHUNKS = ((0, 1024), (1024, 1024), (2048, 768))

VMEM_LIMIT = 56 * 1024 * 1024


def _const_spec(shape):
    return pl.BlockSpec(shape, lambda *_: (0,) * len(shape), pipeline_mode=pl.Buffered(1))


def _rmsnorm(x, g):
    return x * lax.rsqrt(jnp.mean(x * x, axis=-1, keepdims=True) + RMS_EPS) * g


def _layernorm(x, g, b):
    mu = jnp.mean(x, axis=-1, keepdims=True)
    xc = x - mu
    var = jnp.mean(xc * xc, axis=-1, keepdims=True)
    return xc * lax.rsqrt(var + LN_EPS) * g + b


def _dot(a, b):
    return jnp.dot(a, b, preferred_element_type=F32)


def _gmlp_kernel(xp_ref, xs_ref, g_ref, win_ref, lng_ref, lnb_ref, ws_ref, bs_ref, wout_ref,
                 o_ref, v_ref, vn_s, y_s):
    is_sample = pl.program_id(0) >= N_PROMPT_TILES
    x = jnp.where(is_sample, xs_ref[...], xp_ref[...])
    h = _rmsnorm(x, g_ref[...]).astype(BF16)
    v = jax.nn.gelu(_dot(h, win_ref[:, A_HALF:]))
    vn = _layernorm(v, lng_ref[...], lnb_ref[...])

    @pl.when(is_sample)
    def _():
        v_ref[...] = vn

    vn_s[...] = vn.astype(BF16)
    r = lax.broadcasted_iota(jnp.int32, (A_CHUNK, A_CHUNK), 0)
    c = lax.broadcasted_iota(jnp.int32, (A_CHUNK, A_CHUNK), 1)
    seg_shift = jnp.where(is_sample, 6, 7)
    mask = (r >= c) & ((r >> seg_shift) == (c >> seg_shift))
    for g in range(A_GROUPS):
        cols = slice(g * A_HEAD, (g + 1) * A_HEAD)
        u_g = jax.nn.gelu(_dot(h, win_ref[:, cols]))
        ws = jnp.where(mask, ws_ref[0, g], 0.0).astype(BF16)
        bias = jnp.concatenate([bs_ref[0, g]] * (A_HEAD // LANES), axis=1)
        for ch in range(TM // A_CHUNK):
            rows = slice(ch * A_CHUNK, (ch + 1) * A_CHUNK)
            s = _dot(ws, vn_s[rows, cols]) + bias
            y_s[rows, cols] = (u_g[rows] * s).astype(BF16)
    o_ref[...] = x + _dot(y_s[...], wout_ref[...])


def _gmlp_mixer(x_prompt, x_sample, sample_block, g, w_in, ln_g, ln_b, w_s, b_s, w_out):
    ws2 = jnp.stack([w_s, jnp.tile(w_s[:, :SAMPLE_SEQ, :SAMPLE_SEQ], (1, 2, 2))])
    b2 = jnp.stack([b_s, jnp.tile(b_s[:, :SAMPLE_SEQ], (1, 2))])
    b2 = jnp.broadcast_to(b2[..., None], (2, A_GROUPS, A_CHUNK, LANES))
    return pl.pallas_call(
        _gmlp_kernel,
        grid=(N_TILES,),
        in_specs=[
            pl.BlockSpec((TM, D), lambda i: (jnp.minimum(i, N_PROMPT_TILES - 1), 0)),
            pl.BlockSpec((TM, D), lambda i: (sample_block, 0)),
            _const_spec((1, D)),
            _const_spec((D, 2 * A_HALF)),
            _const_spec((1, A_HALF)),
            _const_spec((1, A_HALF)),
            pl.BlockSpec((1, A_GROUPS, A_CHUNK, A_CHUNK), lambda i: (i // N_PROMPT_TILES, 0, 0, 0)),
            pl.BlockSpec((1, A_GROUPS, A_CHUNK, LANES), lambda i: (i // N_PROMPT_TILES, 0, 0, 0)),
            _const_spec((A_HALF, D)),
        ],
        out_specs=[
            pl.BlockSpec((TM, D), lambda i: (i, 0)),
            pl.BlockSpec((TM, A_HALF), lambda i: (0, 0)),
        ],
        out_shape=[
            jax.ShapeDtypeStruct((N_TOK, D), F32),
            jax.ShapeDtypeStruct((N_SAMPLE_TOK, A_HALF), F32),
        ],
        scratch_shapes=[pltpu.VMEM((TM, A_HALF), BF16), pltpu.VMEM((TM, A_HALF), BF16)],
        compiler_params=pltpu.CompilerParams(dimension_semantics=("arbitrary",), vmem_limit_bytes=VMEM_LIMIT),
        name="gmlp_mixer",
    )(x_prompt, x_sample, g.reshape(1, D), w_in.astype(BF16), ln_g.reshape(1, A_HALF), ln_b.reshape(1, A_HALF),
      ws2, b2, w_out.astype(BF16))


def _sconv_kernel(x_ref, g_ref, win_ref, cw_ref, wout_ref, st_ref,
                  o_ref, tailp_ref, tails_ref, full_s, y_s):
    i = pl.program_id(0)
    is_sample = i >= N_PROMPT_TILES
    x = x_ref[...]
    h = _rmsnorm(x, g_ref[...]).astype(BF16)
    p = _dot(h, win_ref[...])
    bg = p[:, :D]
    u = p[:, D:2 * D] * p[:, 2 * D:]
    w = [cw_ref[k:k + 1, :] for k in range(B_WIDTH)]

    def conv(n):
        return sum(w[k] * full_s[pl.ds(B_HALO - (B_WIDTH - 1) + k, n), :] for k in range(B_WIDTH))

    @pl.when(jnp.logical_not(is_sample))
    def _():
        @pl.when(i % TILES_PER_SEQ == 0)
        def _():
            full_s[0:B_HALO, :] = jnp.zeros((B_HALO, D), F32)

        full_s[B_HALO:, :] = u
        y_s[...] = (bg * conv(TM)).astype(BF16)
        tail = full_s[TM:TM + B_HALO, :]
        tailp_ref[0] = tail
        full_s[0:B_HALO, :] = tail

    @pl.when(is_sample)
    def _():
        for b in range(N_SEQ):
            rows = slice(b * SAMPLE_SEQ, (b + 1) * SAMPLE_SEQ)
            full_s[0:B_HALO, :] = st_ref[b]
            full_s[B_HALO:B_HALO + SAMPLE_SEQ, :] = u[rows]
            y_s[rows, :] = (bg[rows] * conv(SAMPLE_SEQ)).astype(BF16)
            tails_ref[b] = full_s[SAMPLE_SEQ:SAMPLE_SEQ + B_HALO, :]
        tailp_ref[0] = jnp.zeros((B_HALO, D), F32)

    o_ref[...] = x + _dot(y_s[...], wout_ref[...])


def _sconv_mixer(x, g, w_in, w_conv, w_out, state):
    st = jnp.pad(state, ((0, 0), (B_HALO - (B_WIDTH - 1), 0), (0, 0)))
    return pl.pallas_call(
        _sconv_kernel,
        grid=(N_TILES,),
        in_specs=[
            pl.BlockSpec((TM, D), lambda i: (i, 0)),
            _const_spec((1, D)),
            _const_spec((D, 3 * D)),
            _const_spec((B_WIDTH, D)),
            _const_spec((D, D)),
            _const_spec((N_SEQ, B_HALO, D)),
        ],
        out_specs=[
            pl.BlockSpec((TM, D), lambda i: (i, 0)),
            pl.BlockSpec((1, B_HALO, D), lambda i: (i, 0, 0)),
            pl.BlockSpec((N_SEQ, B_HALO, D), lambda i: (0, 0, 0)),
        ],
        out_shape=[
            jax.ShapeDtypeStruct((N_TOK, D), F32),
            jax.ShapeDtypeStruct((N_TILES, B_HALO, D), F32),
            jax.ShapeDtypeStruct((N_SEQ, B_HALO, D), F32),
        ],
        scratch_shapes=[pltpu.VMEM((B_HALO + TM, D), F32), pltpu.VMEM((TM, D), BF16)],
        compiler_params=pltpu.CompilerParams(dimension_semantics=("arbitrary",), vmem_limit_bytes=VMEM_LIMIT),
        name="sconv_mixer",
    )(x, g.reshape(1, D), w_in.astype(BF16), w_conv, w_out.astype(BF16), st)


C_ROW_BLOCK = 16


def _conf_kernel(x_ref, g_ref, w1_ref, b1_ref, dw_ref, bdw_ref, lng_ref, lnb_ref, w2_ref, b2_ref, st_ref,
                 o_ref, tailp_ref, tails_ref, full_s, shift_s, conv_s, y_s):
    i = pl.program_id(0)
    is_sample = i >= N_PROMPT_TILES
    x = x_ref[...]
    h = _rmsnorm(x, g_ref[...]).astype(BF16)
    p = _dot(h, w1_ref[...]) + b1_ref[...]
    gl = p[:, :D] * jax.nn.sigmoid(p[:, D:])

    def conv(n, out_row0):
        n_in = C_HALO + n
        for r in range(1, SUBLANES):
            shift_s[r - 1, 0:n_in - SUBLANES, :] = full_s[r:r + n_in - SUBLANES, :]

        def block(rb, carry):
            base = pl.multiple_of(rb * C_ROW_BLOCK, C_ROW_BLOCK)
            acc = [jnp.zeros((SUBLANES, D), F32) for _ in range(C_ROW_BLOCK // SUBLANES)]
            for k in range(C_WIDTH):
                off = k + (C_HALO - (C_WIDTH - 1))
                row0 = base + (off // SUBLANES) * SUBLANES
                if off % SUBLANES == 0:
                    tap = full_s[pl.ds(row0, C_ROW_BLOCK), :]
                else:
                    tap = shift_s[off % SUBLANES - 1, pl.ds(row0, C_ROW_BLOCK), :]
                wk = dw_ref[k]
                for a in range(len(acc)):
                    acc[a] = acc[a] + wk * tap[a * SUBLANES:(a + 1) * SUBLANES]
            conv_s[pl.ds(out_row0 + base, C_ROW_BLOCK), :] = jnp.concatenate(acc, axis=0)
            return carry

        lax.fori_loop(0, n // C_ROW_BLOCK, block, 0)

    @pl.when(jnp.logical_not(is_sample))
    def _():
        @pl.when(i % TILES_PER_SEQ == 0)
        def _():
            full_s[0:C_HALO, :] = jnp.zeros((C_HALO, D), F32)

        full_s[C_HALO:, :] = gl
        conv(TM, 0)
        tail = full_s[TM:TM + C_HALO, :]
        tailp_ref[0] = tail
        full_s[0:C_HALO, :] = tail

    @pl.when(is_sample)
    def _():
        for b in range(N_SEQ):
            full_s[0:C_HALO, :] = st_ref[b]
            full_s[C_HALO:C_HALO + SAMPLE_SEQ, :] = gl[b * SAMPLE_SEQ:(b + 1) * SAMPLE_SEQ]
            conv(SAMPLE_SEQ, b * SAMPLE_SEQ)
            tails_ref[b] = full_s[SAMPLE_SEQ:SAMPLE_SEQ + C_HALO, :]
        tailp_ref[0] = jnp.zeros((C_HALO, D), F32)

    y = _layernorm(conv_s[...] + bdw_ref[...], lng_ref[...], lnb_ref[...])
    y_s[...] = (y * jax.nn.sigmoid(y)).astype(BF16)
    o_ref[...] = x + _dot(y_s[...], w2_ref[...]) + b2_ref[...]


def _conf_mixer(x, g, w_pw1, b_pw1, w_dw, b_dw, ln_g, ln_b, w_pw2, b_pw2, state):
    st = jnp.pad(state, ((0, 0), (C_HALO - (C_WIDTH - 1), 0), (0, 0)))
    dw = jnp.broadcast_to(w_dw[:, None, :], (C_WIDTH, SUBLANES, D))
    return pl.pallas_call(
        _conf_kernel,
        grid=(N_TILES,),
        in_specs=[
            pl.BlockSpec((TM, D), lambda i: (i, 0)),
            _const_spec((1, D)),
            _const_spec((D, 2 * D)),
            _const_spec((1, 2 * D)),
            _const_spec((C_WIDTH, SUBLANES, D)),
            _const_spec((1, D)),
            _const_spec((1, D)),
            _const_spec((1, D)),
            _const_spec((D, D)),
            _const_spec((1, D)),
            _const_spec((N_SEQ, C_HALO, D)),
        ],
        out_specs=[
            pl.BlockSpec((TM, D), lambda i: (i, 0)),
            pl.BlockSpec((1, C_HALO, D), lambda i: (i, 0, 0)),
            pl.BlockSpec((N_SEQ, C_HALO, D), lambda i: (0, 0, 0)),
        ],
        out_shape=[
            jax.ShapeDtypeStruct((N_TOK, D), F32),
            jax.ShapeDtypeStruct((N_TILES, C_HALO, D), F32),
            jax.ShapeDtypeStruct((N_SEQ, C_HALO, D), F32),
        ],
        scratch_shapes=[
            pltpu.VMEM((C_HALO + TM, D), F32),
            pltpu.VMEM((SUBLANES - 1, C_HALO + TM, D), F32),
            pltpu.VMEM((TM, D), F32),
            pltpu.VMEM((TM, D), BF16),
        ],
        compiler_params=pltpu.CompilerParams(dimension_semantics=("arbitrary",), vmem_limit_bytes=VMEM_LIMIT),
        name="conf_mixer",
    )(x, g.reshape(1, D), w_pw1.astype(BF16), b_pw1.reshape(1, 2 * D), dw, b_dw.reshape(1, D),
      ln_g.reshape(1, D), ln_b.reshape(1, D), w_pw2.astype(BF16), b_pw2.reshape(1, D), st)


def _swiglu_hidden(h, wg_ref, wu_ref, hid_s, lead=()):
    for c0, cn in FF_CHUNKS:
        a = _dot(h, wg_ref[lead + (slice(None), slice(c0, c0 + cn))])
        b = _dot(h, wu_ref[lead + (slice(None), slice(c0, c0 + cn))])
        hid_s[:, c0:c0 + cn] = (a * jax.nn.sigmoid(a) * b).astype(BF16)


def _ffn_kernel(x_ref, g_ref, wg_ref, wu_ref, wd_ref, o_ref, hid_s):
    x = x_ref[...]
    h = _rmsnorm(x, g_ref[...]).astype(BF16)
    _swiglu_hidden(h, wg_ref, wu_ref, hid_s)
    o_ref[...] = x + _dot(hid_s[...], wd_ref[...])


def _dense_ffn(x, g, w_gate, w_up, w_down):
    return pl.pallas_call(
        _ffn_kernel,
        grid=(N_TILES,),
        in_specs=[
            pl.BlockSpec((TM, D), lambda i: (i, 0)),
            _const_spec((1, D)),
            _const_spec((D, D_FF)),
            _const_spec((D, D_FF)),
            _const_spec((D_FF, D)),
        ],
        out_specs=pl.BlockSpec((TM, D), lambda i: (i, 0)),
        out_shape=jax.ShapeDtypeStruct((N_TOK, D), F32),
        scratch_shapes=[pltpu.VMEM((TM, D_FF), BF16)],
        compiler_params=pltpu.CompilerParams(dimension_semantics=("arbitrary",), vmem_limit_bytes=VMEM_LIMIT),
        name="dense_ffn",
    )(x, g.reshape(1, D), w_gate.astype(BF16), w_up.astype(BF16), w_down.astype(BF16))


def _router_kernel(x_ref, g_ref, wr_ref, meta_ref, wts_ref, cnt_ref, xtok_ref, run_s):
    i = pl.program_id(0)

    @pl.when(i == 0)
    def _():
        run_s[...] = jnp.zeros((1, LANES), F32)

    x = x_ref[...]
    _to_token_major(xtok_ref, x, TM)
    h = _rmsnorm(x, g_ref[...])
    logits = jnp.dot(h, wr_ref[...], preferred_element_type=F32, precision=lax.Precision.HIGHEST)
    lane = lax.broadcasted_iota(jnp.int32, (TM, LANES), 1)
    lane_f = lane.astype(F32)
    neg = jnp.float32(-jnp.inf)
    logits = jnp.where(lane < N_EXPERTS, logits, neg)
    l1 = jnp.max(logits, axis=-1, keepdims=True)
    e1 = jnp.min(jnp.where(logits == l1, lane_f, float(LANES)), axis=-1, keepdims=True).astype(jnp.int32)
    rest = jnp.where(lane == e1, neg, logits)
    l2 = jnp.max(rest, axis=-1, keepdims=True)
    e2 = jnp.min(jnp.where(rest == l2, lane_f, float(LANES)), axis=-1, keepdims=True).astype(jnp.int32)
    t = jnp.exp(l2 - l1)
    w1 = 1.0 / (1.0 + t)
    w2 = t * w1
    sel = jnp.logical_or(lane == e1, lane == e2)
    onehot = jnp.where(sel, 1.0, 0.0)
    r = lax.broadcasted_iota(jnp.int32, (TM, TM), 0)
    c = lax.broadcasted_iota(jnp.int32, (TM, TM), 1)
    below = jnp.where(r > c, 1.0, 0.0).astype(BF16)
    before = _dot(below, onehot.astype(BF16)) + run_s[...]
    r1 = jnp.sum(jnp.where(lane == e1, before, 0.0), axis=-1, keepdims=True).astype(jnp.int32)
    r2 = jnp.sum(jnp.where(lane == e2, before, 0.0), axis=-1, keepdims=True).astype(jnp.int32)
    run_s[...] = run_s[...] + jnp.sum(onehot, axis=0, keepdims=True)
    meta = jnp.where(lane == 0, e1, jnp.where(lane == 1, e2, jnp.where(lane == 2, r1, jnp.where(lane == 3, r2, 0))))
    meta_ref[...] = meta
    wts_ref[...] = jnp.where(lane == 0, w1, jnp.where(lane == 1, w2, 0.0))
    cnt_ref[...] = run_s[...].astype(jnp.int32)


def _router(x, g, w_router):
    wr = jnp.pad(w_router, ((0, 0), (0, LANES - N_EXPERTS)))
    return pl.pallas_call(
        _router_kernel,
        grid=(N_TILES,),
        in_specs=[
            pl.BlockSpec((TM, D), lambda i: (i, 0)),
            _const_spec((1, D)),
            _const_spec((D, LANES)),
        ],
        out_specs=[
            pl.BlockSpec((TM, LANES), lambda i: (i, 0)),
            pl.BlockSpec((TM, LANES), lambda i: (i, 0)),
            pl.BlockSpec((1, LANES), lambda i: (0, 0)),
            pl.BlockSpec((TM * ROW_TILE, LANES), lambda i: (i, 0)),
        ],
        out_shape=[
            jax.ShapeDtypeStruct((N_TOK, LANES), jnp.int32),
            jax.ShapeDtypeStruct((N_TOK, LANES), F32),
            jax.ShapeDtypeStruct((1, LANES), jnp.int32),
            jax.ShapeDtypeStruct((N_TOK * ROW_TILE, LANES), F32),
        ],
        scratch_shapes=[pltpu.VMEM((1, LANES), F32)],
        compiler_params=pltpu.CompilerParams(dimension_semantics=("arbitrary",), vmem_limit_bytes=VMEM_LIMIT),
        name="moe_router",
    )(x, g.reshape(1, D), wr)


ROW_TILE = D // LANES
assert ROW_TILE == SUBLANES


def _from_token_major(ref, n):
    return jnp.concatenate([ref[pl.ds(s, n, stride=ROW_TILE), :] for s in range(ROW_TILE)], axis=-1)


def _to_token_major(ref, val, n):
    for s in range(ROW_TILE):
        ref[pl.ds(s, n, stride=ROW_TILE), :] = val[:, s * LANES:(s + 1) * LANES]


def _row_copy(src, src_row, dst, dst_row, sem):
    s0 = pl.multiple_of(src_row * ROW_TILE, ROW_TILE)
    d0 = pl.multiple_of(dst_row * ROW_TILE, ROW_TILE)
    return pltpu.make_async_copy(src.at[pl.ds(s0, ROW_TILE), :], dst.at[pl.ds(d0, ROW_TILE), :], sem)


DMA_UNROLL = 8


CODE_SHIFT = 16
CODE_MASK = (1 << CODE_SHIFT) - 1
N_PAD_BLOCKS = N_EXPERTS + 1
assert N_TOK <= CODE_MASK and N_PAD_BLOCKS * BM <= N_TOK
N_OUT_ROWS = 2 * N_TOK + N_PAD_BLOCKS * BM
DOWN_CHUNKS = tuple((c, MXU_COLS) for c in range(0, D, MXU_COLS))


def _looped(n, fn):
    def body(j, carry):
        for u in range(DMA_UNROLL):
            fn(j * DMA_UNROLL + u)
        return carry

    lax.fori_loop(0, n // DMA_UNROLL, body, 0)


def _gmm_kernel(be_ref, nu_ref, last_ref, nblk_ref, dest_ref, xtok_hbm, g_ref, wg_ref, wu_ref, wd_ref, y2_hbm,
                xbuf, ybuf, hid_s, code_s, gsem, ssem, zsem):
    i = pl.program_id(0)
    nu = nu_ref[0]
    slot = i % 2
    other = 1 - slot
    nxt = jnp.minimum(i + 1, nu - 1)

    def gather(blk, r, s):
        src = code_s[blk * BM + r] & CODE_MASK
        return _row_copy(xtok_hbm, src, xbuf.at[s], r, gsem.at[s])

    def scatter(blk, r, s):
        code = code_s[blk * BM + r]
        return _row_copy(ybuf.at[s], r, y2_hbm, (code >> CODE_SHIFT) * N_TOK + (code & CODE_MASK), ssem.at[s])

    def wait_gathers(s):
        _looped(BM, lambda r: gather(0, 0, s).wait())

    def wait_scatters(s):
        _looped(BM, lambda r: scatter(0, 0, s).wait())

    prev = jnp.where(i == 0, N_BLOCKS, i - 1)

    @pl.when(i == 0)
    def _():
        ybuf[...] = jnp.zeros((2, BM * ROW_TILE, LANES), F32)
        zero = [pltpu.make_async_copy(
            ybuf.at[0], y2_hbm.at[pl.ds((2 * N_TOK + e * BM) * ROW_TILE, BM * ROW_TILE), :], zsem)
            for e in range(N_PAD_BLOCKS)]
        for cp in zero:
            cp.start()
        def pad_codes(first_row, region):
            def fill(r):
                code_s[first_row + r] = (2 << CODE_SHIFT) | (region * BM + r)
            _looped(BM, fill)

        pad_codes(N_ROWS, N_EXPERTS)
        for e in range(N_EXPERTS):
            pl.when(nblk_ref[e] > 0)(functools.partial(pad_codes, last_ref[e] * BM, e))

        def plan_tile(tile, carry):
            def plan(tt):
                t = tile * TM + tt
                code_s[dest_ref[tile, 2 * tt]] = t
                code_s[dest_ref[tile, 2 * tt + 1]] = t | (1 << CODE_SHIFT)
            _looped(TM, plan)
            return carry

        lax.fori_loop(0, N_TILES, plan_tile, 0)
        for cp in zero:
            cp.wait()
        _looped(BM, lambda r: gather(0, r, 0).start())

    def compute():
        xs = xbuf.at[slot]
        ys = ybuf.at[slot]
        _to_token_major(xs, _rmsnorm(_from_token_major(xs, BM), g_ref[...]), BM)
        n_pin = 2 * len(FF_CHUNKS) - 1
        per = -(-BM // n_pin)
        pin = 0
        for ci, (c0, cn) in enumerate(FF_CHUNKS):
            halves = []
            for w_ref in (wg_ref, wu_ref):
                if halves or ci > 0:
                    for r in range(pin * per, min((pin + 1) * per, BM)):
                        gather(nxt, r, other).start()
                    pin += 1
                hb = _from_token_major(xs, BM).astype(BF16)
                halves.append(_dot(hb, w_ref[0, :, c0:c0 + cn]))
            a, b = halves
            hid_s[:, c0:c0 + cn] = (a * jax.nn.sigmoid(a) * b).astype(BF16)
        per = BM // len(DOWN_CHUNKS)
        for ni, (c0, cn) in enumerate(DOWN_CHUNKS):
            for r in range(ni * per, (ni + 1) * per):
                scatter(prev, r, other).start()
            y = _dot(hid_s[...], wd_ref[0, :, c0:c0 + cn])
            for s in range(cn // LANES):
                ys[pl.ds(c0 // LANES + s, BM, stride=ROW_TILE), :] = y[:, s * LANES:(s + 1) * LANES]

    @pl.when(i < nu)
    def _():
        wait_gathers(slot)
        pl.when(i >= 1)(lambda: wait_scatters(slot))
        compute()

        @pl.when(i == nu - 1)
        def _():
            _looped(BM, lambda r: scatter(i, r, slot).start())
            wait_scatters(slot)
            wait_scatters(other)
            wait_gathers(other)


def _gmm(xtok, dest, g, block_expert, n_used, last_block, n_blocks, w_gate, w_up, w_down):
    wspec = lambda shape: pl.BlockSpec((1,) + shape, lambda i, be, *_: (be[i], 0, 0))
    return pl.pallas_call(
        _gmm_kernel,
        grid_spec=pltpu.PrefetchScalarGridSpec(
            num_scalar_prefetch=4,
            grid=(N_BLOCKS,),
            in_specs=[
                pl.BlockSpec(memory_space=pltpu.SMEM),
                pl.BlockSpec(memory_space=pl.ANY),
                pl.BlockSpec((1, D), lambda i, *_: (0, 0)),
                wspec((D, D_FF)),
                wspec((D, D_FF)),
                wspec((D_FF, D)),
            ],
            out_specs=pl.BlockSpec(memory_space=pl.ANY),
            scratch_shapes=[
                pltpu.VMEM((2, BM * ROW_TILE, LANES), F32),
                pltpu.VMEM((2, BM * ROW_TILE, LANES), F32),
                pltpu.VMEM((BM, D_FF), BF16),
                pltpu.SMEM((N_ROWS + BM,), jnp.int32),
                pltpu.SemaphoreType.DMA((2,)),
                pltpu.SemaphoreType.DMA((2,)),
                pltpu.SemaphoreType.DMA(()),
            ],
        ),
        out_shape=jax.ShapeDtypeStruct((N_OUT_ROWS * ROW_TILE, LANES), F32),
        compiler_params=pltpu.CompilerParams(dimension_semantics=("arbitrary",), vmem_limit_bytes=VMEM_LIMIT),
        name="moe_gmm",
    )(block_expert, n_used, last_block, n_blocks, dest.reshape(N_TILES, 2 * TM), xtok, g.reshape(1, D),
      w_gate.astype(BF16), w_up.astype(BF16), w_down.astype(BF16))


def _combine_kernel(x_ref, wts_ref, gf_ref, y1_ref, y2_ref, *out_refs, final):
    w = wts_ref[...]
    y = x_ref[...] + w[:, 0:1] * _from_token_major(y1_ref, TM) + w[:, 1:2] * _from_token_major(y2_ref, TM)
    if final:
        y = _rmsnorm(y, gf_ref[...])
        yp_ref, ysm_ref = out_refs
        is_sample = pl.program_id(0) >= N_PROMPT_TILES

        @pl.when(jnp.logical_not(is_sample))
        def _():
            yp_ref[...] = y

        @pl.when(is_sample)
        def _():
            ysm_ref[...] = y
    else:
        out_refs[0][...] = y


def _combine(x, wts, y2, g_final, final):
    if final:
        out_specs = [pl.BlockSpec((TM, D), lambda i: (jnp.minimum(i, N_PROMPT_TILES - 1), 0)),
                     pl.BlockSpec((TM, D), lambda i: (0, 0))]
        out_shape = [jax.ShapeDtypeStruct((N_PROMPT_TOK, D), F32), jax.ShapeDtypeStruct((N_SAMPLE_TOK, D), F32)]
    else:
        out_specs = [pl.BlockSpec((TM, D), lambda i: (i, 0))]
        out_shape = [jax.ShapeDtypeStruct((N_TOK, D), F32)]
    outs = pl.pallas_call(
        functools.partial(_combine_kernel, final=final),
        grid=(N_TILES,),
        in_specs=[
            pl.BlockSpec((TM, D), lambda i: (i, 0)),
            pl.BlockSpec((TM, LANES), lambda i: (i, 0)),
            _const_spec((1, D)),
            pl.BlockSpec((TM * ROW_TILE, LANES), lambda i: (i, 0)),
            pl.BlockSpec((TM * ROW_TILE, LANES), lambda i: (i + N_TILES, 0)),
        ],
        out_specs=out_specs,
        out_shape=out_shape,
        compiler_params=pltpu.CompilerParams(dimension_semantics=("arbitrary",), vmem_limit_bytes=VMEM_LIMIT),
        name="moe_combine",
    )(x, wts, g_final.reshape(1, D), y2, y2)
    return outs if final else outs[0]


def _moe_ffn(x, g, w_router, w_gate, w_up, w_down, g_final, final):
    meta, wts, counts, xtok = _router(x, g, w_router)
    counts = counts[0, :N_EXPERTS]
    blocks = (counts + BM - 1) // BM
    block_end = jnp.cumsum(blocks)
    seg_start = (block_end - blocks) * BM
    dest = seg_start[meta[:, 0:2]] + meta[:, 2:4]
    n_used = block_end[-1:]
    bidx = jnp.minimum(jnp.arange(N_BLOCKS, dtype=jnp.int32), n_used - 1)
    block_expert = jnp.sum(bidx[:, None] >= block_end[None, :], axis=1).astype(jnp.int32)
    n_used = n_used.astype(jnp.int32)
    y2 = _gmm(xtok, dest, g, block_expert, n_used, (block_end - 1).astype(jnp.int32), blocks.astype(jnp.int32),
              w_gate, w_up, w_down)
    return _combine(x, wts, y2, g_final, final)


def kernel(x_prompt, x_sample, state_conv_b, state_conv_c, norm_mix, norm_ffn, norm_final, a_w_in, a_ln_g, a_ln_b, a_w_s, a_b_s, a_w_out, b_w_in, b_conv, b_w_out, c_w_pw1, c_b_pw1, c_dw, c_b_dw, c_ln_g, c_ln_b, c_w_pw2, c_b_pw2, f_w_gate, f_w_up, f_w_down, m_router, m_w_gate, m_w_up, m_w_down):
    last_tile = jnp.arange(N_SEQ) * TILES_PER_SEQ + TILES_PER_SEQ - 1

    x, v0 = _gmlp_mixer(x_prompt.reshape(N_PROMPT_TOK, D), x_sample.reshape(N_SAMPLE_TOK, D), 0,
                        norm_mix[0], a_w_in[0], a_ln_g[0], a_ln_b[0], a_w_s[0], a_b_s[0], a_w_out[0])
    x = _dense_ffn(x, norm_ffn[0], f_w_gate[0], f_w_up[0], f_w_down[0])
    x, tb_p, tb_s = _sconv_mixer(x, norm_mix[1], b_w_in[0], b_conv[0], b_w_out[0], state_conv_b[0])
    x = _moe_ffn(x, norm_ffn[1], m_router[0], m_w_gate[0], m_w_up[0], m_w_down[0], norm_final, False)
    x, tc_p, tc_s = _conf_mixer(x, norm_mix[2], c_w_pw1[0], c_b_pw1[0], c_dw[0], c_b_dw[0], c_ln_g[0], c_ln_b[0],
                                c_w_pw2[0], c_b_pw2[0], state_conv_c[0])
    x = _dense_ffn(x, norm_ffn[2], f_w_gate[1], f_w_up[1], f_w_down[1])
    x, v1 = _gmlp_mixer(x, x, N_PROMPT_TILES,
                        norm_mix[3], a_w_in[1], a_ln_g[1], a_ln_b[1], a_w_s[1], a_b_s[1], a_w_out[1])
    y_prompt, y_sample = _moe_ffn(x, norm_ffn[3], m_router[1], m_w_gate[1], m_w_up[1], m_w_down[1], norm_final, True)

    y_prompt = y_prompt.reshape(N_SEQ, SEQ, D)
    y_sample = y_sample.reshape(N_SEQ, SAMPLE_SEQ, D)
    nb = B_WIDTH - 1
    nc = C_WIDTH - 1
    new_b_p = tb_p[last_tile, B_HALO - nb:, :][None]
    new_b_s = tb_s[:, B_HALO - nb:, :][None]
    new_c_p = tc_p[last_tile, C_HALO - nc:, :][None]
    new_c_s = tc_s[:, C_HALO - nc:, :][None]
    new_v = jnp.stack([v0, v1]).reshape(2, N_SEQ, SAMPLE_SEQ, A_HALF)
    return (y_prompt, y_sample, new_b_p, new_b_s, new_c_p, new_c_s, new_v)
```

```python
import functools

import jax
import jax.numpy as jnp
from jax import lax
from jax.experimental import pallas as pl
from jax.experimental.pallas import tpu as pltpu

F32 = jnp.float32
BF16 = jnp.bfloat16

D = 1024
TM = 512
SEQ = 4096
SAMPLE_SEQ = 64
N_SEQ = 8
TILES_PER_SEQ = SEQ // TM
N_PROMPT_TOK = N_SEQ * SEQ
N_SAMPLE_TOK = N_SEQ * SAMPLE_SEQ
N_TOK = N_PROMPT_TOK + N_SAMPLE_TOK
N_PROMPT_TILES = N_PROMPT_TOK // TM
N_TILES = N_TOK // TM
assert N_SAMPLE_TOK == TM and N_TILES == N_PROMPT_TILES + 1

A_HALF = 2 * D
A_GROUPS = 8
A_HEAD = A_HALF // A_GROUPS
A_CHUNK = 128
B_WIDTH = 3
C_WIDTH = 31
B_HALO = 8
C_HALO = 32
D_FF = 2816
N_EXPERTS = 8
LANES = 128
SUBLANES = 8
MXU_COLS = 256
RMS_EPS = 1e-6
LN_EPS = 1e-5

BM = TM
N_ROWS = 2 * N_TOK + N_EXPERTS * BM
N_BLOCKS = N_ROWS // BM
FF_CHUNKS = ((0, 1024), (1024, 1024), (2048, 768))

VMEM_LIMIT = 56 * 1024 * 1024


def _const_spec(shape):
    return pl.BlockSpec(shape, lambda *_: (0,) * len(shape), pipeline_mode=pl.Buffered(1))


def _rmsnorm(x, g):
    return x * lax.rsqrt(jnp.mean(x * x, axis=-1, keepdims=True) + RMS_EPS) * g


def _layernorm(x, g, b):
    mu = jnp.mean(x, axis=-1, keepdims=True)
    xc = x - mu
    var = jnp.mean(xc * xc, axis=-1, keepdims=True)
    return xc * lax.rsqrt(var + LN_EPS) * g + b


def _dot(a, b):
    return jnp.dot(a, b, preferred_element_type=F32)


def _gmlp_kernel(xp_ref, xs_ref, g_ref, win_ref, lng_ref, lnb_ref, ws_ref, bs_ref, wout_ref,
                 o_ref, v_ref, vn_s, y_s):
    is_sample = pl.program_id(0) >= N_PROMPT_TILES
    x = jnp.where(is_sample, xs_ref[...], xp_ref[...])
    h = _rmsnorm(x, g_ref[...]).astype(BF16)
    v = jax.nn.gelu(_dot(h, win_ref[:, A_HALF:]))
    vn = _layernorm(v, lng_ref[...], lnb_ref[...])

    @pl.when(is_sample)
    def _():
        v_ref[...] = vn

    vn_s[...] = vn.astype(BF16)
    r = lax.broadcasted_iota(jnp.int32, (A_CHUNK, A_CHUNK), 0)
    c = lax.broadcasted_iota(jnp.int32, (A_CHUNK, A_CHUNK), 1)
    seg_shift = jnp.where(is_sample, 6, 7)
    mask = (r >= c) & ((r >> seg_shift) == (c >> seg_shift))
    for g in range(A_GROUPS):
        cols = slice(g * A_HEAD, (g + 1) * A_HEAD)
        u_g = jax.nn.gelu(_dot(h, win_ref[:, cols]))
        ws = jnp.where(mask, ws_ref[0, g], 0.0).astype(BF16)
        bias = jnp.concatenate([bs_ref[0, g]] * (A_HEAD // LANES), axis=1)
        for ch in range(TM // A_CHUNK):
            rows = slice(ch * A_CHUNK, (ch + 1) * A_CHUNK)
            s = _dot(ws, vn_s[rows, cols]) + bias
            y_s[rows, cols] = (u_g[rows] * s).astype(BF16)
    o_ref[...] = x + _dot(y_s[...], wout_ref[...])


def _gmlp_mixer(x_prompt, x_sample, sample_block, g, w_in, ln_g, ln_b, w_s, b_s, w_out):
    ws2 = jnp.stack([w_s, jnp.tile(w_s[:, :SAMPLE_SEQ, :SAMPLE_SEQ], (1, 2, 2))])
    b2 = jnp.stack([b_s, jnp.tile(b_s[:, :SAMPLE_SEQ], (1, 2))])
    b2 = jnp.broadcast_to(b2[..., None], (2, A_GROUPS, A_CHUNK, LANES))
    return pl.pallas_call(
        _gmlp_kernel,
        grid=(N_TILES,),
        in_specs=[
            pl.BlockSpec((TM, D), lambda i: (jnp.minimum(i, N_PROMPT_TILES - 1), 0)),
            pl.BlockSpec((TM, D), lambda i: (sample_block, 0)),
            _const_spec((1, D)),
            _const_spec((D, 2 * A_HALF)),
            _const_spec((1, A_HALF)),
            _const_spec((1, A_HALF)),
            pl.BlockSpec((1, A_GROUPS, A_CHUNK, A_CHUNK), lambda i: (i // N_PROMPT_TILES, 0, 0, 0)),
            pl.BlockSpec((1, A_GROUPS, A_CHUNK, LANES), lambda i: (i // N_PROMPT_TILES, 0, 0, 0)),
            _const_spec((A_HALF, D)),
        ],
        out_specs=[
            pl.BlockSpec((TM, D), lambda i: (i, 0)),
            pl.BlockSpec((TM, A_HALF), lambda i: (0, 0)),
        ],
        out_shape=[
            jax.ShapeDtypeStruct((N_TOK, D), F32),
            jax.ShapeDtypeStruct((N_SAMPLE_TOK, A_HALF), F32),
        ],
        scratch_shapes=[pltpu.VMEM((TM, A_HALF), BF16), pltpu.VMEM((TM, A_HALF), BF16)],
        compiler_params=pltpu.CompilerParams(dimension_semantics=("arbitrary",), vmem_limit_bytes=VMEM_LIMIT),
        name="gmlp_mixer",
    )(x_prompt, x_sample, g.reshape(1, D), w_in.astype(BF16), ln_g.reshape(1, A_HALF), ln_b.reshape(1, A_HALF),
      ws2, b2, w_out.astype(BF16))


def _sconv_kernel(x_ref, g_ref, win_ref, cw_ref, wout_ref, st_ref,
                  o_ref, tailp_ref, tails_ref, full_s, y_s):
    i = pl.program_id(0)
    is_sample = i >= N_PROMPT_TILES
    x = x_ref[...]
    h = _rmsnorm(x, g_ref[...]).astype(BF16)
    p = _dot(h, win_ref[...])
    bg = p[:, :D]
    u = p[:, D:2 * D] * p[:, 2 * D:]
    w = [cw_ref[k:k + 1, :] for k in range(B_WIDTH)]

    def conv(n):
        return sum(w[k] * full_s[pl.ds(B_HALO - (B_WIDTH - 1) + k, n), :] for k in range(B_WIDTH))

    @pl.when(jnp.logical_not(is_sample))
    def _():
        @pl.when(i % TILES_PER_SEQ == 0)
        def _():
            full_s[0:B_HALO, :] = jnp.zeros((B_HALO, D), F32)

        full_s[B_HALO:, :] = u
        y_s[...] = (bg * conv(TM)).astype(BF16)
        tail = full_s[TM:TM + B_HALO, :]
        tailp_ref[0] = tail
        full_s[0:B_HALO, :] = tail

    @pl.when(is_sample)
    def _():
        for b in range(N_SEQ):
            rows = slice(b * SAMPLE_SEQ, (b + 1) * SAMPLE_SEQ)
            full_s[0:B_HALO, :] = st_ref[b]
            full_s[B_HALO:B_HALO + SAMPLE_SEQ, :] = u[rows]
            y_s[rows, :] = (bg[rows] * conv(SAMPLE_SEQ)).astype(BF16)
            tails_ref[b] = full_s[SAMPLE_SEQ:SAMPLE_SEQ + B_HALO, :]
        tailp_ref[0] = jnp.zeros((B_HALO, D), F32)

    o_ref[...] = x + _dot(y_s[...], wout_ref[...])


def _sconv_mixer(x, g, w_in, w_conv, w_out, state):
    st = jnp.pad(state, ((0, 0), (B_HALO - (B_WIDTH - 1), 0), (0, 0)))
    return pl.pallas_call(
        _sconv_kernel,
        grid=(N_TILES,),
        in_specs=[
            pl.BlockSpec((TM, D), lambda i: (i, 0)),
            _const_spec((1, D)),
            _const_spec((D, 3 * D)),
            _const_spec((B_WIDTH, D)),
            _const_spec((D, D)),
            _const_spec((N_SEQ, B_HALO, D)),
        ],
        out_specs=[
            pl.BlockSpec((TM, D), lambda i: (i, 0)),
            pl.BlockSpec((1, B_HALO, D), lambda i: (i, 0, 0)),
            pl.BlockSpec((N_SEQ, B_HALO, D), lambda i: (0, 0, 0)),
        ],
        out_shape=[
            jax.ShapeDtypeStruct((N_TOK, D), F32),
            jax.ShapeDtypeStruct((N_TILES, B_HALO, D), F32),
            jax.ShapeDtypeStruct((N_SEQ, B_HALO, D), F32),
        ],
        scratch_shapes=[pltpu.VMEM((B_HALO + TM, D), F32), pltpu.VMEM((TM, D), BF16)],
        compiler_params=pltpu.CompilerParams(dimension_semantics=("arbitrary",), vmem_limit_bytes=VMEM_LIMIT),
        name="sconv_mixer",
    )(x, g.reshape(1, D), w_in.astype(BF16), w_conv, w_out.astype(BF16), st)


C_ROW_BLOCK = 16


def _conf_kernel(x_ref, g_ref, w1_ref, b1_ref, dw_ref, bdw_ref, lng_ref, lnb_ref, w2_ref, b2_ref, st_ref,
                 o_ref, tailp_ref, tails_ref, full_s, shift_s, conv_s, y_s):
    i = pl.program_id(0)
    is_sample = i >= N_PROMPT_TILES
    x = x_ref[...]
    h = _rmsnorm(x, g_ref[...]).astype(BF16)
    p = _dot(h, w1_ref[...]) + b1_ref[...]
    gl = p[:, :D] * jax.nn.sigmoid(p[:, D:])

    def conv(n, out_row0):
        n_in = C_HALO + n
        for r in range(1, SUBLANES):
            shift_s[r - 1, 0:n_in - SUBLANES, :] = full_s[r:r + n_in - SUBLANES, :]

        def block(rb, carry):
            base = pl.multiple_of(rb * C_ROW_BLOCK, C_ROW_BLOCK)
            acc = [jnp.zeros((SUBLANES, D), F32) for _ in range(C_ROW_BLOCK // SUBLANES)]
            for k in range(C_WIDTH):
                off = k + (C_HALO - (C_WIDTH - 1))
                row0 = base + (off // SUBLANES) * SUBLANES
                if off % SUBLANES == 0:
                    tap = full_s[pl.ds(row0, C_ROW_BLOCK), :]
                else:
                    tap = shift_s[off % SUBLANES - 1, pl.ds(row0, C_ROW_BLOCK), :]
                wk = dw_ref[k]
                for a in range(len(acc)):
                    acc[a] = acc[a] + wk * tap[a * SUBLANES:(a + 1) * SUBLANES]
            conv_s[pl.ds(out_row0 + base, C_ROW_BLOCK), :] = jnp.concatenate(acc, axis=0)
            return carry

        lax.fori_loop(0, n // C_ROW_BLOCK, block, 0)

    @pl.when(jnp.logical_not(is_sample))
    def _():
        @pl.when(i % TILES_PER_SEQ == 0)
        def _():
            full_s[0:C_HALO, :] = jnp.zeros((C_HALO, D), F32)

        full_s[C_HALO:, :] = gl
        conv(TM, 0)
        tail = full_s[TM:TM + C_HALO, :]
        tailp_ref[0] = tail
        full_s[0:C_HALO, :] = tail

    @pl.when(is_sample)
    def _():
        for b in range(N_SEQ):
            full_s[0:C_HALO, :] = st_ref[b]
            full_s[C_HALO:C_HALO + SAMPLE_SEQ, :] = gl[b * SAMPLE_SEQ:(b + 1) * SAMPLE_SEQ]
            conv(SAMPLE_SEQ, b * SAMPLE_SEQ)
            tails_ref[b] = full_s[SAMPLE_SEQ:SAMPLE_SEQ + C_HALO, :]
        tailp_ref[0] = jnp.zeros((C_HALO, D), F32)

    y = _layernorm(conv_s[...] + bdw_ref[...], lng_ref[...], lnb_ref[...])
    y_s[...] = (y * jax.nn.sigmoid(y)).astype(BF16)
    o_ref[...] = x + _dot(y_s[...], w2_ref[...]) + b2_ref[...]


def _conf_mixer(x, g, w_pw1, b_pw1, w_dw, b_dw, ln_g, ln_b, w_pw2, b_pw2, state):
    st = jnp.pad(state, ((0, 0), (C_HALO - (C_WIDTH - 1), 0), (0, 0)))
    dw = jnp.broadcast_to(w_dw[:, None, :], (C_WIDTH, SUBLANES, D))
    return pl.pallas_call(
        _conf_kernel,
        grid=(N_TILES,),
        in_specs=[
            pl.BlockSpec((TM, D), lambda i: (i, 0)),
            _const_spec((1, D)),
            _const_spec((D, 2 * D)),
            _const_spec((1, 2 * D)),
            _const_spec((C_WIDTH, SUBLANES, D)),
            _const_spec((1, D)),
            _const_spec((1, D)),
            _const_spec((1, D)),
            _const_spec((D, D)),
            _const_spec((1, D)),
            _const_spec((N_SEQ, C_HALO, D)),
        ],
        out_specs=[
            pl.BlockSpec((TM, D), lambda i: (i, 0)),
            pl.BlockSpec((1, C_HALO, D), lambda i: (i, 0, 0)),
            pl.BlockSpec((N_SEQ, C_HALO, D), lambda i: (0, 0, 0)),
        ],
        out_shape=[
            jax.ShapeDtypeStruct((N_TOK, D), F32),
            jax.ShapeDtypeStruct((N_TILES, C_HALO, D), F32),
            jax.ShapeDtypeStruct((N_SEQ, C_HALO, D), F32),
        ],
        scratch_shapes=[
            pltpu.VMEM((C_HALO + TM, D), F32),
            pltpu.VMEM((SUBLANES - 1, C_HALO + TM, D), F32),
            pltpu.VMEM((TM, D), F32),
            pltpu.VMEM((TM, D), BF16),
        ],
        compiler_params=pltpu.CompilerParams(dimension_semantics=("arbitrary",), vmem_limit_bytes=VMEM_LIMIT),
        name="conf_mixer",
    )(x, g.reshape(1, D), w_pw1.astype(BF16), b_pw1.reshape(1, 2 * D), dw, b_dw.reshape(1, D),
      ln_g.reshape(1, D), ln_b.reshape(1, D), w_pw2.astype(BF16), b_pw2.reshape(1, D), st)


def _swiglu_hidden(h, wg_ref, wu_ref, hid_s, lead=()):
    for c0, cn in FF_CHUNKS:
        a = _dot(h, wg_ref[lead + (slice(None), slice(c0, c0 + cn))])
        b = _dot(h, wu_ref[lead + (slice(None), slice(c0, c0 + cn))])
        hid_s[:, c0:c0 + cn] = (a * jax.nn.sigmoid(a) * b).astype(BF16)


def _ffn_kernel(x_ref, g_ref, wg_ref, wu_ref, wd_ref, o_ref, hid_s):
    x = x_ref[...]
    h = _rmsnorm(x, g_ref[...]).astype(BF16)
    _swiglu_hidden(h, wg_ref, wu_ref, hid_s)
    o_ref[...] = x + _dot(hid_s[...], wd_ref[...])


def _dense_ffn(x, g, w_gate, w_up, w_down):
    return pl.pallas_call(
        _ffn_kernel,
        grid=(N_TILES,),
        in_specs=[
            pl.BlockSpec((TM, D), lambda i: (i, 0)),
            _const_spec((1, D)),
            _const_spec((D, D_FF)),
            _const_spec((D, D_FF)),
            _const_spec((D_FF, D)),
        ],
        out_specs=pl.BlockSpec((TM, D), lambda i: (i, 0)),
        out_shape=jax.ShapeDtypeStruct((N_TOK, D), F32),
        scratch_shapes=[pltpu.VMEM((TM, D_FF), BF16)],
        compiler_params=pltpu.CompilerParams(dimension_semantics=("arbitrary",), vmem_limit_bytes=VMEM_LIMIT),
        name="dense_ffn",
    )(x, g.reshape(1, D), w_gate.astype(BF16), w_up.astype(BF16), w_down.astype(BF16))


def _router_kernel(x_ref, g_ref, wr_ref, meta_ref, wts_ref, cnt_ref, run_s):
    i = pl.program_id(0)

    @pl.when(i == 0)
    def _():
        run_s[...] = jnp.zeros((1, LANES), F32)

    h = _rmsnorm(x_ref[...], g_ref[...])
    logits = _dot(h.astype(BF16), wr_ref[...].astype(BF16))
    lane = lax.broadcasted_iota(jnp.int32, (TM, LANES), 1)
    lane_f = lane.astype(F32)
    neg = jnp.float32(-jnp.inf)
    logits = jnp.where(lane < N_EXPERTS, logits, neg)
    l1 = jnp.max(logits, axis=-1, keepdims=True)
    e1 = jnp.min(jnp.where(logits == l1, lane_f, float(LANES)), axis=-1, keepdims=True).astype(jnp.int32)
    rest = jnp.where(lane == e1, neg, logits)
    l2 = jnp.max(rest, axis=-1, keepdims=True)
    e2 = jnp.min(jnp.where(rest == l2, lane_f, float(LANES)), axis=-1, keepdims=True).astype(jnp.int32)
    t = jnp.exp(l2 - l1)
    w1 = 1.0 / (1.0 + t)
    w2 = t * w1
    sel = jnp.logical_or(lane == e1, lane == e2)
    onehot = jnp.where(sel, 1.0, 0.0)
    r = lax.broadcasted_iota(jnp.int32, (TM, TM), 0)
    c = lax.broadcasted_iota(jnp.int32, (TM, TM), 1)
    below = jnp.where(r > c, 1.0, 0.0).astype(BF16)
    before = _dot(below, onehot.astype(BF16)) + run_s[...]
    r1 = jnp.sum(jnp.where(lane == e1, before, 0.0), axis=-1, keepdims=True).astype(jnp.int32)
    r2 = jnp.sum(jnp.where(lane == e2, before, 0.0), axis=-1, keepdims=True).astype(jnp.int32)
    run_s[...] = run_s[...] + jnp.sum(onehot, axis=0, keepdims=True)
    meta = jnp.where(lane == 0, e1, jnp.where(lane == 1, e2, jnp.where(lane == 2, r1, jnp.where(lane == 3, r2, 0))))
    meta_ref[...] = meta
    wts_ref[...] = jnp.where(lane == 0, w1, jnp.where(lane == 1, w2, 0.0))
    cnt_ref[...] = run_s[...].astype(jnp.int32)


def _router(x, g, w_router):
    wr = jnp.pad(w_router, ((0, 0), (0, LANES - N_EXPERTS)))
    return pl.pallas_call(
        _router_kernel,
        grid=(N_TILES,),
        in_specs=[
            pl.BlockSpec((TM, D), lambda i: (i, 0)),
            _const_spec((1, D)),
            _const_spec((D, LANES)),
        ],
        out_specs=[
            pl.BlockSpec((TM, LANES), lambda i: (i, 0)),
            pl.BlockSpec((TM, LANES), lambda i: (i, 0)),
            pl.BlockSpec((1, LANES), lambda i: (0, 0)),
        ],
        out_shape=[
            jax.ShapeDtypeStruct((N_TOK, LANES), jnp.int32),
            jax.ShapeDtypeStruct((N_TOK, LANES), F32),
            jax.ShapeDtypeStruct((1, LANES), jnp.int32),
        ],
        scratch_shapes=[pltpu.VMEM((1, LANES), F32)],
        compiler_params=pltpu.CompilerParams(dimension_semantics=("arbitrary",), vmem_limit_bytes=VMEM_LIMIT),
        name="moe_router",
    )(x, g.reshape(1, D), wr)


def _row_copy(src, src_row, dst, dst_row, sem):
    return pltpu.make_async_copy(src.at[pl.ds(src_row, 1), :], dst.at[pl.ds(dst_row, 1), :], sem)


DMA_UNROLL = 8


CODE_SHIFT = 16
CODE_MASK = (1 << CODE_SHIFT) - 1
N_PAD_BLOCKS = N_EXPERTS + 1
assert N_TOK <= CODE_MASK and N_PAD_BLOCKS * BM <= N_TOK
N_OUT_ROWS = 2 * N_TOK + N_PAD_BLOCKS * BM
UP_CHUNKS = tuple((c, MXU_COLS) for c in range(0, D_FF, MXU_COLS))
DOWN_CHUNKS = tuple((c, MXU_COLS) for c in range(0, D, MXU_COLS))


def _looped(n, fn):
    def body(j, carry):
        for u in range(DMA_UNROLL):
            fn(j * DMA_UNROLL + u)
        return carry

    lax.fori_loop(0, n // DMA_UNROLL, body, 0)


def _gmm_kernel(be_ref, nu_ref, last_ref, nblk_ref, dest_ref, x_hbm, g_ref, wg_ref, wu_ref, wd_ref, y2_hbm,
                xbuf, ybuf, hid_s, code_s, gsem, ssem, zsem):
    i = pl.program_id(0)
    nu = nu_ref[0]
    slot = i % 2
    other = 1 - slot
    nxt = jnp.minimum(i + 1, nu - 1)

    def gather(blk, r, s):
        src = code_s[blk * BM + r] & CODE_MASK
        return _row_copy(x_hbm, src, xbuf.at[s], r, gsem.at[s])

    def scatter(blk, r, s):
        code = code_s[blk * BM + r]
        return _row_copy(ybuf.at[s], r, y2_hbm, (code >> CODE_SHIFT) * N_TOK + (code & CODE_MASK), ssem.at[s])

    def wait_gathers(s):
        _looped(BM, lambda r: gather(0, 0, s).wait())

    def wait_scatters(s):
        _looped(BM, lambda r: scatter(0, 0, s).wait())

    prev = jnp.where(i == 0, N_BLOCKS, i - 1)

    @pl.when(i == 0)
    def _():
        ybuf[...] = jnp.zeros((2, BM, D), F32)
        zero = [pltpu.make_async_copy(ybuf.at[0], y2_hbm.at[pl.ds(2 * N_TOK + e * BM, BM), :], zsem)
                for e in range(N_PAD_BLOCKS)]
        for cp in zero:
            cp.start()
        def pad_codes(first_row, region):
            def fill(r):
                code_s[first_row + r] = (2 << CODE_SHIFT) | (region * BM + r)
            _looped(BM, fill)

        pad_codes(N_ROWS, N_EXPERTS)
        for e in range(N_EXPERTS):
            pl.when(nblk_ref[e] > 0)(functools.partial(pad_codes, last_ref[e] * BM, e))

        def plan_tile(tile, carry):
            def plan(tt):
                t = tile * TM + tt
                code_s[dest_ref[tile, 2 * tt]] = t
                code_s[dest_ref[tile, 2 * tt + 1]] = t | (1 << CODE_SHIFT)
            _looped(TM, plan)
            return carry

        lax.fori_loop(0, N_TILES, plan_tile, 0)
        for cp in zero:
            cp.wait()
        _looped(BM, lambda r: gather(0, r, 0).start())

    def up_phase():
        xbuf[slot] = _rmsnorm(xbuf[slot], g_ref[...])
        n_pin = 2 * len(UP_CHUNKS) - 1
        per = -(-BM // n_pin)
        pin = 0
        for c0, cn in UP_CHUNKS:
            halves = []
            for w_ref in (wg_ref, wu_ref):
                if halves or c0 > 0:
                    for r in range(pin * per, min((pin + 1) * per, BM)):
                        gather(nxt, r, other).start()
                    pin += 1
                halves.append(_dot(xbuf[slot].astype(BF16), w_ref[0, :, c0:c0 + cn]))
            a, b = halves
            hid_s[:, c0:c0 + cn] = (a * jax.nn.sigmoid(a) * b).astype(BF16)

    def down_phase():
        per = BM // len(DOWN_CHUNKS)
        for ni, (c0, cn) in enumerate(DOWN_CHUNKS):
            for r in range(ni * per, (ni + 1) * per):
                scatter(prev, r, other).start()
            ybuf[slot, :, c0:c0 + cn] = _dot(hid_s[...], wd_ref[0, :, c0:c0 + cn])

    @pl.when(i < nu)
    def _():
        wait_gathers(slot)
        up_phase()
        pl.when(i >= 1)(lambda: wait_scatters(slot))
        down_phase()

        @pl.when(i == nu - 1)
        def _():
            _looped(BM, lambda r: scatter(i, r, slot).start())
            wait_scatters(slot)
            wait_scatters(other)
            wait_gathers(other)


def _gmm(x, dest, g, block_expert, n_used, last_block, n_blocks, w_gate, w_up, w_down):
    wspec = lambda shape: pl.BlockSpec((1,) + shape, lambda i, be, *_: (be[i], 0, 0))
    return pl.pallas_call(
        _gmm_kernel,
        grid_spec=pltpu.PrefetchScalarGridSpec(
            num_scalar_prefetch=4,
            grid=(N_BLOCKS,),
            in_specs=[
                pl.BlockSpec(memory_space=pltpu.SMEM),
                pl.BlockSpec(memory_space=pl.ANY),
                pl.BlockSpec((1, D), lambda i, *_: (0, 0)),
                wspec((D, D_FF)),
                wspec((D, D_FF)),
                wspec((D_FF, D)),
            ],
            out_specs=pl.BlockSpec(memory_space=pl.ANY),
            scratch_shapes=[
                pltpu.VMEM((2, BM, D), F32),
                pltpu.VMEM((2, BM, D), F32),
                pltpu.VMEM((BM, D_FF), BF16),
                pltpu.SMEM((N_ROWS + BM,), jnp.int32),
                pltpu.SemaphoreType.DMA((2,)),
                pltpu.SemaphoreType.DMA((2,)),
                pltpu.SemaphoreType.DMA(()),
            ],
        ),
        out_shape=jax.ShapeDtypeStruct((N_OUT_ROWS, D), F32),
        compiler_params=pltpu.CompilerParams(dimension_semantics=("arbitrary",), vmem_limit_bytes=VMEM_LIMIT),
        name="moe_gmm",
    )(block_expert, n_used, last_block, n_blocks, dest.reshape(N_TILES, 2 * TM), x, g.reshape(1, D),
      w_gate.astype(BF16), w_up.astype(BF16), w_down.astype(BF16))


def _combine_kernel(x_ref, wts_ref, gf_ref, y1_ref, y2_ref, *out_refs, final):
    w = wts_ref[...]
    y = x_ref[...] + w[:, 0:1] * y1_ref[...] + w[:, 1:2] * y2_ref[...]
    if final:
        y = _rmsnorm(y, gf_ref[...])
        yp_ref, ysm_ref = out_refs
        is_sample = pl.program_id(0) >= N_PROMPT_TILES

        @pl.when(jnp.logical_not(is_sample))
        def _():
            yp_ref[...] = y

        @pl.when(is_sample)
        def _():
            ysm_ref[...] = y
    else:
        out_refs[0][...] = y


def _combine(x, wts, y2, g_final, final):
    if final:
        out_specs = [pl.BlockSpec((TM, D), lambda i: (jnp.minimum(i, N_PROMPT_TILES - 1), 0)),
                     pl.BlockSpec((TM, D), lambda i: (0, 0))]
        out_shape = [jax.ShapeDtypeStruct((N_PROMPT_TOK, D), F32), jax.ShapeDtypeStruct((N_SAMPLE_TOK, D), F32)]
    else:
        out_specs = [pl.BlockSpec((TM, D), lambda i: (i, 0))]
        out_shape = [jax.ShapeDtypeStruct((N_TOK, D), F32)]
    outs = pl.pallas_call(
        functools.partial(_combine_kernel, final=final),
        grid=(N_TILES,),
        in_specs=[
            pl.BlockSpec((TM, D), lambda i: (i, 0)),
            pl.BlockSpec((TM, LANES), lambda i: (i, 0)),
            _const_spec((1, D)),
            pl.BlockSpec((TM, D), lambda i: (i, 0)),
            pl.BlockSpec((TM, D), lambda i: (i + N_TILES, 0)),
        ],
        out_specs=out_specs,
        out_shape=out_shape,
        compiler_params=pltpu.CompilerParams(dimension_semantics=("arbitrary",), vmem_limit_bytes=VMEM_LIMIT),
        name="moe_combine",
    )(x, wts, g_final.reshape(1, D), y2, y2)
    return outs if final else outs[0]


def _moe_ffn(x, g, w_router, w_gate, w_up, w_down, g_final, final):
    meta, wts, counts = _router(x, g, w_router)
    counts = counts[0, :N_EXPERTS]
    blocks = (counts + BM - 1) // BM
    block_end = jnp.cumsum(blocks)
    seg_start = (block_end - blocks) * BM
    dest = seg_start[meta[:, 0:2]] + meta[:, 2:4]
    n_used = block_end[-1:]
    bidx = jnp.minimum(jnp.arange(N_BLOCKS, dtype=jnp.int32), n_used - 1)
    block_expert = jnp.sum(bidx[:, None] >= block_end[None, :], axis=1).astype(jnp.int32)
    n_used = n_used.astype(jnp.int32)
    y2 = _gmm(x, dest, g, block_expert, n_used, (block_end - 1).astype(jnp.int32), blocks.astype(jnp.int32),
              w_gate, w_up, w_down)
    return _combine(x, wts, y2, g_final, final)


def kernel(x_prompt, x_sample, state_conv_b, state_conv_c, norm_mix, norm_ffn, norm_final, a_w_in, a_ln_g, a_ln_b, a_w_s, a_b_s, a_w_out, b_w_in, b_conv, b_w_out, c_w_pw1, c_b_pw1, c_dw, c_b_dw, c_ln_g, c_ln_b, c_w_pw2, c_b_pw2, f_w_gate, f_w_up, f_w_down, m_router, m_w_gate, m_w_up, m_w_down):
    last_tile = jnp.arange(N_SEQ) * TILES_PER_SEQ + TILES_PER_SEQ - 1

    x, v0 = _gmlp_mixer(x_prompt.reshape(N_PROMPT_TOK, D), x_sample.reshape(N_SAMPLE_TOK, D), 0,
                        norm_mix[0], a_w_in[0], a_ln_g[0], a_ln_b[0], a_w_s[0], a_b_s[0], a_w_out[0])
    x = _dense_ffn(x, norm_ffn[0], f_w_gate[0], f_w_up[0], f_w_down[0])
    x, tb_p, tb_s = _sconv_mixer(x, norm_mix[1], b_w_in[0], b_conv[0], b_w_out[0], state_conv_b[0])
    x = _moe_ffn(x, norm_ffn[1], m_router[0], m_w_gate[0], m_w_up[0], m_w_down[0], norm_final, False)
    x, tc_p, tc_s = _conf_mixer(x, norm_mix[2], c_w_pw1[0], c_b_pw1[0], c_dw[0], c_b_dw[0], c_ln_g[0], c_ln_b[0],
                                c_w_pw2[0], c_b_pw2[0], state_conv_c[0])
    x = _dense_ffn(x, norm_ffn[2], f_w_gate[1], f_w_up[1], f_w_down[1])
    x, v1 = _gmlp_mixer(x, x, N_PROMPT_TILES,
                        norm_mix[3], a_w_in[1], a_ln_g[1], a_ln_b[1], a_w_s[1], a_b_s[1], a_w_out[1])
    y_prompt, y_sample = _moe_ffn(x, norm_ffn[3], m_router[1], m_w_gate[1], m_w_up[1], m_w_down[1], norm_final, True)

    y_prompt = y_prompt.reshape(N_SEQ, SEQ, D)
    y_sample = y_sample.reshape(N_SEQ, SAMPLE_SEQ, D)
    nb = B_WIDTH - 1
    nc = C_WIDTH - 1
    new_b_p = tb_p[last_tile, B_HALO - nb:, :][None]
    new_b_s = tb_s[:, B_HALO - nb:, :][None]
    new_c_p = tc_p[last_tile, C_HALO - nc:, :][None]
    new_c_s = tc_s[:, C_HALO - nc:, :][None]
    new_v = jnp.stack([v0, v1]).reshape(2, N_SEQ, SAMPLE_SEQ, A_HALF)
    return (y_prompt, y_sample, new_b_p, new_b_s, new_c_p, new_c_s, new_v)
```

```python
import functools

import jax
import jax.numpy as jnp
from jax import lax
from jax.experimental import pallas as pl
from jax.experimental.pallas import tpu as pltpu

F32 = jnp.float32
BF16 = jnp.bfloat16

D = 1024
TM = 512
SEQ = 4096
SAMPLE_SEQ = 64
N_SEQ = 8
TILES_PER_SEQ = SEQ // TM
N_PROMPT_TOK = N_SEQ * SEQ
N_SAMPLE_TOK = N_SEQ * SAMPLE_SEQ
N_TOK = N_PROMPT_TOK + N_SAMPLE_TOK
N_PROMPT_TILES = N_PROMPT_TOK // TM
N_TILES = N_TOK // TM
assert N_SAMPLE_TOK == TM and N_TILES == N_PROMPT_TILES + 1

A_HALF = 2 * D
A_GROUPS = 8
A_HEAD = A_HALF // A_GROUPS
A_CHUNK = 128
B_WIDTH = 3
C_WIDTH = 31
B_HALO = 8
C_HALO = 32
D_FF = 2816
N_EXPERTS = 8
LANES = 128
SUBLANES = 8
MXU_COLS = 256
RMS_EPS = 1e-6
LN_EPS = 1e-5

BM = TM
N_ROWS = 2 * N_TOK + N_EXPERTS * BM
N_BLOCKS = N_ROWS // BM
FF_CHUNKS = ((0, 1024), (1024, 1024), (2048, 768))

VMEM_LIMIT = 56 * 1024 * 1024


def _const_spec(shape):
    return pl.BlockSpec(shape, lambda *_: (0,) * len(shape), pipeline_mode=pl.Buffered(1))


def _rmsnorm(x, g):
    return x * lax.rsqrt(jnp.mean(x * x, axis=-1, keepdims=True) + RMS_EPS) * g


def _layernorm(x, g, b):
    mu = jnp.mean(x, axis=-1, keepdims=True)
    xc = x - mu
    var = jnp.mean(xc * xc, axis=-1, keepdims=True)
    return xc * lax.rsqrt(var + LN_EPS) * g + b


def _dot(a, b):
    return jnp.dot(a, b, preferred_element_type=F32)


def _gmlp_kernel(xp_ref, xs_ref, g_ref, win_ref, lng_ref, lnb_ref, ws_ref, bs_ref, wout_ref,
                 o_ref, v_ref, vn_s, y_s):
    is_sample = pl.program_id(0) >= N_PROMPT_TILES
    x = jnp.where(is_sample, xs_ref[...], xp_ref[...])
    h = _rmsnorm(x, g_ref[...]).astype(BF16)
    v = jax.nn.gelu(_dot(h, win_ref[:, A_HALF:]))
    vn = _layernorm(v, lng_ref[...], lnb_ref[...])
    v_ref[...] = vn
    vn_s[...] = vn.astype(BF16)
    r = lax.broadcasted_iota(jnp.int32, (A_CHUNK, A_CHUNK), 0)
    c = lax.broadcasted_iota(jnp.int32, (A_CHUNK, A_CHUNK), 1)
    seg_shift = jnp.where(is_sample, 6, 7)
    mask = (r >= c) & ((r >> seg_shift) == (c >> seg_shift))
    for g in range(A_GROUPS):
        cols = slice(g * A_HEAD, (g + 1) * A_HEAD)
        u_g = jax.nn.gelu(_dot(h, win_ref[:, cols]))
        ws = jnp.where(mask, ws_ref[0, g], 0.0).astype(BF16)
        bias = jnp.concatenate([bs_ref[0, g]] * (A_HEAD // LANES), axis=1)
        for ch in range(TM // A_CHUNK):
            rows = slice(ch * A_CHUNK, (ch + 1) * A_CHUNK)
            s = _dot(ws, vn_s[rows, cols]) + bias
            y_s[rows, cols] = (u_g[rows] * s).astype(BF16)
    o_ref[...] = x + _dot(y_s[...], wout_ref[...])


def _gmlp_mixer(x_prompt, x_sample, sample_block, g, w_in, ln_g, ln_b, w_s, b_s, w_out):
    ws2 = jnp.stack([w_s, jnp.tile(w_s[:, :SAMPLE_SEQ, :SAMPLE_SEQ], (1, 2, 2))])
    b2 = jnp.stack([b_s, jnp.tile(b_s[:, :SAMPLE_SEQ], (1, 2))])
    b2 = jnp.broadcast_to(b2[..., None], (2, A_GROUPS, A_CHUNK, LANES))
    return pl.pallas_call(
        _gmlp_kernel,
        grid=(N_TILES,),
        in_specs=[
            pl.BlockSpec((TM, D), lambda i: (jnp.minimum(i, N_PROMPT_TILES - 1), 0)),
            pl.BlockSpec((TM, D), lambda i: (sample_block, 0)),
            _const_spec((1, D)),
            _const_spec((D, 2 * A_HALF)),
            _const_spec((1, A_HALF)),
            _const_spec((1, A_HALF)),
            pl.BlockSpec((1, A_GROUPS, A_CHUNK, A_CHUNK), lambda i: (i // N_PROMPT_TILES, 0, 0, 0)),
            pl.BlockSpec((1, A_GROUPS, A_CHUNK, LANES), lambda i: (i // N_PROMPT_TILES, 0, 0, 0)),
            _const_spec((A_HALF, D)),
        ],
        out_specs=[
            pl.BlockSpec((TM, D), lambda i: (i, 0)),
            pl.BlockSpec((TM, A_HALF), lambda i: (0, 0)),
        ],
        out_shape=[
            jax.ShapeDtypeStruct((N_TOK, D), F32),
            jax.ShapeDtypeStruct((N_SAMPLE_TOK, A_HALF), F32),
        ],
        scratch_shapes=[pltpu.VMEM((TM, A_HALF), BF16), pltpu.VMEM((TM, A_HALF), BF16)],
        compiler_params=pltpu.CompilerParams(dimension_semantics=("arbitrary",), vmem_limit_bytes=VMEM_LIMIT),
        name="gmlp_mixer",
    )(x_prompt, x_sample, g.reshape(1, D), w_in.astype(BF16), ln_g.reshape(1, A_HALF), ln_b.reshape(1, A_HALF),
      ws2, b2, w_out.astype(BF16))


def _sconv_kernel(x_ref, g_ref, win_ref, cw_ref, wout_ref, st_ref,
                  o_ref, tailp_ref, tails_ref, full_s, y_s):
    i = pl.program_id(0)
    is_sample = i >= N_PROMPT_TILES
    x = x_ref[...]
    h = _rmsnorm(x, g_ref[...]).astype(BF16)
    p = _dot(h, win_ref[...])
    bg = p[:, :D]
    u = p[:, D:2 * D] * p[:, 2 * D:]
    w = [cw_ref[k:k + 1, :] for k in range(B_WIDTH)]

    def conv(n):
        return sum(w[k] * full_s[pl.ds(B_HALO - (B_WIDTH - 1) + k, n), :] for k in range(B_WIDTH))

    @pl.when(jnp.logical_not(is_sample))
    def _():
        @pl.when(i % TILES_PER_SEQ == 0)
        def _():
            full_s[0:B_HALO, :] = jnp.zeros((B_HALO, D), F32)

        full_s[B_HALO:, :] = u
        y_s[...] = (bg * conv(TM)).astype(BF16)
        tail = full_s[TM:TM + B_HALO, :]
        tailp_ref[0] = tail
        full_s[0:B_HALO, :] = tail

    @pl.when(is_sample)
    def _():
        for b in range(N_SEQ):
            rows = slice(b * SAMPLE_SEQ, (b + 1) * SAMPLE_SEQ)
            full_s[0:B_HALO, :] = st_ref[b]
            full_s[B_HALO:B_HALO + SAMPLE_SEQ, :] = u[rows]
            y_s[rows, :] = (bg[rows] * conv(SAMPLE_SEQ)).astype(BF16)
            tails_ref[b] = full_s[SAMPLE_SEQ:SAMPLE_SEQ + B_HALO, :]
        tailp_ref[0] = jnp.zeros((B_HALO, D), F32)

    o_ref[...] = x + _dot(y_s[...], wout_ref[...])


def _sconv_mixer(x, g, w_in, w_conv, w_out, state):
    st = jnp.pad(state, ((0, 0), (B_HALO - (B_WIDTH - 1), 0), (0, 0)))
    return pl.pallas_call(
        _sconv_kernel,
        grid=(N_TILES,),
        in_specs=[
            pl.BlockSpec((TM, D), lambda i: (i, 0)),
            _const_spec((1, D)),
            _const_spec((D, 3 * D)),
            _const_spec((B_WIDTH, D)),
            _const_spec((D, D)),
            _const_spec((N_SEQ, B_HALO, D)),
        ],
        out_specs=[
            pl.BlockSpec((TM, D), lambda i: (i, 0)),
            pl.BlockSpec((1, B_HALO, D), lambda i: (i, 0, 0)),
            pl.BlockSpec((N_SEQ, B_HALO, D), lambda i: (0, 0, 0)),
        ],
        out_shape=[
            jax.ShapeDtypeStruct((N_TOK, D), F32),
            jax.ShapeDtypeStruct((N_TILES, B_HALO, D), F32),
            jax.ShapeDtypeStruct((N_SEQ, B_HALO, D), F32),
        ],
        scratch_shapes=[pltpu.VMEM((B_HALO + TM, D), F32), pltpu.VMEM((TM, D), BF16)],
        compiler_params=pltpu.CompilerParams(dimension_semantics=("arbitrary",), vmem_limit_bytes=VMEM_LIMIT),
        name="sconv_mixer",
    )(x, g.reshape(1, D), w_in.astype(BF16), w_conv, w_out.astype(BF16), st)


C_ROW_BLOCK = 16


def _conf_kernel(x_ref, g_ref, w1_ref, b1_ref, dw_ref, bdw_ref, lng_ref, lnb_ref, w2_ref, b2_ref, st_ref,
                 o_ref, tailp_ref, tails_ref, full_s, shift_s, conv_s, y_s):
    i = pl.program_id(0)
    is_sample = i >= N_PROMPT_TILES
    x = x_ref[...]
    h = _rmsnorm(x, g_ref[...]).astype(BF16)
    p = _dot(h, w1_ref[...]) + b1_ref[...]
    gl = p[:, :D] * jax.nn.sigmoid(p[:, D:])

    def conv(n, out_row0):
        n_in = C_HALO + n
        for r in range(1, SUBLANES):
            shift_s[r - 1, 0:n_in - SUBLANES, :] = full_s[r:r + n_in - SUBLANES, :]

        def block(rb, carry):
            base = pl.multiple_of(rb * C_ROW_BLOCK, C_ROW_BLOCK)
            acc = [jnp.zeros((SUBLANES, D), F32) for _ in range(C_ROW_BLOCK // SUBLANES)]
            for k in range(C_WIDTH):
                off = k + (C_HALO - (C_WIDTH - 1))
                row0 = base + (off // SUBLANES) * SUBLANES
                if off % SUBLANES == 0:
                    tap = full_s[pl.ds(row0, C_ROW_BLOCK), :]
                else:
                    tap = shift_s[off % SUBLANES - 1, pl.ds(row0, C_ROW_BLOCK), :]
                wk = dw_ref[k]
                for a in range(len(acc)):
                    acc[a] = acc[a] + wk * tap[a * SUBLANES:(a + 1) * SUBLANES]
            conv_s[pl.ds(out_row0 + base, C_ROW_BLOCK), :] = jnp.concatenate(acc, axis=0)
            return carry

        lax.fori_loop(0, n // C_ROW_BLOCK, block, 0)

    @pl.when(jnp.logical_not(is_sample))
    def _():
        @pl.when(i % TILES_PER_SEQ == 0)
        def _():
            full_s[0:C_HALO, :] = jnp.zeros((C_HALO, D), F32)

        full_s[C_HALO:, :] = gl
        conv(TM, 0)
        tail = full_s[TM:TM + C_HALO, :]
        tailp_ref[0] = tail
        full_s[0:C_HALO, :] = tail

    @pl.when(is_sample)
    def _():
        for b in range(N_SEQ):
            full_s[0:C_HALO, :] = st_ref[b]
            full_s[C_HALO:C_HALO + SAMPLE_SEQ, :] = gl[b * SAMPLE_SEQ:(b + 1) * SAMPLE_SEQ]
            conv(SAMPLE_SEQ, b * SAMPLE_SEQ)
            tails_ref[b] = full_s[SAMPLE_SEQ:SAMPLE_SEQ + C_HALO, :]
        tailp_ref[0] = jnp.zeros((C_HALO, D), F32)

    y = _layernorm(conv_s[...] + bdw_ref[...], lng_ref[...], lnb_ref[...])
    y_s[...] = (y * jax.nn.sigmoid(y)).astype(BF16)
    o_ref[...] = x + _dot(y_s[...], w2_ref[...]) + b2_ref[...]


def _conf_mixer(x, g, w_pw1, b_pw1, w_dw, b_dw, ln_g, ln_b, w_pw2, b_pw2, state):
    st = jnp.pad(state, ((0, 0), (C_HALO - (C_WIDTH - 1), 0), (0, 0)))
    dw = jnp.broadcast_to(w_dw[:, None, :], (C_WIDTH, SUBLANES, D))
    return pl.pallas_call(
        _conf_kernel,
        grid=(N_TILES,),
        in_specs=[
            pl.BlockSpec((TM, D), lambda i: (i, 0)),
            _const_spec((1, D)),
            _const_spec((D, 2 * D)),
            _const_spec((1, 2 * D)),
            _const_spec((C_WIDTH, SUBLANES, D)),
            _const_spec((1, D)),
            _const_spec((1, D)),
            _const_spec((1, D)),
            _const_spec((D, D)),
            _const_spec((1, D)),
            _const_spec((N_SEQ, C_HALO, D)),
        ],
        out_specs=[
            pl.BlockSpec((TM, D), lambda i: (i, 0)),
            pl.BlockSpec((1, C_HALO, D), lambda i: (i, 0, 0)),
            pl.BlockSpec((N_SEQ, C_HALO, D), lambda i: (0, 0, 0)),
        ],
        out_shape=[
            jax.ShapeDtypeStruct((N_TOK, D), F32),
            jax.ShapeDtypeStruct((N_TILES, C_HALO, D), F32),
            jax.ShapeDtypeStruct((N_SEQ, C_HALO, D), F32),
        ],
        scratch_shapes=[
            pltpu.VMEM((C_HALO + TM, D), F32),
            pltpu.VMEM((SUBLANES - 1, C_HALO + TM, D), F32),
            pltpu.VMEM((TM, D), F32),
            pltpu.VMEM((TM, D), BF16),
        ],
        compiler_params=pltpu.CompilerParams(dimension_semantics=("arbitrary",), vmem_limit_bytes=VMEM_LIMIT),
        name="conf_mixer",
    )(x, g.reshape(1, D), w_pw1.astype(BF16), b_pw1.reshape(1, 2 * D), dw, b_dw.reshape(1, D),
      ln_g.reshape(1, D), ln_b.reshape(1, D), w_pw2.astype(BF16), b_pw2.reshape(1, D), st)


def _swiglu_hidden(h, wg_ref, wu_ref, hid_s, lead=()):
    for c0, cn in FF_CHUNKS:
        a = _dot(h, wg_ref[lead + (slice(None), slice(c0, c0 + cn))])
        b = _dot(h, wu_ref[lead + (slice(None), slice(c0, c0 + cn))])
        hid_s[:, c0:c0 + cn] = (a * jax.nn.sigmoid(a) * b).astype(BF16)


def _ffn_kernel(x_ref, g_ref, wg_ref, wu_ref, wd_ref, o_ref, hid_s):
    x = x_ref[...]
    h = _rmsnorm(x, g_ref[...]).astype(BF16)
    _swiglu_hidden(h, wg_ref, wu_ref, hid_s)
    o_ref[...] = x + _dot(hid_s[...], wd_ref[...])


def _dense_ffn(x, g, w_gate, w_up, w_down):
    return pl.pallas_call(
        _ffn_kernel,
        grid=(N_TILES,),
        in_specs=[
            pl.BlockSpec((TM, D), lambda i: (i, 0)),
            _const_spec((1, D)),
            _const_spec((D, D_FF)),
            _const_spec((D, D_FF)),
            _const_spec((D_FF, D)),
        ],
        out_specs=pl.BlockSpec((TM, D), lambda i: (i, 0)),
        out_shape=jax.ShapeDtypeStruct((N_TOK, D), F32),
        scratch_shapes=[pltpu.VMEM((TM, D_FF), BF16)],
        compiler_params=pltpu.CompilerParams(dimension_semantics=("arbitrary",), vmem_limit_bytes=VMEM_LIMIT),
        name="dense_ffn",
    )(x, g.reshape(1, D), w_gate.astype(BF16), w_up.astype(BF16), w_down.astype(BF16))


def _router_kernel(x_ref, g_ref, wr_ref, meta_ref, wts_ref, cnt_ref, run_s):
    i = pl.program_id(0)

    @pl.when(i == 0)
    def _():
        run_s[...] = jnp.zeros((1, LANES), F32)

    h = _rmsnorm(x_ref[...], g_ref[...])
    logits = _dot(h.astype(BF16), wr_ref[...].astype(BF16))
    lane = lax.broadcasted_iota(jnp.int32, (TM, LANES), 1)
    lane_f = lane.astype(F32)
    neg = jnp.float32(-jnp.inf)
    logits = jnp.where(lane < N_EXPERTS, logits, neg)
    l1 = jnp.max(logits, axis=-1, keepdims=True)
    e1 = jnp.min(jnp.where(logits == l1, lane_f, float(LANES)), axis=-1, keepdims=True).astype(jnp.int32)
    rest = jnp.where(lane == e1, neg, logits)
    l2 = jnp.max(rest, axis=-1, keepdims=True)
    e2 = jnp.min(jnp.where(rest == l2, lane_f, float(LANES)), axis=-1, keepdims=True).astype(jnp.int32)
    t = jnp.exp(l2 - l1)
    w1 = 1.0 / (1.0 + t)
    w2 = t * w1
    sel = jnp.logical_or(lane == e1, lane == e2)
    onehot = jnp.where(sel, 1.0, 0.0)
    r = lax.broadcasted_iota(jnp.int32, (TM, TM), 0)
    c = lax.broadcasted_iota(jnp.int32, (TM, TM), 1)
    below = jnp.where(r > c, 1.0, 0.0).astype(BF16)
    before = _dot(below, onehot.astype(BF16)) + run_s[...]
    r1 = jnp.sum(jnp.where(lane == e1, before, 0.0), axis=-1, keepdims=True).astype(jnp.int32)
    r2 = jnp.sum(jnp.where(lane == e2, before, 0.0), axis=-1, keepdims=True).astype(jnp.int32)
    run_s[...] = run_s[...] + jnp.sum(onehot, axis=0, keepdims=True)
    meta = jnp.where(lane == 0, e1, jnp.where(lane == 1, e2, jnp.where(lane == 2, r1, jnp.where(lane == 3, r2, 0))))
    meta_ref[...] = meta
    wts_ref[...] = jnp.where(lane == 0, w1, jnp.where(lane == 1, w2, 0.0))
    cnt_ref[...] = run_s[...].astype(jnp.int32)


def _router(x, g, w_router):
    wr = jnp.pad(w_router, ((0, 0), (0, LANES - N_EXPERTS)))
    return pl.pallas_call(
        _router_kernel,
        grid=(N_TILES,),
        in_specs=[
            pl.BlockSpec((TM, D), lambda i: (i, 0)),
            _const_spec((1, D)),
            _const_spec((D, LANES)),
        ],
        out_specs=[
            pl.BlockSpec((TM, LANES), lambda i: (i, 0)),
            pl.BlockSpec((TM, LANES), lambda i: (i, 0)),
            pl.BlockSpec((1, LANES), lambda i: (0, 0)),
        ],
        out_shape=[
            jax.ShapeDtypeStruct((N_TOK, LANES), jnp.int32),
            jax.ShapeDtypeStruct((N_TOK, LANES), F32),
            jax.ShapeDtypeStruct((1, LANES), jnp.int32),
        ],
        scratch_shapes=[pltpu.VMEM((1, LANES), F32)],
        compiler_params=pltpu.CompilerParams(dimension_semantics=("arbitrary",), vmem_limit_bytes=VMEM_LIMIT),
        name="moe_router",
    )(x, g.reshape(1, D), wr)


def _row_copy(src, src_row, dst, dst_row, sem):
    return pltpu.make_async_copy(src.at[pl.ds(src_row, 1), :], dst.at[pl.ds(dst_row, 1), :], sem)


DMA_UNROLL = 8


def _dispatch_kernel(last_ref, nblk_ref, nu_ref, dest_ref, x_ref, xs_hbm, zero_s, sem, zsem):
    def zero_block(b):
        return pltpu.make_async_copy(zero_s, xs_hbm.at[pl.ds(b * BM, BM), :], zsem)

    @pl.when(pl.program_id(0) == 0)
    def _():
        zero_s[...] = jnp.zeros((BM, D), F32)
        for start in (True, False):
            for e in range(N_EXPERTS):
                tail = N_BLOCKS - 1 - e
                for cond, b in ((nblk_ref[e] > 0, last_ref[e]), (tail >= nu_ref[0], tail)):
                    cp = zero_block(b)
                    pl.when(cond)(cp.start if start else cp.wait)

    def issue(j, carry):
        for u in range(DMA_UNROLL):
            t = j * DMA_UNROLL + u
            _row_copy(x_ref, t, xs_hbm, dest_ref[0, 0, 2 * t], sem).start()
            _row_copy(x_ref, t, xs_hbm, dest_ref[0, 0, 2 * t + 1], sem).start()
        return carry

    lax.fori_loop(0, TM // DMA_UNROLL, issue, 0)

    def drain(j, carry):
        for _ in range(2 * DMA_UNROLL):
            _row_copy(x_ref, 0, xs_hbm, 0, sem).wait()
        return carry

    lax.fori_loop(0, TM // DMA_UNROLL, drain, 0)


def _dispatch(x, dest, last_block, n_blocks, n_used):
    return pl.pallas_call(
        _dispatch_kernel,
        grid_spec=pltpu.PrefetchScalarGridSpec(
            num_scalar_prefetch=3,
            grid=(N_TILES,),
            in_specs=[
                pl.BlockSpec((1, 1, 2 * TM), lambda i, *_: (i, 0, 0), memory_space=pltpu.SMEM),
                pl.BlockSpec((TM, D), lambda i, *_: (i, 0)),
            ],
            out_specs=pl.BlockSpec(memory_space=pl.ANY),
            scratch_shapes=[pltpu.VMEM((BM, D), F32), pltpu.SemaphoreType.DMA(()), pltpu.SemaphoreType.DMA(())],
        ),
        out_shape=jax.ShapeDtypeStruct((N_ROWS, D), F32),
        compiler_params=pltpu.CompilerParams(dimension_semantics=("arbitrary",), vmem_limit_bytes=VMEM_LIMIT),
        name="moe_dispatch",
    )(last_block, n_blocks, n_used, dest.reshape(N_TILES, 1, 2 * TM), x)


def _gmm_kernel(be_ref, nu_ref, xs_ref, g_ref, wg_ref, wu_ref, wd_ref, ys_ref, hid_s):
    used = pl.program_id(0) < nu_ref[0]

    @pl.when(used)
    def _():
        h = _rmsnorm(xs_ref[...], g_ref[...]).astype(BF16)
        _swiglu_hidden(h, wg_ref, wu_ref, hid_s, lead=(0,))
        ys_ref[...] = _dot(hid_s[...], wd_ref[0])

    @pl.when(jnp.logical_not(used))
    def _():
        ys_ref[...] = jnp.zeros((BM, D), F32)


def _gmm(xs, g, block_expert, n_used, w_gate, w_up, w_down):
    row_map = lambda i, be, nu: (jnp.minimum(i, nu[0] - 1), 0)
    wspec = lambda shape: pl.BlockSpec((1,) + shape, lambda i, be, nu: (be[i], 0, 0))
    return pl.pallas_call(
        _gmm_kernel,
        grid_spec=pltpu.PrefetchScalarGridSpec(
            num_scalar_prefetch=2,
            grid=(N_BLOCKS,),
            in_specs=[
                pl.BlockSpec((BM, D), row_map),
                pl.BlockSpec((1, D), lambda i, be, nu: (0, 0)),
                wspec((D, D_FF)),
                wspec((D, D_FF)),
                wspec((D_FF, D)),
            ],
            out_specs=pl.BlockSpec((BM, D), lambda i, be, nu: (i, 0)),
            scratch_shapes=[pltpu.VMEM((BM, D_FF), BF16)],
        ),
        out_shape=jax.ShapeDtypeStruct((N_ROWS, D), F32),
        compiler_params=pltpu.CompilerParams(dimension_semantics=("arbitrary",), vmem_limit_bytes=VMEM_LIMIT),
        name="moe_gmm",
    )(block_expert, n_used, xs, g.reshape(1, D), w_gate.astype(BF16), w_up.astype(BF16), w_down.astype(BF16))


def _combine_kernel(dest_ref, x_ref, wts_ref, gf_ref, ys_hbm, *rest, final):
    out_refs, (buf, sem) = rest[:-2], rest[-2:]

    def issue(j, carry):
        for u in range(DMA_UNROLL):
            t = j * DMA_UNROLL + u
            _row_copy(ys_hbm, dest_ref[0, 0, 2 * t], buf.at[0], t, sem).start()
            _row_copy(ys_hbm, dest_ref[0, 0, 2 * t + 1], buf.at[1], t, sem).start()
        return carry

    lax.fori_loop(0, TM // DMA_UNROLL, issue, 0)

    def drain(j, carry):
        for _ in range(DMA_UNROLL):
            _row_copy(ys_hbm, 0, buf.at[0], 0, sem).wait()
            _row_copy(ys_hbm, 0, buf.at[1], 0, sem).wait()
        return carry

    lax.fori_loop(0, TM // DMA_UNROLL, drain, 0)
    w = wts_ref[...]
    y = x_ref[...] + w[:, 0:1] * buf[0] + w[:, 1:2] * buf[1]
    if final:
        y = _rmsnorm(y, gf_ref[...])
        yp_ref, ysm_ref = out_refs
        is_sample = pl.program_id(0) >= N_PROMPT_TILES

        @pl.when(jnp.logical_not(is_sample))
        def _():
            yp_ref[...] = y

        @pl.when(is_sample)
        def _():
            ysm_ref[...] = y
    else:
        out_refs[0][...] = y


def _combine(x, dest, wts, ys, g_final, final):
    if final:
        out_specs = [pl.BlockSpec((TM, D), lambda i: (jnp.minimum(i, N_PROMPT_TILES - 1), 0)),
                     pl.BlockSpec((TM, D), lambda i: (0, 0))]
        out_shape = [jax.ShapeDtypeStruct((N_PROMPT_TOK, D), F32), jax.ShapeDtypeStruct((N_SAMPLE_TOK, D), F32)]
    else:
        out_specs = [pl.BlockSpec((TM, D), lambda i: (i, 0))]
        out_shape = [jax.ShapeDtypeStruct((N_TOK, D), F32)]
    outs = pl.pallas_call(
        functools.partial(_combine_kernel, final=final),
        grid=(N_TILES,),
        in_specs=[
            pl.BlockSpec((1, 1, 2 * TM), lambda i: (i, 0, 0), memory_space=pltpu.SMEM),
            pl.BlockSpec((TM, D), lambda i: (i, 0)),
            pl.BlockSpec((TM, LANES), lambda i: (i, 0)),
            _const_spec((1, D)),
            pl.BlockSpec(memory_space=pl.ANY),
        ],
        out_specs=out_specs,
        out_shape=out_shape,
        scratch_shapes=[pltpu.VMEM((2, TM, D), F32), pltpu.SemaphoreType.DMA(())],
        compiler_params=pltpu.CompilerParams(dimension_semantics=("arbitrary",), vmem_limit_bytes=VMEM_LIMIT),
        name="moe_combine",
    )(dest.reshape(N_TILES, 1, 2 * TM), x, wts, g_final.reshape(1, D), ys)
    return outs if final else outs[0]


def _moe_ffn(x, g, w_router, w_gate, w_up, w_down, g_final, final):
    meta, wts, counts = _router(x, g, w_router)
    counts = counts[0, :N_EXPERTS]
    blocks = (counts + BM - 1) // BM
    block_end = jnp.cumsum(blocks)
    seg_start = (block_end - blocks) * BM
    dest = seg_start[meta[:, 0:2]] + meta[:, 2:4]
    n_used = block_end[-1:]
    bidx = jnp.minimum(jnp.arange(N_BLOCKS, dtype=jnp.int32), n_used - 1)
    block_expert = jnp.sum(bidx[:, None] >= block_end[None, :], axis=1).astype(jnp.int32)
    n_used = n_used.astype(jnp.int32)
    xs = _dispatch(x, dest, (block_end - 1).astype(jnp.int32), blocks.astype(jnp.int32), n_used)
    ys = _gmm(xs, g, block_expert, n_used, w_gate, w_up, w_down)
    return _combine(x, dest, wts, ys, g_final, final)


def kernel(x_prompt, x_sample, state_conv_b, state_conv_c, norm_mix, norm_ffn, norm_final, a_w_in, a_ln_g, a_ln_b, a_w_s, a_b_s, a_w_out, b_w_in, b_conv, b_w_out, c_w_pw1, c_b_pw1, c_dw, c_b_dw, c_ln_g, c_ln_b, c_w_pw2, c_b_pw2, f_w_gate, f_w_up, f_w_down, m_router, m_w_gate, m_w_up, m_w_down):
    last_tile = jnp.arange(N_SEQ) * TILES_PER_SEQ + TILES_PER_SEQ - 1

    x, v0 = _gmlp_mixer(x_prompt.reshape(N_PROMPT_TOK, D), x_sample.reshape(N_SAMPLE_TOK, D), 0,
                        norm_mix[0], a_w_in[0], a_ln_g[0], a_ln_b[0], a_w_s[0], a_b_s[0], a_w_out[0])
    x = _dense_ffn(x, norm_ffn[0], f_w_gate[0], f_w_up[0], f_w_down[0])
    x, tb_p, tb_s = _sconv_mixer(x, norm_mix[1], b_w_in[0], b_conv[0], b_w_out[0], state_conv_b[0])
    x = _moe_ffn(x, norm_ffn[1], m_router[0], m_w_gate[0], m_w_up[0], m_w_down[0], norm_final, False)
    x, tc_p, tc_s = _conf_mixer(x, norm_mix[2], c_w_pw1[0], c_b_pw1[0], c_dw[0], c_b_dw[0], c_ln_g[0], c_ln_b[0],
                                c_w_pw2[0], c_b_pw2[0], state_conv_c[0])
    x = _dense_ffn(x, norm_ffn[2], f_w_gate[1], f_w_up[1], f_w_down[1])
    x, v1 = _gmlp_mixer(x, x, N_PROMPT_TILES,
                        norm_mix[3], a_w_in[1], a_ln_g[1], a_ln_b[1], a_w_s[1], a_b_s[1], a_w_out[1])
    y_prompt, y_sample = _moe_ffn(x, norm_ffn[3], m_router[1], m_w_gate[1], m_w_up[1], m_w_down[1], norm_final, True)

    y_prompt = y_prompt.reshape(N_SEQ, SEQ, D)
    y_sample = y_sample.reshape(N_SEQ, SAMPLE_SEQ, D)
    nb = B_WIDTH - 1
    nc = C_WIDTH - 1
    new_b_p = tb_p[last_tile, B_HALO - nb:, :][None]
    new_b_s = tb_s[:, B_HALO - nb:, :][None]
    new_c_p = tc_p[last_tile, C_HALO - nc:, :][None]
    new_c_s = tc_s[:, C_HALO - nc:, :][None]
    new_v = jnp.stack([v0, v1]).reshape(2, N_SEQ, SAMPLE_SEQ, A_HALF)
    return (y_prompt, y_sample, new_b_p, new_b_s, new_c_p, new_c_s, new_v)
```

```python
import functools

import jax
import jax.numpy as jnp
from jax import lax
from jax.experimental import pallas as pl
from jax.experimental.pallas import tpu as pltpu

F32 = jnp.float32
BF16 = jnp.bfloat16

D = 1024
TM = 512
SEQ = 4096
SAMPLE_SEQ = 64
N_SEQ = 8
TILES_PER_SEQ = SEQ // TM
N_PROMPT_TOK = N_SEQ * SEQ
N_SAMPLE_TOK = N_SEQ * SAMPLE_SEQ
N_TOK = N_PROMPT_TOK + N_SAMPLE_TOK
N_PROMPT_TILES = N_PROMPT_TOK // TM
N_TILES = N_TOK // TM
assert N_SAMPLE_TOK == TM and N_TILES == N_PROMPT_TILES + 1

A_HALF = 2 * D
A_GROUPS = 8
A_HEAD = A_HALF // A_GROUPS
A_CHUNK = 128
B_WIDTH = 3
C_WIDTH = 31
B_HALO = 8
C_HALO = 32
D_FF = 2816
N_EXPERTS = 8
LANES = 128
SUBLANES = 8
MXU_COLS = 256
RMS_EPS = 1e-6
LN_EPS = 1e-5

BM = TM
N_ROWS = 2 * N_TOK + N_EXPERTS * BM
N_BLOCKS = N_ROWS // BM
FF_CHUNKS = ((0, 1024), (1024, 1024), (2048, 768))

VMEM_LIMIT = 56 * 1024 * 1024


def _const_spec(shape):
    return pl.BlockSpec(shape, lambda *_: (0,) * len(shape), pipeline_mode=pl.Buffered(1))


def _rmsnorm(x, g):
    return x * lax.rsqrt(jnp.mean(x * x, axis=-1, keepdims=True) + RMS_EPS) * g


def _layernorm(x, g, b):
    mu = jnp.mean(x, axis=-1, keepdims=True)
    xc = x - mu
    var = jnp.mean(xc * xc, axis=-1, keepdims=True)
    return xc * lax.rsqrt(var + LN_EPS) * g + b


def _dot(a, b):
    return jnp.dot(a, b, preferred_element_type=F32)


def _gmlp_kernel(xp_ref, xs_ref, g_ref, win_ref, lng_ref, lnb_ref, ws_ref, bs_ref, wout_ref,
                 o_ref, v_ref, vn_s, y_s):
    is_sample = pl.program_id(0) >= N_PROMPT_TILES
    x = jnp.where(is_sample, xs_ref[...], xp_ref[...])
    h = _rmsnorm(x, g_ref[...]).astype(BF16)
    v = jax.nn.gelu(_dot(h, win_ref[:, A_HALF:]))
    vn = _layernorm(v, lng_ref[...], lnb_ref[...])
    v_ref[...] = vn
    vn_s[...] = vn.astype(BF16)
    r = lax.broadcasted_iota(jnp.int32, (A_CHUNK, A_CHUNK), 0)
    c = lax.broadcasted_iota(jnp.int32, (A_CHUNK, A_CHUNK), 1)
    seg_shift = jnp.where(is_sample, 6, 7)
    mask = (r >= c) & ((r >> seg_shift) == (c >> seg_shift))
    for g in range(A_GROUPS):
        cols = slice(g * A_HEAD, (g + 1) * A_HEAD)
        u_g = jax.nn.gelu(_dot(h, win_ref[:, cols]))
        ws = jnp.where(mask, ws_ref[0, g], 0.0).astype(BF16)
        bias = jnp.concatenate([bs_ref[0, g]] * (A_HEAD // LANES), axis=1)
        for ch in range(TM // A_CHUNK):
            rows = slice(ch * A_CHUNK, (ch + 1) * A_CHUNK)
            s = _dot(ws, vn_s[rows, cols]) + bias
            y_s[rows, cols] = (u_g[rows] * s).astype(BF16)
    o_ref[...] = x + _dot(y_s[...], wout_ref[...])


def _gmlp_mixer(x_prompt, x_sample, sample_block, g, w_in, ln_g, ln_b, w_s, b_s, w_out):
    ws2 = jnp.stack([w_s, jnp.tile(w_s[:, :SAMPLE_SEQ, :SAMPLE_SEQ], (1, 2, 2))])
    b2 = jnp.stack([b_s, jnp.tile(b_s[:, :SAMPLE_SEQ], (1, 2))])
    b2 = jnp.broadcast_to(b2[..., None], (2, A_GROUPS, A_CHUNK, LANES))
    return pl.pallas_call(
        _gmlp_kernel,
        grid=(N_TILES,),
        in_specs=[
            pl.BlockSpec((TM, D), lambda i: (jnp.minimum(i, N_PROMPT_TILES - 1), 0)),
            pl.BlockSpec((TM, D), lambda i: (sample_block, 0)),
            _const_spec((1, D)),
            _const_spec((D, 2 * A_HALF)),
            _const_spec((1, A_HALF)),
            _const_spec((1, A_HALF)),
            pl.BlockSpec((1, A_GROUPS, A_CHUNK, A_CHUNK), lambda i: (i // N_PROMPT_TILES, 0, 0, 0)),
            pl.BlockSpec((1, A_GROUPS, A_CHUNK, LANES), lambda i: (i // N_PROMPT_TILES, 0, 0, 0)),
            _const_spec((A_HALF, D)),
        ],
        out_specs=[
            pl.BlockSpec((TM, D), lambda i: (i, 0)),
            pl.BlockSpec((TM, A_HALF), lambda i: (0, 0)),
        ],
        out_shape=[
            jax.ShapeDtypeStruct((N_TOK, D), F32),
            jax.ShapeDtypeStruct((N_SAMPLE_TOK, A_HALF), F32),
        ],
        scratch_shapes=[pltpu.VMEM((TM, A_HALF), BF16), pltpu.VMEM((TM, A_HALF), BF16)],
        compiler_params=pltpu.CompilerParams(dimension_semantics=("arbitrary",), vmem_limit_bytes=VMEM_LIMIT),
        name="gmlp_mixer",
    )(x_prompt, x_sample, g.reshape(1, D), w_in.astype(BF16), ln_g.reshape(1, A_HALF), ln_b.reshape(1, A_HALF),
      ws2, b2, w_out.astype(BF16))


def _sconv_kernel(x_ref, g_ref, win_ref, cw_ref, wout_ref, st_ref,
                  o_ref, tailp_ref, tails_ref, full_s, y_s):
    i = pl.program_id(0)
    is_sample = i >= N_PROMPT_TILES
    x = x_ref[...]
    h = _rmsnorm(x, g_ref[...]).astype(BF16)
    p = _dot(h, win_ref[...])
    bg = p[:, :D]
    u = p[:, D:2 * D] * p[:, 2 * D:]
    w = [cw_ref[k:k + 1, :] for k in range(B_WIDTH)]

    def conv(n):
        return sum(w[k] * full_s[pl.ds(B_HALO - (B_WIDTH - 1) + k, n), :] for k in range(B_WIDTH))

    @pl.when(jnp.logical_not(is_sample))
    def _():
        @pl.when(i % TILES_PER_SEQ == 0)
        def _():
            full_s[0:B_HALO, :] = jnp.zeros((B_HALO, D), F32)

        full_s[B_HALO:, :] = u
        y_s[...] = (bg * conv(TM)).astype(BF16)
        tail = full_s[TM:TM + B_HALO, :]
        tailp_ref[0] = tail
        full_s[0:B_HALO, :] = tail

    @pl.when(is_sample)
    def _():
        for b in range(N_SEQ):
            rows = slice(b * SAMPLE_SEQ, (b + 1) * SAMPLE_SEQ)
            full_s[0:B_HALO, :] = st_ref[b]
            full_s[B_HALO:B_HALO + SAMPLE_SEQ, :] = u[rows]
            y_s[rows, :] = (bg[rows] * conv(SAMPLE_SEQ)).astype(BF16)
            tails_ref[b] = full_s[SAMPLE_SEQ:SAMPLE_SEQ + B_HALO, :]
        tailp_ref[0] = jnp.zeros((B_HALO, D), F32)

    o_ref[...] = x + _dot(y_s[...], wout_ref[...])


def _sconv_mixer(x, g, w_in, w_conv, w_out, state):
    st = jnp.pad(state, ((0, 0), (B_HALO - (B_WIDTH - 1), 0), (0, 0)))
    return pl.pallas_call(
        _sconv_kernel,
        grid=(N_TILES,),
        in_specs=[
            pl.BlockSpec((TM, D), lambda i: (i, 0)),
            _const_spec((1, D)),
            _const_spec((D, 3 * D)),
            _const_spec((B_WIDTH, D)),
            _const_spec((D, D)),
            _const_spec((N_SEQ, B_HALO, D)),
        ],
        out_specs=[
            pl.BlockSpec((TM, D), lambda i: (i, 0)),
            pl.BlockSpec((1, B_HALO, D), lambda i: (i, 0, 0)),
            pl.BlockSpec((N_SEQ, B_HALO, D), lambda i: (0, 0, 0)),
        ],
        out_shape=[
            jax.ShapeDtypeStruct((N_TOK, D), F32),
            jax.ShapeDtypeStruct((N_TILES, B_HALO, D), F32),
            jax.ShapeDtypeStruct((N_SEQ, B_HALO, D), F32),
        ],
        scratch_shapes=[pltpu.VMEM((B_HALO + TM, D), F32), pltpu.VMEM((TM, D), BF16)],
        compiler_params=pltpu.CompilerParams(dimension_semantics=("arbitrary",), vmem_limit_bytes=VMEM_LIMIT),
        name="sconv_mixer",
    )(x, g.reshape(1, D), w_in.astype(BF16), w_conv, w_out.astype(BF16), st)


C_ROW_BLOCK = 32


def _conf_kernel(x_ref, g_ref, w1_ref, b1_ref, dw_ref, bdw_ref, lng_ref, lnb_ref, w2_ref, b2_ref, st_ref,
                 o_ref, tailp_ref, tails_ref, full_s, shift_s, conv_s, y_s):
    i = pl.program_id(0)
    is_sample = i >= N_PROMPT_TILES
    x = x_ref[...]
    h = _rmsnorm(x, g_ref[...]).astype(BF16)
    p = _dot(h, w1_ref[...]) + b1_ref[...]
    gl = p[:, :D] * jax.nn.sigmoid(p[:, D:])

    def conv(n, out_row0):
        n_in = C_HALO + n
        for r in range(1, SUBLANES):
            shift_s[r - 1, 0:n_in - SUBLANES, :] = full_s[r:r + n_in - SUBLANES, :]

        def block(rb, carry):
            base = pl.multiple_of(rb * C_ROW_BLOCK, C_ROW_BLOCK)
            acc = [jnp.zeros((SUBLANES, D), F32) for _ in range(C_ROW_BLOCK // SUBLANES)]
            for k in range(C_WIDTH):
                off = k + (C_HALO - (C_WIDTH - 1))
                row0 = base + (off // SUBLANES) * SUBLANES
                wk = dw_ref[k]
                for a in range(len(acc)):
                    rows = pl.ds(row0 + a * SUBLANES, SUBLANES)
                    tap = full_s[rows, :] if off % SUBLANES == 0 else shift_s[off % SUBLANES - 1, rows, :]
                    acc[a] = acc[a] + wk * tap
            conv_s[pl.ds(out_row0 + base, C_ROW_BLOCK), :] = jnp.concatenate(acc, axis=0)
            return carry

        lax.fori_loop(0, n // C_ROW_BLOCK, block, 0)

    @pl.when(jnp.logical_not(is_sample))
    def _():
        @pl.when(i % TILES_PER_SEQ == 0)
        def _():
            full_s[0:C_HALO, :] = jnp.zeros((C_HALO, D), F32)

        full_s[C_HALO:, :] = gl
        conv(TM, 0)
        tail = full_s[TM:TM + C_HALO, :]
        tailp_ref[0] = tail
        full_s[0:C_HALO, :] = tail

    @pl.when(is_sample)
    def _():
        for b in range(N_SEQ):
            full_s[0:C_HALO, :] = st_ref[b]
            full_s[C_HALO:C_HALO + SAMPLE_SEQ, :] = gl[b * SAMPLE_SEQ:(b + 1) * SAMPLE_SEQ]
            conv(SAMPLE_SEQ, b * SAMPLE_SEQ)
            tails_ref[b] = full_s[SAMPLE_SEQ:SAMPLE_SEQ + C_HALO, :]
        tailp_ref[0] = jnp.zeros((C_HALO, D), F32)

    y = _layernorm(conv_s[...] + bdw_ref[...], lng_ref[...], lnb_ref[...])
    y_s[...] = (y * jax.nn.sigmoid(y)).astype(BF16)
    o_ref[...] = x + _dot(y_s[...], w2_ref[...]) + b2_ref[...]


def _conf_mixer(x, g, w_pw1, b_pw1, w_dw, b_dw, ln_g, ln_b, w_pw2, b_pw2, state):
    st = jnp.pad(state, ((0, 0), (C_HALO - (C_WIDTH - 1), 0), (0, 0)))
    dw = jnp.broadcast_to(w_dw[:, None, :], (C_WIDTH, SUBLANES, D))
    return pl.pallas_call(
        _conf_kernel,
        grid=(N_TILES,),
        in_specs=[
            pl.BlockSpec((TM, D), lambda i: (i, 0)),
            _const_spec((1, D)),
            _const_spec((D, 2 * D)),
            _const_spec((1, 2 * D)),
            _const_spec((C_WIDTH, SUBLANES, D)),
            _const_spec((1, D)),
            _const_spec((1, D)),
            _const_spec((1, D)),
            _const_spec((D, D)),
            _const_spec((1, D)),
            _const_spec((N_SEQ, C_HALO, D)),
        ],
        out_specs=[
            pl.BlockSpec((TM, D), lambda i: (i, 0)),
            pl.BlockSpec((1, C_HALO, D), lambda i: (i, 0, 0)),
            pl.BlockSpec((N_SEQ, C_HALO, D), lambda i: (0, 0, 0)),
        ],
        out_shape=[
            jax.ShapeDtypeStruct((N_TOK, D), F32),
            jax.ShapeDtypeStruct((N_TILES, C_HALO, D), F32),
            jax.ShapeDtypeStruct((N_SEQ, C_HALO, D), F32),
        ],
        scratch_shapes=[
            pltpu.VMEM((C_HALO + TM, D), F32),
            pltpu.VMEM((SUBLANES - 1, C_HALO + TM, D), F32),
            pltpu.VMEM((TM, D), F32),
            pltpu.VMEM((TM, D), BF16),
        ],
        compiler_params=pltpu.CompilerParams(dimension_semantics=("arbitrary",), vmem_limit_bytes=VMEM_LIMIT),
        name="conf_mixer",
    )(x, g.reshape(1, D), w_pw1.astype(BF16), b_pw1.reshape(1, 2 * D), dw, b_dw.reshape(1, D),
      ln_g.reshape(1, D), ln_b.reshape(1, D), w_pw2.astype(BF16), b_pw2.reshape(1, D), st)


def _swiglu_hidden(h, wg_ref, wu_ref, hid_s, lead=()):
    for c0, cn in FF_CHUNKS:
        a = _dot(h, wg_ref[lead + (slice(None), slice(c0, c0 + cn))])
        b = _dot(h, wu_ref[lead + (slice(None), slice(c0, c0 + cn))])
        hid_s[:, c0:c0 + cn] = (a * jax.nn.sigmoid(a) * b).astype(BF16)


def _ffn_kernel(x_ref, g_ref, wg_ref, wu_ref, wd_ref, o_ref, hid_s):
    x = x_ref[...]
    h = _rmsnorm(x, g_ref[...]).astype(BF16)
    _swiglu_hidden(h, wg_ref, wu_ref, hid_s, lead=(0,))
    o_ref[...] = x + _dot(hid_s[...], wd_ref[0])


def _dense_ffn(x, g, layer, w_gate, w_up, w_down):
    wspec = lambda shape: pl.BlockSpec((1,) + shape, lambda i: (layer, 0, 0), pipeline_mode=pl.Buffered(1))
    return pl.pallas_call(
        _ffn_kernel,
        grid=(N_TILES,),
        in_specs=[
            pl.BlockSpec((TM, D), lambda i: (i, 0)),
            _const_spec((1, D)),
            wspec((D, D_FF)),
            wspec((D, D_FF)),
            wspec((D_FF, D)),
        ],
        out_specs=pl.BlockSpec((TM, D), lambda i: (i, 0)),
        out_shape=jax.ShapeDtypeStruct((N_TOK, D), F32),
        scratch_shapes=[pltpu.VMEM((TM, D_FF), BF16)],
        compiler_params=pltpu.CompilerParams(dimension_semantics=("arbitrary",), vmem_limit_bytes=VMEM_LIMIT),
        name="dense_ffn",
    )(x, g.reshape(1, D), w_gate, w_up, w_down)


def _router_kernel(x_ref, g_ref, wr_ref, meta_ref, wts_ref, cnt_ref, run_s):
    i = pl.program_id(0)

    @pl.when(i == 0)
    def _():
        run_s[...] = jnp.zeros((1, LANES), F32)

    h = _rmsnorm(x_ref[...], g_ref[...])
    logits = _dot(h.astype(BF16), wr_ref[...].astype(BF16))
    lane = lax.broadcasted_iota(jnp.int32, (TM, LANES), 1)
    lane_f = lane.astype(F32)
    neg = jnp.float32(-jnp.inf)
    logits = jnp.where(lane < N_EXPERTS, logits, neg)
    l1 = jnp.max(logits, axis=-1, keepdims=True)
    e1 = jnp.min(jnp.where(logits == l1, lane_f, float(LANES)), axis=-1, keepdims=True).astype(jnp.int32)
    rest = jnp.where(lane == e1, neg, logits)
    l2 = jnp.max(rest, axis=-1, keepdims=True)
    e2 = jnp.min(jnp.where(rest == l2, lane_f, float(LANES)), axis=-1, keepdims=True).astype(jnp.int32)
    t = jnp.exp(l2 - l1)
    w1 = 1.0 / (1.0 + t)
    w2 = t * w1
    sel = jnp.logical_or(lane == e1, lane == e2)
    onehot = jnp.where(sel, 1.0, 0.0)
    r = lax.broadcasted_iota(jnp.int32, (TM, TM), 0)
    c = lax.broadcasted_iota(jnp.int32, (TM, TM), 1)
    below = jnp.where(r > c, 1.0, 0.0).astype(BF16)
    before = _dot(below, onehot.astype(BF16)) + run_s[...]
    r1 = jnp.sum(jnp.where(lane == e1, before, 0.0), axis=-1, keepdims=True).astype(jnp.int32)
    r2 = jnp.sum(jnp.where(lane == e2, before, 0.0), axis=-1, keepdims=True).astype(jnp.int32)
    run_s[...] = run_s[...] + jnp.sum(onehot, axis=0, keepdims=True)
    meta = jnp.where(lane == 0, e1, jnp.where(lane == 1, e2, jnp.where(lane == 2, r1, jnp.where(lane == 3, r2, 0))))
    meta_ref[...] = meta
    wts_ref[...] = jnp.where(lane == 0, w1, jnp.where(lane == 1, w2, 0.0))
    cnt_ref[...] = run_s[...].astype(jnp.int32)


def _router(x, g, w_router):
    wr = jnp.pad(w_router, ((0, 0), (0, LANES - N_EXPERTS)))
    return pl.pallas_call(
        _router_kernel,
        grid=(N_TILES,),
        in_specs=[
            pl.BlockSpec((TM, D), lambda i: (i, 0)),
            _const_spec((1, D)),
            _const_spec((D, LANES)),
        ],
        out_specs=[
            pl.BlockSpec((TM, LANES), lambda i: (i, 0)),
            pl.BlockSpec((TM, LANES), lambda i: (i, 0)),
            pl.BlockSpec((1, LANES), lambda i: (0, 0)),
        ],
        out_shape=[
            jax.ShapeDtypeStruct((N_TOK, LANES), jnp.int32),
            jax.ShapeDtypeStruct((N_TOK, LANES), F32),
            jax.ShapeDtypeStruct((1, LANES), jnp.int32),
        ],
        scratch_shapes=[pltpu.VMEM((1, LANES), F32)],
        compiler_params=pltpu.CompilerParams(dimension_semantics=("arbitrary",), vmem_limit_bytes=VMEM_LIMIT),
        name="moe_router",
    )(x, g.reshape(1, D), wr)


def _row_copy(src, src_row, dst, dst_row, sem):
    return pltpu.make_async_copy(src.at[pl.ds(src_row, 1), :], dst.at[pl.ds(dst_row, 1), :], sem)


DMA_UNROLL = 8


def _dispatch_kernel(last_ref, nblk_ref, nu_ref, dest_ref, x_ref, xs_hbm, zero_s, sem, zsem):
    def zero_block(b):
        return pltpu.make_async_copy(zero_s, xs_hbm.at[pl.ds(b * BM, BM), :], zsem)

    @pl.when(pl.program_id(0) == 0)
    def _():
        zero_s[...] = jnp.zeros((BM, D), F32)
        for start in (True, False):
            for e in range(N_EXPERTS):
                tail = N_BLOCKS - 1 - e
                for cond, b in ((nblk_ref[e] > 0, last_ref[e]), (tail >= nu_ref[0], tail)):
                    cp = zero_block(b)
                    pl.when(cond)(cp.start if start else cp.wait)

    def issue(j, carry):
        for u in range(DMA_UNROLL):
            t = j * DMA_UNROLL + u
            _row_copy(x_ref, t, xs_hbm, dest_ref[0, 0, 2 * t], sem).start()
            _row_copy(x_ref, t, xs_hbm, dest_ref[0, 0, 2 * t + 1], sem).start()
        return carry

    lax.fori_loop(0, TM // DMA_UNROLL, issue, 0)

    def drain(j, carry):
        for _ in range(2 * DMA_UNROLL):
            _row_copy(x_ref, 0, xs_hbm, 0, sem).wait()
        return carry

    lax.fori_loop(0, TM // DMA_UNROLL, drain, 0)


def _dispatch(x, dest, last_block, n_blocks, n_used):
    return pl.pallas_call(
        _dispatch_kernel,
        grid_spec=pltpu.PrefetchScalarGridSpec(
            num_scalar_prefetch=3,
            grid=(N_TILES,),
            in_specs=[
                pl.BlockSpec((1, 1, 2 * TM), lambda i, *_: (i, 0, 0), memory_space=pltpu.SMEM),
                pl.BlockSpec((TM, D), lambda i, *_: (i, 0)),
            ],
            out_specs=pl.BlockSpec(memory_space=pl.ANY),
            scratch_shapes=[pltpu.VMEM((BM, D), F32), pltpu.SemaphoreType.DMA(()), pltpu.SemaphoreType.DMA(())],
        ),
        out_shape=jax.ShapeDtypeStruct((N_ROWS, D), F32),
        compiler_params=pltpu.CompilerParams(dimension_semantics=("arbitrary",), vmem_limit_bytes=VMEM_LIMIT),
        name="moe_dispatch",
    )(last_block, n_blocks, n_used, dest.reshape(N_TILES, 1, 2 * TM), x)


def _gmm_kernel(be_ref, nu_ref, xs_ref, g_ref, wg_ref, wu_ref, wd_ref, ys_ref, hid_s):
    used = pl.program_id(0) < nu_ref[0]

    @pl.when(used)
    def _():
        h = _rmsnorm(xs_ref[...], g_ref[...]).astype(BF16)
        _swiglu_hidden(h, wg_ref, wu_ref, hid_s, lead=(0, 0))
        ys_ref[...] = _dot(hid_s[...], wd_ref[0, 0])

    @pl.when(jnp.logical_not(used))
    def _():
        ys_ref[...] = jnp.zeros((BM, D), F32)


def _gmm(xs, g, block_expert, n_used, layer, w_gate, w_up, w_down):
    row_map = lambda i, be, nu: (jnp.minimum(i, nu[0] - 1), 0)
    wspec = lambda shape: pl.BlockSpec((1, 1) + shape, lambda i, be, nu: (layer, be[i], 0, 0))
    return pl.pallas_call(
        _gmm_kernel,
        grid_spec=pltpu.PrefetchScalarGridSpec(
            num_scalar_prefetch=2,
            grid=(N_BLOCKS,),
            in_specs=[
                pl.BlockSpec((BM, D), row_map),
                pl.BlockSpec((1, D), lambda i, be, nu: (0, 0)),
                wspec((D, D_FF)),
                wspec((D, D_FF)),
                wspec((D_FF, D)),
            ],
            out_specs=pl.BlockSpec((BM, D), lambda i, be, nu: (i, 0)),
            scratch_shapes=[pltpu.VMEM((BM, D_FF), BF16)],
        ),
        out_shape=jax.ShapeDtypeStruct((N_ROWS, D), F32),
        compiler_params=pltpu.CompilerParams(dimension_semantics=("arbitrary",), vmem_limit_bytes=VMEM_LIMIT),
        name="moe_gmm",
    )(block_expert, n_used, xs, g.reshape(1, D), w_gate, w_up, w_down)


def _combine_kernel(dest_ref, x_ref, wts_ref, gf_ref, ys_hbm, *rest, final):
    out_refs, (buf, sem) = rest[:-2], rest[-2:]

    def issue(j, carry):
        for u in range(DMA_UNROLL):
            t = j * DMA_UNROLL + u
            _row_copy(ys_hbm, dest_ref[0, 0, 2 * t], buf.at[0], t, sem).start()
            _row_copy(ys_hbm, dest_ref[0, 0, 2 * t + 1], buf.at[1], t, sem).start()
        return carry

    lax.fori_loop(0, TM // DMA_UNROLL, issue, 0)

    def drain(j, carry):
        for _ in range(DMA_UNROLL):
            _row_copy(ys_hbm, 0, buf.at[0], 0, sem).wait()
            _row_copy(ys_hbm, 0, buf.at[1], 0, sem).wait()
        return carry

    lax.fori_loop(0, TM // DMA_UNROLL, drain, 0)
    w = wts_ref[...]
    y = x_ref[...] + w[:, 0:1] * buf[0] + w[:, 1:2] * buf[1]
    if final:
        y = _rmsnorm(y, gf_ref[...])
        yp_ref, ysm_ref = out_refs
        is_sample = pl.program_id(0) >= N_PROMPT_TILES

        @pl.when(jnp.logical_not(is_sample))
        def _():
            yp_ref[...] = y

        @pl.when(is_sample)
        def _():
            ysm_ref[...] = y
    else:
        out_refs[0][...] = y


def _combine(x, dest, wts, ys, g_final, final):
    if final:
        out_specs = [pl.BlockSpec((TM, D), lambda i: (jnp.minimum(i, N_PROMPT_TILES - 1), 0)),
                     pl.BlockSpec((TM, D), lambda i: (0, 0))]
        out_shape = [jax.ShapeDtypeStruct((N_PROMPT_TOK, D), F32), jax.ShapeDtypeStruct((N_SAMPLE_TOK, D), F32)]
    else:
        out_specs = [pl.BlockSpec((TM, D), lambda i: (i, 0))]
        out_shape = [jax.ShapeDtypeStruct((N_TOK, D), F32)]
    outs = pl.pallas_call(
        functools.partial(_combine_kernel, final=final),
        grid=(N_TILES,),
        in_specs=[
            pl.BlockSpec((1, 1, 2 * TM), lambda i: (i, 0, 0), memory_space=pltpu.SMEM),
            pl.BlockSpec((TM, D), lambda i: (i, 0)),
            pl.BlockSpec((TM, LANES), lambda i: (i, 0)),
            _const_spec((1, D)),
            pl.BlockSpec(memory_space=pl.ANY),
        ],
        out_specs=out_specs,
        out_shape=out_shape,
        scratch_shapes=[pltpu.VMEM((2, TM, D), F32), pltpu.SemaphoreType.DMA(())],
        compiler_params=pltpu.CompilerParams(dimension_semantics=("arbitrary",), vmem_limit_bytes=VMEM_LIMIT),
        name="moe_combine",
    )(dest.reshape(N_TILES, 1, 2 * TM), x, wts, g_final.reshape(1, D), ys)
    return outs if final else outs[0]


def _moe_ffn(x, g, w_router, layer, w_gate, w_up, w_down, g_final, final):
    meta, wts, counts = _router(x, g, w_router)
    counts = counts[0, :N_EXPERTS]
    blocks = (counts + BM - 1) // BM
    block_end = jnp.cumsum(blocks)
    seg_start = (block_end - blocks) * BM
    dest = seg_start[meta[:, 0:2]] + meta[:, 2:4]
    n_used = block_end[-1:]
    bidx = jnp.minimum(jnp.arange(N_BLOCKS, dtype=jnp.int32), n_used - 1)
    block_expert = jnp.sum(bidx[:, None] >= block_end[None, :], axis=1).astype(jnp.int32)
    n_used = n_used.astype(jnp.int32)
    xs = _dispatch(x, dest, (block_end - 1).astype(jnp.int32), blocks.astype(jnp.int32), n_used)
    ys = _gmm(xs, g, block_expert, n_used, layer, w_gate, w_up, w_down)
    return _combine(x, dest, wts, ys, g_final, final)


def kernel(x_prompt, x_sample, state_conv_b, state_conv_c, norm_mix, norm_ffn, norm_final, a_w_in, a_ln_g, a_ln_b, a_w_s, a_b_s, a_w_out, b_w_in, b_conv, b_w_out, c_w_pw1, c_b_pw1, c_dw, c_b_dw, c_ln_g, c_ln_b, c_w_pw2, c_b_pw2, f_w_gate, f_w_up, f_w_down, m_router, m_w_gate, m_w_up, m_w_down):
    last_tile = jnp.arange(N_SEQ) * TILES_PER_SEQ + TILES_PER_SEQ - 1
    fw = [w.astype(BF16) for w in (f_w_gate, f_w_up, f_w_down)]
    mw = [w.astype(BF16) for w in (m_w_gate, m_w_up, m_w_down)]

    x, v0 = _gmlp_mixer(x_prompt.reshape(N_PROMPT_TOK, D), x_sample.reshape(N_SAMPLE_TOK, D), 0,
                        norm_mix[0], a_w_in[0], a_ln_g[0], a_ln_b[0], a_w_s[0], a_b_s[0], a_w_out[0])
    x = _dense_ffn(x, norm_ffn[0], 0, *fw)
    x, tb_p, tb_s = _sconv_mixer(x, norm_mix[1], b_w_in[0], b_conv[0], b_w_out[0], state_conv_b[0])
    x = _moe_ffn(x, norm_ffn[1], m_router[0], 0, *mw, norm_final, False)
    x, tc_p, tc_s = _conf_mixer(x, norm_mix[2], c_w_pw1[0], c_b_pw1[0], c_dw[0], c_b_dw[0], c_ln_g[0], c_ln_b[0],
                                c_w_pw2[0], c_b_pw2[0], state_conv_c[0])
    x = _dense_ffn(x, norm_ffn[2], 1, *fw)
    x, v1 = _gmlp_mixer(x, x, N_PROMPT_TILES,
                        norm_mix[3], a_w_in[1], a_ln_g[1], a_ln_b[1], a_w_s[1], a_b_s[1], a_w_out[1])
    y_prompt, y_sample = _moe_ffn(x, norm_ffn[3], m_router[1], 1, *mw, norm_final, True)

    y_prompt = y_prompt.reshape(N_SEQ, SEQ, D)
    y_sample = y_sample.reshape(N_SEQ, SAMPLE_SEQ, D)
    nb = B_WIDTH - 1
    nc = C_WIDTH - 1
    new_b_p = tb_p[last_tile, B_HALO - nb:, :][None]
    new_b_s = tb_s[:, B_HALO - nb:, :][None]
    new_c_p = tc_p[last_tile, C_HALO - nc:, :][None]
    new_c_s = tc_s[:, C_HALO - nc:, :][None]
    new_v = jnp.stack([v0, v1]).reshape(2, N_SEQ, SAMPLE_SEQ, A_HALF)
    return (y_prompt, y_sample, new_b_p, new_b_s, new_c_p, new_c_s, new_v)
```

```python
import functools

import jax
import jax.numpy as jnp
from jax import lax
from jax.experimental import pallas as pl
from jax.experimental.pallas import tpu as pltpu

F32 = jnp.float32
BF16 = jnp.bfloat16

D = 1024
TM = 512
SEQ = 4096
SAMPLE_SEQ = 64
N_SEQ = 8
TILES_PER_SEQ = SEQ // TM
N_PROMPT_TOK = N_SEQ * SEQ
N_SAMPLE_TOK = N_SEQ * SAMPLE_SEQ
N_TOK = N_PROMPT_TOK + N_SAMPLE_TOK
N_PROMPT_TILES = N_PROMPT_TOK // TM
N_TILES = N_TOK // TM
assert N_SAMPLE_TOK == TM and N_TILES == N_PROMPT_TILES + 1

A_HALF = 2 * D
A_GROUPS = 8
A_HEAD = A_HALF // A_GROUPS
A_CHUNK = 128
B_WIDTH = 3
C_WIDTH = 31
B_HALO = 8
C_HALO = 32
D_FF = 2816
N_EXPERTS = 8
LANES = 128
SUBLANES = 8
MXU_COLS = 256
RMS_EPS = 1e-6
LN_EPS = 1e-5

BM = TM
N_ROWS = 2 * N_TOK + N_EXPERTS * BM
N_BLOCKS = N_ROWS // BM
FF_CHUNKS = ((0, 1024), (1024, 1024), (2048, 768))

VMEM_LIMIT = 56 * 1024 * 1024


def _const_spec(shape):
    return pl.BlockSpec(shape, lambda *_: (0,) * len(shape), pipeline_mode=pl.Buffered(1))


def _rmsnorm(x, g):
    return x * lax.rsqrt(jnp.mean(x * x, axis=-1, keepdims=True) + RMS_EPS) * g


def _layernorm(x, g, b):
    mu = jnp.mean(x, axis=-1, keepdims=True)
    xc = x - mu
    var = jnp.mean(xc * xc, axis=-1, keepdims=True)
    return xc * lax.rsqrt(var + LN_EPS) * g + b


def _dot(a, b):
    return jnp.dot(a, b, preferred_element_type=F32)


def _gmlp_kernel(xp_ref, xs_ref, g_ref, win_ref, lng_ref, lnb_ref, ws_ref, bs_ref, wout_ref,
                 o_ref, v_ref, vn_s, y_s):
    is_sample = pl.program_id(0) >= N_PROMPT_TILES
    x = jnp.where(is_sample, xs_ref[...], xp_ref[...])
    h = _rmsnorm(x, g_ref[...]).astype(BF16)
    v = jax.nn.gelu(_dot(h, win_ref[:, A_HALF:]))
    vn = _layernorm(v, lng_ref[...], lnb_ref[...])
    v_ref[...] = vn
    vn_s[...] = vn.astype(BF16)
    r = lax.broadcasted_iota(jnp.int32, (A_CHUNK, A_CHUNK), 0)
    c = lax.broadcasted_iota(jnp.int32, (A_CHUNK, A_CHUNK), 1)
    seg_shift = jnp.where(is_sample, 6, 7)
    mask = (r >= c) & ((r >> seg_shift) == (c >> seg_shift))
    for g in range(A_GROUPS):
        cols = slice(g * A_HEAD, (g + 1) * A_HEAD)
        u_g = jax.nn.gelu(_dot(h, win_ref[:, cols]))
        ws = jnp.where(mask, ws_ref[0, g], 0.0).astype(BF16)
        bias = jnp.concatenate([bs_ref[0, g]] * (A_HEAD // LANES), axis=1)
        for ch in range(TM // A_CHUNK):
            rows = slice(ch * A_CHUNK, (ch + 1) * A_CHUNK)
            s = _dot(ws, vn_s[rows, cols]) + bias
            y_s[rows, cols] = (u_g[rows] * s).astype(BF16)
    o_ref[...] = x + _dot(y_s[...], wout_ref[...])


def _gmlp_mixer(x_prompt, x_sample, sample_block, g, w_in, ln_g, ln_b, w_s, b_s, w_out):
    ws2 = jnp.stack([w_s, jnp.tile(w_s[:, :SAMPLE_SEQ, :SAMPLE_SEQ], (1, 2, 2))])
    b2 = jnp.stack([b_s, jnp.tile(b_s[:, :SAMPLE_SEQ], (1, 2))])
    b2 = jnp.broadcast_to(b2[..., None], (2, A_GROUPS, A_CHUNK, LANES))
    return pl.pallas_call(
        _gmlp_kernel,
        grid=(N_TILES,),
        in_specs=[
            pl.BlockSpec((TM, D), lambda i: (jnp.minimum(i, N_PROMPT_TILES - 1), 0)),
            pl.BlockSpec((TM, D), lambda i: (sample_block, 0)),
            _const_spec((1, D)),
            _const_spec((D, 2 * A_HALF)),
            _const_spec((1, A_HALF)),
            _const_spec((1, A_HALF)),
            pl.BlockSpec((1, A_GROUPS, A_CHUNK, A_CHUNK), lambda i: (i // N_PROMPT_TILES, 0, 0, 0)),
            pl.BlockSpec((1, A_GROUPS, A_CHUNK, LANES), lambda i: (i // N_PROMPT_TILES, 0, 0, 0)),
            _const_spec((A_HALF, D)),
        ],
        out_specs=[
            pl.BlockSpec((TM, D), lambda i: (i, 0)),
            pl.BlockSpec((TM, A_HALF), lambda i: (0, 0)),
        ],
        out_shape=[
            jax.ShapeDtypeStruct((N_TOK, D), F32),
            jax.ShapeDtypeStruct((N_SAMPLE_TOK, A_HALF), F32),
        ],
        scratch_shapes=[pltpu.VMEM((TM, A_HALF), BF16), pltpu.VMEM((TM, A_HALF), BF16)],
        compiler_params=pltpu.CompilerParams(dimension_semantics=("arbitrary",), vmem_limit_bytes=VMEM_LIMIT),
        name="gmlp_mixer",
    )(x_prompt, x_sample, g.reshape(1, D), w_in.astype(BF16), ln_g.reshape(1, A_HALF), ln_b.reshape(1, A_HALF),
      ws2, b2, w_out.astype(BF16))


def _sconv_kernel(x_ref, g_ref, win_ref, cw_ref, wout_ref, st_ref,
                  o_ref, tailp_ref, tails_ref, full_s, y_s):
    i = pl.program_id(0)
    is_sample = i >= N_PROMPT_TILES
    x = x_ref[...]
    h = _rmsnorm(x, g_ref[...]).astype(BF16)
    p = _dot(h, win_ref[...])
    bg = p[:, :D]
    u = p[:, D:2 * D] * p[:, 2 * D:]
    w = [cw_ref[k:k + 1, :] for k in range(B_WIDTH)]

    def conv(n):
        return sum(w[k] * full_s[pl.ds(B_HALO - (B_WIDTH - 1) + k, n), :] for k in range(B_WIDTH))

    @pl.when(jnp.logical_not(is_sample))
    def _():
        @pl.when(i % TILES_PER_SEQ == 0)
        def _():
            full_s[0:B_HALO, :] = jnp.zeros((B_HALO, D), F32)

        full_s[B_HALO:, :] = u
        y_s[...] = (bg * conv(TM)).astype(BF16)
        tail = full_s[TM:TM + B_HALO, :]
        tailp_ref[0] = tail
        full_s[0:B_HALO, :] = tail

    @pl.when(is_sample)
    def _():
        for b in range(N_SEQ):
            rows = slice(b * SAMPLE_SEQ, (b + 1) * SAMPLE_SEQ)
            full_s[0:B_HALO, :] = st_ref[b]
            full_s[B_HALO:B_HALO + SAMPLE_SEQ, :] = u[rows]
            y_s[rows, :] = (bg[rows] * conv(SAMPLE_SEQ)).astype(BF16)
            tails_ref[b] = full_s[SAMPLE_SEQ:SAMPLE_SEQ + B_HALO, :]
        tailp_ref[0] = jnp.zeros((B_HALO, D), F32)

    o_ref[...] = x + _dot(y_s[...], wout_ref[...])


def _sconv_mixer(x, g, w_in, w_conv, w_out, state):
    st = jnp.pad(state, ((0, 0), (B_HALO - (B_WIDTH - 1), 0), (0, 0)))
    return pl.pallas_call(
        _sconv_kernel,
        grid=(N_TILES,),
        in_specs=[
            pl.BlockSpec((TM, D), lambda i: (i, 0)),
            _const_spec((1, D)),
            _const_spec((D, 3 * D)),
            _const_spec((B_WIDTH, D)),
            _const_spec((D, D)),
            _const_spec((N_SEQ, B_HALO, D)),
        ],
        out_specs=[
            pl.BlockSpec((TM, D), lambda i: (i, 0)),
            pl.BlockSpec((1, B_HALO, D), lambda i: (i, 0, 0)),
            pl.BlockSpec((N_SEQ, B_HALO, D), lambda i: (0, 0, 0)),
        ],
        out_shape=[
            jax.ShapeDtypeStruct((N_TOK, D), F32),
            jax.ShapeDtypeStruct((N_TILES, B_HALO, D), F32),
            jax.ShapeDtypeStruct((N_SEQ, B_HALO, D), F32),
        ],
        scratch_shapes=[pltpu.VMEM((B_HALO + TM, D), F32), pltpu.VMEM((TM, D), BF16)],
        compiler_params=pltpu.CompilerParams(dimension_semantics=("arbitrary",), vmem_limit_bytes=VMEM_LIMIT),
        name="sconv_mixer",
    )(x, g.reshape(1, D), w_in.astype(BF16), w_conv, w_out.astype(BF16), st)


C_ROW_BLOCK = 32


def _conf_kernel(x_ref, g_ref, w1_ref, b1_ref, dw_ref, bdw_ref, lng_ref, lnb_ref, w2_ref, b2_ref, st_ref,
                 o_ref, tailp_ref, tails_ref, full_s, shift_s, conv_s, y_s):
    i = pl.program_id(0)
    is_sample = i >= N_PROMPT_TILES
    x = x_ref[...]
    h = _rmsnorm(x, g_ref[...]).astype(BF16)
    p = _dot(h, w1_ref[...]) + b1_ref[...]
    gl = p[:, :D] * jax.nn.sigmoid(p[:, D:])

    def conv(n, out_row0):
        n_in = C_HALO + n
        for r in range(1, SUBLANES):
            shift_s[r - 1, 0:n_in - SUBLANES, :] = full_s[r:r + n_in - SUBLANES, :]

        def block(rb, carry):
            base = pl.multiple_of(rb * C_ROW_BLOCK, C_ROW_BLOCK)
            acc = [jnp.zeros((SUBLANES, D), F32) for _ in range(C_ROW_BLOCK // SUBLANES)]
            for k in range(C_WIDTH):
                off = k + (C_HALO - (C_WIDTH - 1))
                row0 = base + (off // SUBLANES) * SUBLANES
                wk = dw_ref[k]
                for a in range(len(acc)):
                    rows = pl.ds(row0 + a * SUBLANES, SUBLANES)
                    tap = full_s[rows, :] if off % SUBLANES == 0 else shift_s[off % SUBLANES - 1, rows, :]
                    acc[a] = acc[a] + wk * tap
            conv_s[pl.ds(out_row0 + base, C_ROW_BLOCK), :] = jnp.concatenate(acc, axis=0)
            return carry

        lax.fori_loop(0, n // C_ROW_BLOCK, block, 0)

    @pl.when(jnp.logical_not(is_sample))
    def _():
        @pl.when(i % TILES_PER_SEQ == 0)
        def _():
            full_s[0:C_HALO, :] = jnp.zeros((C_HALO, D), F32)

        full_s[C_HALO:, :] = gl
        conv(TM, 0)
        tail = full_s[TM:TM + C_HALO, :]
        tailp_ref[0] = tail
        full_s[0:C_HALO, :] = tail

    @pl.when(is_sample)
    def _():
        for b in range(N_SEQ):
            full_s[0:C_HALO, :] = st_ref[b]
            full_s[C_HALO:C_HALO + SAMPLE_SEQ, :] = gl[b * SAMPLE_SEQ:(b + 1) * SAMPLE_SEQ]
            conv(SAMPLE_SEQ, b * SAMPLE_SEQ)
            tails_ref[b] = full_s[SAMPLE_SEQ:SAMPLE_SEQ + C_HALO, :]
        tailp_ref[0] = jnp.zeros((C_HALO, D), F32)

    y = _layernorm(conv_s[...] + bdw_ref[...], lng_ref[...], lnb_ref[...])
    y_s[...] = (y * jax.nn.sigmoid(y)).astype(BF16)
    o_ref[...] = x + _dot(y_s[...], w2_ref[...]) + b2_ref[...]


def _conf_mixer(x, g, w_pw1, b_pw1, w_dw, b_dw, ln_g, ln_b, w_pw2, b_pw2, state):
    st = jnp.pad(state, ((0, 0), (C_HALO - (C_WIDTH - 1), 0), (0, 0)))
    dw = jnp.broadcast_to(w_dw[:, None, :], (C_WIDTH, SUBLANES, D))
    return pl.pallas_call(
        _conf_kernel,
        grid=(N_TILES,),
        in_specs=[
            pl.BlockSpec((TM, D), lambda i: (i, 0)),
            _const_spec((1, D)),
            _const_spec((D, 2 * D)),
            _const_spec((1, 2 * D)),
            _const_spec((C_WIDTH, SUBLANES, D)),
            _const_spec((1, D)),
            _const_spec((1, D)),
            _const_spec((1, D)),
            _const_spec((D, D)),
            _const_spec((1, D)),
            _const_spec((N_SEQ, C_HALO, D)),
        ],
        out_specs=[
            pl.BlockSpec((TM, D), lambda i: (i, 0)),
            pl.BlockSpec((1, C_HALO, D), lambda i: (i, 0, 0)),
            pl.BlockSpec((N_SEQ, C_HALO, D), lambda i: (0, 0, 0)),
        ],
        out_shape=[
            jax.ShapeDtypeStruct((N_TOK, D), F32),
            jax.ShapeDtypeStruct((N_TILES, C_HALO, D), F32),
            jax.ShapeDtypeStruct((N_SEQ, C_HALO, D), F32),
        ],
        scratch_shapes=[
            pltpu.VMEM((C_HALO + TM, D), F32),
            pltpu.VMEM((SUBLANES - 1, C_HALO + TM, D), F32),
            pltpu.VMEM((TM, D), F32),
            pltpu.VMEM((TM, D), BF16),
        ],
        compiler_params=pltpu.CompilerParams(dimension_semantics=("arbitrary",), vmem_limit_bytes=VMEM_LIMIT),
        name="conf_mixer",
    )(x, g.reshape(1, D), w_pw1.astype(BF16), b_pw1.reshape(1, 2 * D), dw, b_dw.reshape(1, D),
      ln_g.reshape(1, D), ln_b.reshape(1, D), w_pw2.astype(BF16), b_pw2.reshape(1, D), st)


def _swiglu_hidden(h, wg_ref, wu_ref, hid_s, lead=()):
    for c0, cn in FF_CHUNKS:
        a = _dot(h, wg_ref[lead + (slice(None), slice(c0, c0 + cn))])
        b = _dot(h, wu_ref[lead + (slice(None), slice(c0, c0 + cn))])
        hid_s[:, c0:c0 + cn] = (a * jax.nn.sigmoid(a) * b).astype(BF16)


def _ffn_kernel(x_ref, g_ref, wg_ref, wu_ref, wd_ref, o_ref, hid_s):
    x = x_ref[...]
    h = _rmsnorm(x, g_ref[...]).astype(BF16)
    _swiglu_hidden(h, wg_ref, wu_ref, hid_s, lead=(0,))
    o_ref[...] = x + _dot(hid_s[...], wd_ref[0])


def _dense_ffn(x, g, layer, w_gate, w_up, w_down):
    wspec = lambda shape: pl.BlockSpec((1,) + shape, lambda i: (layer, 0, 0), pipeline_mode=pl.Buffered(1))
    return pl.pallas_call(
        _ffn_kernel,
        grid=(N_TILES,),
        in_specs=[
            pl.BlockSpec((TM, D), lambda i: (i, 0)),
            _const_spec((1, D)),
            wspec((D, D_FF)),
            wspec((D, D_FF)),
            wspec((D_FF, D)),
        ],
        out_specs=pl.BlockSpec((TM, D), lambda i: (i, 0)),
        out_shape=jax.ShapeDtypeStruct((N_TOK, D), F32),
        scratch_shapes=[pltpu.VMEM((TM, D_FF), BF16)],
        compiler_params=pltpu.CompilerParams(dimension_semantics=("arbitrary",), vmem_limit_bytes=VMEM_LIMIT),
        name="dense_ffn",
    )(x, g.reshape(1, D), w_gate, w_up, w_down)


def _router_kernel(x_ref, g_ref, wr_ref, meta_ref, wts_ref, cnt_ref, run_s):
    i = pl.program_id(0)

    @pl.when(i == 0)
    def _():
        run_s[...] = jnp.zeros((1, LANES), F32)

    h = _rmsnorm(x_ref[...], g_ref[...])
    logits = _dot(h.astype(BF16), wr_ref[...].astype(BF16))
    lane = lax.broadcasted_iota(jnp.int32, (TM, LANES), 1)
    lane_f = lane.astype(F32)
    neg = jnp.float32(-jnp.inf)
    logits = jnp.where(lane < N_EXPERTS, logits, neg)
    l1 = jnp.max(logits, axis=-1, keepdims=True)
    e1 = jnp.min(jnp.where(logits == l1, lane_f, float(LANES)), axis=-1, keepdims=True).astype(jnp.int32)
    rest = jnp.where(lane == e1, neg, logits)
    l2 = jnp.max(rest, axis=-1, keepdims=True)
    e2 = jnp.min(jnp.where(rest == l2, lane_f, float(LANES)), axis=-1, keepdims=True).astype(jnp.int32)
    t = jnp.exp(l2 - l1)
    w1 = 1.0 / (1.0 + t)
    w2 = t * w1
    sel = jnp.logical_or(lane == e1, lane == e2)
    onehot = jnp.where(sel, 1.0, 0.0)
    r = lax.broadcasted_iota(jnp.int32, (TM, TM), 0)
    c = lax.broadcasted_iota(jnp.int32, (TM, TM), 1)
    below = jnp.where(r > c, 1.0, 0.0).astype(BF16)
    before = _dot(below, onehot.astype(BF16)) + run_s[...]
    r1 = jnp.sum(jnp.where(lane == e1, before, 0.0), axis=-1, keepdims=True).astype(jnp.int32)
    r2 = jnp.sum(jnp.where(lane == e2, before, 0.0), axis=-1, keepdims=True).astype(jnp.int32)
    run_s[...] = run_s[...] + jnp.sum(onehot, axis=0, keepdims=True)
    meta = jnp.where(lane == 0, e1, jnp.where(lane == 1, e2, jnp.where(lane == 2, r1, jnp.where(lane == 3, r2, 0))))
    meta_ref[...] = meta
    wts_ref[...] = jnp.where(lane == 0, w1, jnp.where(lane == 1, w2, 0.0))
    cnt_ref[...] = run_s[...].astype(jnp.int32)


def _router(x, g, w_router):
    wr = jnp.pad(w_router, ((0, 0), (0, LANES - N_EXPERTS)))
    return pl.pallas_call(
        _router_kernel,
        grid=(N_TILES,),
        in_specs=[
            pl.BlockSpec((TM, D), lambda i: (i, 0)),
            _const_spec((1, D)),
            _const_spec((D, LANES)),
        ],
        out_specs=[
            pl.BlockSpec((TM, LANES), lambda i: (i, 0)),
            pl.BlockSpec((TM, LANES), lambda i: (i, 0)),
            pl.BlockSpec((1, LANES), lambda i: (0, 0)),
        ],
        out_shape=[
            jax.ShapeDtypeStruct((N_TOK, LANES), jnp.int32),
            jax.ShapeDtypeStruct((N_TOK, LANES), F32),
            jax.ShapeDtypeStruct((1, LANES), jnp.int32),
        ],
        scratch_shapes=[pltpu.VMEM((1, LANES), F32)],
        compiler_params=pltpu.CompilerParams(dimension_semantics=("arbitrary",), vmem_limit_bytes=VMEM_LIMIT),
        name="moe_router",
    )(x, g.reshape(1, D), wr)


def _row_copy(src, src_row, dst, dst_row, sem):
    return pltpu.make_async_copy(src.at[pl.ds(src_row, 1), :], dst.at[pl.ds(dst_row, 1), :], sem)


DMA_UNROLL = 8


def _dispatch_kernel(last_ref, nblk_ref, nu_ref, dest_ref, x_ref, xs_hbm, zero_s, sem, zsem):
    def zero_block(b):
        return pltpu.make_async_copy(zero_s, xs_hbm.at[pl.ds(b * BM, BM), :], zsem)

    @pl.when(pl.program_id(0) == 0)
    def _():
        zero_s[...] = jnp.zeros((BM, D), F32)
        for start in (True, False):
            for e in range(N_EXPERTS):
                tail = N_BLOCKS - 1 - e
                for cond, b in ((nblk_ref[e] > 0, last_ref[e]), (tail >= nu_ref[0], tail)):
                    cp = zero_block(b)
                    pl.when(cond)(cp.start if start else cp.wait)

    def issue(j, carry):
        for u in range(DMA_UNROLL):
            t = j * DMA_UNROLL + u
            _row_copy(x_ref, t, xs_hbm, dest_ref[0, 0, 2 * t], sem).start(priority=0)
            _row_copy(x_ref, t, xs_hbm, dest_ref[0, 0, 2 * t + 1], sem).start(priority=1)
        return carry

    lax.fori_loop(0, TM // DMA_UNROLL, issue, 0)

    def drain(j, carry):
        for _ in range(2 * DMA_UNROLL):
            _row_copy(x_ref, 0, xs_hbm, 0, sem).wait()
        return carry

    lax.fori_loop(0, TM // DMA_UNROLL, drain, 0)


def _dispatch(x, dest, last_block, n_blocks, n_used):
    return pl.pallas_call(
        _dispatch_kernel,
        grid_spec=pltpu.PrefetchScalarGridSpec(
            num_scalar_prefetch=3,
            grid=(N_TILES,),
            in_specs=[
                pl.BlockSpec((1, 1, 2 * TM), lambda i, *_: (i, 0, 0), memory_space=pltpu.SMEM),
                pl.BlockSpec((TM, D), lambda i, *_: (i, 0)),
            ],
            out_specs=pl.BlockSpec(memory_space=pl.ANY),
            scratch_shapes=[pltpu.VMEM((BM, D), F32), pltpu.SemaphoreType.DMA(()), pltpu.SemaphoreType.DMA(())],
        ),
        out_shape=jax.ShapeDtypeStruct((N_ROWS, D), F32),
        compiler_params=pltpu.CompilerParams(dimension_semantics=("arbitrary",), vmem_limit_bytes=VMEM_LIMIT),
        name="moe_dispatch",
    )(last_block, n_blocks, n_used, dest.reshape(N_TILES, 1, 2 * TM), x)


def _gmm_kernel(be_ref, nu_ref, xs_ref, g_ref, wg_ref, wu_ref, wd_ref, ys_ref, hid_s):
    used = pl.program_id(0) < nu_ref[0]

    @pl.when(used)
    def _():
        h = _rmsnorm(xs_ref[...], g_ref[...]).astype(BF16)
        _swiglu_hidden(h, wg_ref, wu_ref, hid_s, lead=(0, 0))
        ys_ref[...] = _dot(hid_s[...], wd_ref[0, 0])

    @pl.when(jnp.logical_not(used))
    def _():
        ys_ref[...] = jnp.zeros((BM, D), F32)


def _gmm(xs, g, block_expert, n_used, layer, w_gate, w_up, w_down):
    row_map = lambda i, be, nu: (jnp.minimum(i, nu[0] - 1), 0)
    wspec = lambda shape: pl.BlockSpec((1, 1) + shape, lambda i, be, nu: (layer, be[i], 0, 0))
    return pl.pallas_call(
        _gmm_kernel,
        grid_spec=pltpu.PrefetchScalarGridSpec(
            num_scalar_prefetch=2,
            grid=(N_BLOCKS,),
            in_specs=[
                pl.BlockSpec((BM, D), row_map),
                pl.BlockSpec((1, D), lambda i, be, nu: (0, 0)),
                wspec((D, D_FF)),
                wspec((D, D_FF)),
                wspec((D_FF, D)),
            ],
            out_specs=pl.BlockSpec((BM, D), lambda i, be, nu: (i, 0)),
            scratch_shapes=[pltpu.VMEM((BM, D_FF), BF16)],
        ),
        out_shape=jax.ShapeDtypeStruct((N_ROWS, D), F32),
        compiler_params=pltpu.CompilerParams(dimension_semantics=("arbitrary",), vmem_limit_bytes=VMEM_LIMIT),
        name="moe_gmm",
    )(block_expert, n_used, xs, g.reshape(1, D), w_gate, w_up, w_down)


def _combine_kernel(dest_ref, x_ref, wts_ref, gf_ref, ys_hbm, *rest, final):
    out_refs, (buf, sem) = rest[:-2], rest[-2:]

    def issue(j, carry):
        for u in range(DMA_UNROLL):
            t = j * DMA_UNROLL + u
            _row_copy(ys_hbm, dest_ref[0, 0, 2 * t], buf.at[0], t, sem).start(priority=0)
            _row_copy(ys_hbm, dest_ref[0, 0, 2 * t + 1], buf.at[1], t, sem).start(priority=1)
        return carry

    lax.fori_loop(0, TM // DMA_UNROLL, issue, 0)

    def drain(j, carry):
        for _ in range(DMA_UNROLL):
            _row_copy(ys_hbm, 0, buf.at[0], 0, sem).wait()
            _row_copy(ys_hbm, 0, buf.at[1], 0, sem).wait()
        return carry

    lax.fori_loop(0, TM // DMA_UNROLL, drain, 0)
    w = wts_ref[...]
    y = x_ref[...] + w[:, 0:1] * buf[0] + w[:, 1:2] * buf[1]
    if final:
        y = _rmsnorm(y, gf_ref[...])
        yp_ref, ysm_ref = out_refs
        is_sample = pl.program_id(0) >= N_PROMPT_TILES

        @pl.when(jnp.logical_not(is_sample))
        def _():
            yp_ref[...] = y

        @pl.when(is_sample)
        def _():
            ysm_ref[...] = y
    else:
        out_refs[0][...] = y


def _combine(x, dest, wts, ys, g_final, final):
    if final:
        out_specs = [pl.BlockSpec((TM, D), lambda i: (jnp.minimum(i, N_PROMPT_TILES - 1), 0)),
                     pl.BlockSpec((TM, D), lambda i: (0, 0))]
        out_shape = [jax.ShapeDtypeStruct((N_PROMPT_TOK, D), F32), jax.ShapeDtypeStruct((N_SAMPLE_TOK, D), F32)]
    else:
        out_specs = [pl.BlockSpec((TM, D), lambda i: (i, 0))]
        out_shape = [jax.ShapeDtypeStruct((N_TOK, D), F32)]
    outs = pl.pallas_call(
        functools.partial(_combine_kernel, final=final),
        grid=(N_TILES,),
        in_specs=[
            pl.BlockSpec((1, 1, 2 * TM), lambda i: (i, 0, 0), memory_space=pltpu.SMEM),
            pl.BlockSpec((TM, D), lambda i: (i, 0)),
            pl.BlockSpec((TM, LANES), lambda i: (i, 0)),
            _const_spec((1, D)),
            pl.BlockSpec(memory_space=pl.ANY),
        ],
        out_specs=out_specs,
        out_shape=out_shape,
        scratch_shapes=[pltpu.VMEM((2, TM, D), F32), pltpu.SemaphoreType.DMA(())],
        compiler_params=pltpu.CompilerParams(dimension_semantics=("arbitrary",), vmem_limit_bytes=VMEM_LIMIT),
        name="moe_combine",
    )(dest.reshape(N_TILES, 1, 2 * TM), x, wts, g_final.reshape(1, D), ys)
    return outs if final else outs[0]


def _moe_ffn(x, g, w_router, layer, w_gate, w_up, w_down, g_final, final):
    meta, wts, counts = _router(x, g, w_router)
    counts = counts[0, :N_EXPERTS]
    blocks = (counts + BM - 1) // BM
    block_end = jnp.cumsum(blocks)
    seg_start = (block_end - blocks) * BM
    dest = seg_start[meta[:, 0:2]] + meta[:, 2:4]
    n_used = block_end[-1:]
    bidx = jnp.minimum(jnp.arange(N_BLOCKS, dtype=jnp.int32), n_used - 1)
    block_expert = jnp.sum(bidx[:, None] >= block_end[None, :], axis=1).astype(jnp.int32)
    n_used = n_used.astype(jnp.int32)
    xs = _dispatch(x, dest, (block_end - 1).astype(jnp.int32), blocks.astype(jnp.int32), n_used)
    ys = _gmm(xs, g, block_expert, n_used, layer, w_gate, w_up, w_down)
    return _combine(x, dest, wts, ys, g_final, final)


def kernel(x_prompt, x_sample, state_conv_b, state_conv_c, norm_mix, norm_ffn, norm_final, a_w_in, a_ln_g, a_ln_b, a_w_s, a_b_s, a_w_out, b_w_in, b_conv, b_w_out, c_w_pw1, c_b_pw1, c_dw, c_b_dw, c_ln_g, c_ln_b, c_w_pw2, c_b_pw2, f_w_gate, f_w_up, f_w_down, m_router, m_w_gate, m_w_up, m_w_down):
    last_tile = jnp.arange(N_SEQ) * TILES_PER_SEQ + TILES_PER_SEQ - 1
    fw = [w.astype(BF16) for w in (f_w_gate, f_w_up, f_w_down)]
    mw = [w.astype(BF16) for w in (m_w_gate, m_w_up, m_w_down)]

    x, v0 = _gmlp_mixer(x_prompt.reshape(N_PROMPT_TOK, D), x_sample.reshape(N_SAMPLE_TOK, D), 0,
                        norm_mix[0], a_w_in[0], a_ln_g[0], a_ln_b[0], a_w_s[0], a_b_s[0], a_w_out[0])
    x = _dense_ffn(x, norm_ffn[0], 0, *fw)
    x, tb_p, tb_s = _sconv_mixer(x, norm_mix[1], b_w_in[0], b_conv[0], b_w_out[0], state_conv_b[0])
    x = _moe_ffn(x, norm_ffn[1], m_router[0], 0, *mw, norm_final, False)
    x, tc_p, tc_s = _conf_mixer(x, norm_mix[2], c_w_pw1[0], c_b_pw1[0], c_dw[0], c_b_dw[0], c_ln_g[0], c_ln_b[0],
                                c_w_pw2[0], c_b_pw2[0], state_conv_c[0])
    x = _dense_ffn(x, norm_ffn[2], 1, *fw)
    x, v1 = _gmlp_mixer(x, x, N_PROMPT_TILES,
                        norm_mix[3], a_w_in[1], a_ln_g[1], a_ln_b[1], a_w_s[1], a_b_s[1], a_w_out[1])
    y_prompt, y_sample = _moe_ffn(x, norm_ffn[3], m_router[1], 1, *mw, norm_final, True)

    y_prompt = y_prompt.reshape(N_SEQ, SEQ, D)
    y_sample = y_sample.reshape(N_SEQ, SAMPLE_SEQ, D)
    nb = B_WIDTH - 1
    nc = C_WIDTH - 1
    new_b_p = tb_p[last_tile, B_HALO - nb:, :][None]
    new_b_s = tb_s[:, B_HALO - nb:, :][None]
    new_c_p = tc_p[last_tile, C_HALO - nc:, :][None]
    new_c_s = tc_s[:, C_HALO - nc:, :][None]
    new_v = jnp.stack([v0, v1]).reshape(2, N_SEQ, SAMPLE_SEQ, A_HALF)
    return (y_prompt, y_sample, new_b_p, new_b_s, new_c_p, new_c_s, new_v)
```

```python
import functools

import jax
import jax.numpy as jnp
from jax import lax
from jax.experimental import pallas as pl
from jax.experimental.pallas import tpu as pltpu

F32 = jnp.float32
BF16 = jnp.bfloat16

D = 1024
TM = 512
SEQ = 4096
SAMPLE_SEQ = 64
N_SEQ = 8
TILES_PER_SEQ = SEQ // TM
N_PROMPT_TOK = N_SEQ * SEQ
N_SAMPLE_TOK = N_SEQ * SAMPLE_SEQ
N_TOK = N_PROMPT_TOK + N_SAMPLE_TOK
N_PROMPT_TILES = N_PROMPT_TOK // TM
N_TILES = N_TOK // TM
assert N_SAMPLE_TOK == TM and N_TILES == N_PROMPT_TILES + 1

A_HALF = 2 * D
A_GROUPS = 8
A_HEAD = A_HALF // A_GROUPS
A_CHUNK = 128
B_WIDTH = 3
C_WIDTH = 31
B_HALO = 8
C_HALO = 32
D_FF = 2816
N_EXPERTS = 8
LANES = 128
SUBLANES = 8
MXU_COLS = 256
RMS_EPS = 1e-6
LN_EPS = 1e-5

BM = TM
N_ROWS = 2 * N_TOK + N_EXPERTS * BM
N_BLOCKS = N_ROWS // BM
FF_CHUNKS = ((0, 1024), (1024, 1024), (2048, 768))

VMEM_LIMIT = 56 * 1024 * 1024


def _const_spec(shape):
    return pl.BlockSpec(shape, lambda *_: (0,) * len(shape), pipeline_mode=pl.Buffered(1))


def _rmsnorm(x, g):
    return x * lax.rsqrt(jnp.mean(x * x, axis=-1, keepdims=True) + RMS_EPS) * g


def _layernorm(x, g, b):
    mu = jnp.mean(x, axis=-1, keepdims=True)
    xc = x - mu
    var = jnp.mean(xc * xc, axis=-1, keepdims=True)
    return xc * lax.rsqrt(var + LN_EPS) * g + b


def _dot(a, b):
    return jnp.dot(a, b, preferred_element_type=F32)


def _gmlp_kernel(xp_ref, xs_ref, g_ref, win_ref, lng_ref, lnb_ref, ws_ref, bs_ref, wout_ref,
                 o_ref, v_ref, vn_s, y_s):
    is_sample = pl.program_id(0) >= N_PROMPT_TILES
    x = jnp.where(is_sample, xs_ref[...], xp_ref[...])
    h = _rmsnorm(x, g_ref[...]).astype(BF16)
    v = jax.nn.gelu(_dot(h, win_ref[:, A_HALF:]))
    vn = _layernorm(v, lng_ref[...], lnb_ref[...])
    v_ref[...] = vn
    vn_s[...] = vn.astype(BF16)
    r = lax.broadcasted_iota(jnp.int32, (A_CHUNK, A_CHUNK), 0)
    c = lax.broadcasted_iota(jnp.int32, (A_CHUNK, A_CHUNK), 1)
    seg_shift = jnp.where(is_sample, 6, 7)
    mask = (r >= c) & ((r >> seg_shift) == (c >> seg_shift))
    for g in range(A_GROUPS):
        cols = slice(g * A_HEAD, (g + 1) * A_HEAD)
        u_g = jax.nn.gelu(_dot(h, win_ref[:, cols]))
        ws = jnp.where(mask, ws_ref[0, g], 0.0).astype(BF16)
        bias = jnp.concatenate([bs_ref[0, g]] * (A_HEAD // LANES), axis=1)
        for ch in range(TM // A_CHUNK):
            rows = slice(ch * A_CHUNK, (ch + 1) * A_CHUNK)
            s = _dot(ws, vn_s[rows, cols]) + bias
            y_s[rows, cols] = (u_g[rows] * s).astype(BF16)
    o_ref[...] = x + _dot(y_s[...], wout_ref[...])


def _gmlp_mixer(x_prompt, x_sample, sample_block, g, w_in, ln_g, ln_b, w_s, b_s, w_out):
    ws2 = jnp.stack([w_s, jnp.tile(w_s[:, :SAMPLE_SEQ, :SAMPLE_SEQ], (1, 2, 2))])
    b2 = jnp.stack([b_s, jnp.tile(b_s[:, :SAMPLE_SEQ], (1, 2))])
    b2 = jnp.broadcast_to(b2[..., None], (2, A_GROUPS, A_CHUNK, LANES))
    return pl.pallas_call(
        _gmlp_kernel,
        grid=(N_TILES,),
        in_specs=[
            pl.BlockSpec((TM, D), lambda i: (jnp.minimum(i, N_PROMPT_TILES - 1), 0)),
            pl.BlockSpec((TM, D), lambda i: (sample_block, 0)),
            _const_spec((1, D)),
            _const_spec((D, 2 * A_HALF)),
            _const_spec((1, A_HALF)),
            _const_spec((1, A_HALF)),
            pl.BlockSpec((1, A_GROUPS, A_CHUNK, A_CHUNK), lambda i: (i // N_PROMPT_TILES, 0, 0, 0)),
            pl.BlockSpec((1, A_GROUPS, A_CHUNK, LANES), lambda i: (i // N_PROMPT_TILES, 0, 0, 0)),
            _const_spec((A_HALF, D)),
        ],
        out_specs=[
            pl.BlockSpec((TM, D), lambda i: (i, 0)),
            pl.BlockSpec((TM, A_HALF), lambda i: (0, 0)),
        ],
        out_shape=[
            jax.ShapeDtypeStruct((N_TOK, D), F32),
            jax.ShapeDtypeStruct((N_SAMPLE_TOK, A_HALF), F32),
        ],
        scratch_shapes=[pltpu.VMEM((TM, A_HALF), BF16), pltpu.VMEM((TM, A_HALF), BF16)],
        compiler_params=pltpu.CompilerParams(dimension_semantics=("arbitrary",), vmem_limit_bytes=VMEM_LIMIT),
        name="gmlp_mixer",
    )(x_prompt, x_sample, g.reshape(1, D), w_in.astype(BF16), ln_g.reshape(1, A_HALF), ln_b.reshape(1, A_HALF),
      ws2, b2, w_out.astype(BF16))


def _sconv_kernel(x_ref, g_ref, win_ref, cw_ref, wout_ref, st_ref,
                  o_ref, tailp_ref, tails_ref, full_s, y_s):
    i = pl.program_id(0)
    is_sample = i >= N_PROMPT_TILES
    x = x_ref[...]
    h = _rmsnorm(x, g_ref[...]).astype(BF16)
    p = _dot(h, win_ref[...])
    bg = p[:, :D]
    u = p[:, D:2 * D] * p[:, 2 * D:]
    w = [cw_ref[k:k + 1, :] for k in range(B_WIDTH)]

    def conv(n):
        return sum(w[k] * full_s[pl.ds(B_HALO - (B_WIDTH - 1) + k, n), :] for k in range(B_WIDTH))

    @pl.when(jnp.logical_not(is_sample))
    def _():
        @pl.when(i % TILES_PER_SEQ == 0)
        def _():
            full_s[0:B_HALO, :] = jnp.zeros((B_HALO, D), F32)

        full_s[B_HALO:, :] = u
        y_s[...] = (bg * conv(TM)).astype(BF16)
        tail = full_s[TM:TM + B_HALO, :]
        tailp_ref[0] = tail
        full_s[0:B_HALO, :] = tail

    @pl.when(is_sample)
    def _():
        for b in range(N_SEQ):
            rows = slice(b * SAMPLE_SEQ, (b + 1) * SAMPLE_SEQ)
            full_s[0:B_HALO, :] = st_ref[b]
            full_s[B_HALO:B_HALO + SAMPLE_SEQ, :] = u[rows]
            y_s[rows, :] = (bg[rows] * conv(SAMPLE_SEQ)).astype(BF16)
            tails_ref[b] = full_s[SAMPLE_SEQ:SAMPLE_SEQ + B_HALO, :]
        tailp_ref[0] = jnp.zeros((B_HALO, D), F32)

    o_ref[...] = x + _dot(y_s[...], wout_ref[...])


def _sconv_mixer(x, g, w_in, w_conv, w_out, state):
    st = jnp.pad(state, ((0, 0), (B_HALO - (B_WIDTH - 1), 0), (0, 0)))
    return pl.pallas_call(
        _sconv_kernel,
        grid=(N_TILES,),
        in_specs=[
            pl.BlockSpec((TM, D), lambda i: (i, 0)),
            _const_spec((1, D)),
            _const_spec((D, 3 * D)),
            _const_spec((B_WIDTH, D)),
            _const_spec((D, D)),
            _const_spec((N_SEQ, B_HALO, D)),
        ],
        out_specs=[
            pl.BlockSpec((TM, D), lambda i: (i, 0)),
            pl.BlockSpec((1, B_HALO, D), lambda i: (i, 0, 0)),
            pl.BlockSpec((N_SEQ, B_HALO, D), lambda i: (0, 0, 0)),
        ],
        out_shape=[
            jax.ShapeDtypeStruct((N_TOK, D), F32),
            jax.ShapeDtypeStruct((N_TILES, B_HALO, D), F32),
            jax.ShapeDtypeStruct((N_SEQ, B_HALO, D), F32),
        ],
        scratch_shapes=[pltpu.VMEM((B_HALO + TM, D), F32), pltpu.VMEM((TM, D), BF16)],
        compiler_params=pltpu.CompilerParams(dimension_semantics=("arbitrary",), vmem_limit_bytes=VMEM_LIMIT),
        name="sconv_mixer",
    )(x, g.reshape(1, D), w_in.astype(BF16), w_conv, w_out.astype(BF16), st)


C_ROW_BLOCK = 32


def _conf_kernel(x_ref, g_ref, w1_ref, b1_ref, dw_ref, bdw_ref, lng_ref, lnb_ref, w2_ref, b2_ref, st_ref,
                 o_ref, tailp_ref, tails_ref, full_s, shift_s, conv_s, y_s):
    i = pl.program_id(0)
    is_sample = i >= N_PROMPT_TILES
    x = x_ref[...]
    h = _rmsnorm(x, g_ref[...]).astype(BF16)
    p = _dot(h, w1_ref[...]) + b1_ref[...]
    gl = p[:, :D] * jax.nn.sigmoid(p[:, D:])

    def conv(n, out_row0):
        n_in = C_HALO + n
        for r in range(1, SUBLANES):
            shift_s[r - 1, 0:n_in - SUBLANES, :] = full_s[r:r + n_in - SUBLANES, :]

        def block(rb, carry):
            base = pl.multiple_of(rb * C_ROW_BLOCK, C_ROW_BLOCK)
            acc = [jnp.zeros((SUBLANES, D), F32) for _ in range(C_ROW_BLOCK // SUBLANES)]
            for k in range(C_WIDTH):
                off = k + (C_HALO - (C_WIDTH - 1))
                row0 = base + (off // SUBLANES) * SUBLANES
                wk = dw_ref[k]
                for a in range(len(acc)):
                    rows = pl.ds(row0 + a * SUBLANES, SUBLANES)
                    tap = full_s[rows, :] if off % SUBLANES == 0 else shift_s[off % SUBLANES - 1, rows, :]
                    acc[a] = acc[a] + wk * tap
            conv_s[pl.ds(out_row0 + base, C_ROW_BLOCK), :] = jnp.concatenate(acc, axis=0)
            return carry

        lax.fori_loop(0, n // C_ROW_BLOCK, block, 0)

    @pl.when(jnp.logical_not(is_sample))
    def _():
        @pl.when(i % TILES_PER_SEQ == 0)
        def _():
            full_s[0:C_HALO, :] = jnp.zeros((C_HALO, D), F32)

        full_s[C_HALO:, :] = gl
        conv(TM, 0)
        tail = full_s[TM:TM + C_HALO, :]
        tailp_ref[0] = tail
        full_s[0:C_HALO, :] = tail

    @pl.when(is_sample)
    def _():
        for b in range(N_SEQ):
            full_s[0:C_HALO, :] = st_ref[b]
            full_s[C_HALO:C_HALO + SAMPLE_SEQ, :] = gl[b * SAMPLE_SEQ:(b + 1) * SAMPLE_SEQ]
            conv(SAMPLE_SEQ, b * SAMPLE_SEQ)
            tails_ref[b] = full_s[SAMPLE_SEQ:SAMPLE_SEQ + C_HALO, :]
        tailp_ref[0] = jnp.zeros((C_HALO, D), F32)

    y = _layernorm(conv_s[...] + bdw_ref[...], lng_ref[...], lnb_ref[...])
    y_s[...] = (y * jax.nn.sigmoid(y)).astype(BF16)
    o_ref[...] = x + _dot(y_s[...], w2_ref[...]) + b2_ref[...]


def _conf_mixer(x, g, w_pw1, b_pw1, w_dw, b_dw, ln_g, ln_b, w_pw2, b_pw2, state):
    st = jnp.pad(state, ((0, 0), (C_HALO - (C_WIDTH - 1), 0), (0, 0)))
    dw = jnp.broadcast_to(w_dw[:, None, :], (C_WIDTH, SUBLANES, D))
    return pl.pallas_call(
        _conf_kernel,
        grid=(N_TILES,),
        in_specs=[
            pl.BlockSpec((TM, D), lambda i: (i, 0)),
            _const_spec((1, D)),
            _const_spec((D, 2 * D)),
            _const_spec((1, 2 * D)),
            _const_spec((C_WIDTH, SUBLANES, D)),
            _const_spec((1, D)),
            _const_spec((1, D)),
            _const_spec((1, D)),
            _const_spec((D, D)),
            _const_spec((1, D)),
            _const_spec((N_SEQ, C_HALO, D)),
        ],
        out_specs=[
            pl.BlockSpec((TM, D), lambda i: (i, 0)),
            pl.BlockSpec((1, C_HALO, D), lambda i: (i, 0, 0)),
            pl.BlockSpec((N_SEQ, C_HALO, D), lambda i: (0, 0, 0)),
        ],
        out_shape=[
            jax.ShapeDtypeStruct((N_TOK, D), F32),
            jax.ShapeDtypeStruct((N_TILES, C_HALO, D), F32),
            jax.ShapeDtypeStruct((N_SEQ, C_HALO, D), F32),
        ],
        scratch_shapes=[
            pltpu.VMEM((C_HALO + TM, D), F32),
            pltpu.VMEM((SUBLANES - 1, C_HALO + TM, D), F32),
            pltpu.VMEM((TM, D), F32),
            pltpu.VMEM((TM, D), BF16),
        ],
        compiler_params=pltpu.CompilerParams(dimension_semantics=("arbitrary",), vmem_limit_bytes=VMEM_LIMIT),
        name="conf_mixer",
    )(x, g.reshape(1, D), w_pw1.astype(BF16), b_pw1.reshape(1, 2 * D), dw, b_dw.reshape(1, D),
      ln_g.reshape(1, D), ln_b.reshape(1, D), w_pw2.astype(BF16), b_pw2.reshape(1, D), st)


def _swiglu_hidden(h, wg_ref, wu_ref, hid_s, lead=()):
    for c0, cn in FF_CHUNKS:
        a = _dot(h, wg_ref[lead + (slice(None), slice(c0, c0 + cn))])
        b = _dot(h, wu_ref[lead + (slice(None), slice(c0, c0 + cn))])
        hid_s[:, c0:c0 + cn] = (a * jax.nn.sigmoid(a) * b).astype(BF16)


def _ffn_kernel(x_ref, g_ref, wg_ref, wu_ref, wd_ref, o_ref, hid_s):
    x = x_ref[...]
    h = _rmsnorm(x, g_ref[...]).astype(BF16)
    _swiglu_hidden(h, wg_ref, wu_ref, hid_s, lead=(0,))
    o_ref[...] = x + _dot(hid_s[...], wd_ref[0])


def _dense_ffn(x, g, layer, w_gate, w_up, w_down):
    wspec = lambda shape: pl.BlockSpec((1,) + shape, lambda i: (layer, 0, 0), pipeline_mode=pl.Buffered(1))
    return pl.pallas_call(
        _ffn_kernel,
        grid=(N_TILES,),
        in_specs=[
            pl.BlockSpec((TM, D), lambda i: (i, 0)),
            _const_spec((1, D)),
            wspec((D, D_FF)),
            wspec((D, D_FF)),
            wspec((D_FF, D)),
        ],
        out_specs=pl.BlockSpec((TM, D), lambda i: (i, 0)),
        out_shape=jax.ShapeDtypeStruct((N_TOK, D), F32),
        scratch_shapes=[pltpu.VMEM((TM, D_FF), BF16)],
        compiler_params=pltpu.CompilerParams(dimension_semantics=("arbitrary",), vmem_limit_bytes=VMEM_LIMIT),
        name="dense_ffn",
    )(x, g.reshape(1, D), w_gate, w_up, w_down)


def _router_kernel(x_ref, g_ref, wr_ref, meta_ref, wts_ref, cnt_ref, run_s):
    i = pl.program_id(0)

    @pl.when(i == 0)
    def _():
        run_s[...] = jnp.zeros((1, LANES), F32)

    h = _rmsnorm(x_ref[...], g_ref[...])
    logits = _dot(h.astype(BF16), wr_ref[...].astype(BF16))
    lane = lax.broadcasted_iota(jnp.int32, (TM, LANES), 1)
    lane_f = lane.astype(F32)
    neg = jnp.float32(-jnp.inf)
    logits = jnp.where(lane < N_EXPERTS, logits, neg)
    l1 = jnp.max(logits, axis=-1, keepdims=True)
    e1 = jnp.min(jnp.where(logits == l1, lane_f, float(LANES)), axis=-1, keepdims=True).astype(jnp.int32)
    rest = jnp.where(lane == e1, neg, logits)
    l2 = jnp.max(rest, axis=-1, keepdims=True)
    e2 = jnp.min(jnp.where(rest == l2, lane_f, float(LANES)), axis=-1, keepdims=True).astype(jnp.int32)
    t = jnp.exp(l2 - l1)
    w1 = 1.0 / (1.0 + t)
    w2 = t * w1
    sel = jnp.logical_or(lane == e1, lane == e2)
    onehot = jnp.where(sel, 1.0, 0.0)
    r = lax.broadcasted_iota(jnp.int32, (TM, TM), 0)
    c = lax.broadcasted_iota(jnp.int32, (TM, TM), 1)
    below = jnp.where(r > c, 1.0, 0.0).astype(BF16)
    before = _dot(below, onehot.astype(BF16)) + run_s[...]
    r1 = jnp.sum(jnp.where(lane == e1, before, 0.0), axis=-1, keepdims=True).astype(jnp.int32)
    r2 = jnp.sum(jnp.where(lane == e2, before, 0.0), axis=-1, keepdims=True).astype(jnp.int32)
    run_s[...] = run_s[...] + jnp.sum(onehot, axis=0, keepdims=True)
    meta = jnp.where(lane == 0, e1, jnp.where(lane == 1, e2, jnp.where(lane == 2, r1, jnp.where(lane == 3, r2, 0))))
    meta_ref[...] = meta
    wts_ref[...] = jnp.where(lane == 0, w1, jnp.where(lane == 1, w2, 0.0))
    cnt_ref[...] = run_s[...].astype(jnp.int32)


def _router(x, g, w_router):
    wr = jnp.pad(w_router, ((0, 0), (0, LANES - N_EXPERTS)))
    return pl.pallas_call(
        _router_kernel,
        grid=(N_TILES,),
        in_specs=[
            pl.BlockSpec((TM, D), lambda i: (i, 0)),
            _const_spec((1, D)),
            _const_spec((D, LANES)),
        ],
        out_specs=[
            pl.BlockSpec((TM, LANES), lambda i: (i, 0)),
            pl.BlockSpec((TM, LANES), lambda i: (i, 0)),
            pl.BlockSpec((1, LANES), lambda i: (0, 0)),
        ],
        out_shape=[
            jax.ShapeDtypeStruct((N_TOK, LANES), jnp.int32),
            jax.ShapeDtypeStruct((N_TOK, LANES), F32),
            jax.ShapeDtypeStruct((1, LANES), jnp.int32),
        ],
        scratch_shapes=[pltpu.VMEM((1, LANES), F32)],
        compiler_params=pltpu.CompilerParams(dimension_semantics=("arbitrary",), vmem_limit_bytes=VMEM_LIMIT),
        name="moe_router",
    )(x, g.reshape(1, D), wr)


ROW_TILE = D // LANES
assert ROW_TILE == SUBLANES


def _from_token_major(ref, n):
    return jnp.concatenate([ref[pl.ds(s, n, stride=ROW_TILE), :] for s in range(ROW_TILE)], axis=-1)


def _to_token_major(ref, val, n):
    for s in range(ROW_TILE):
        ref[pl.ds(s, n, stride=ROW_TILE), :] = val[:, s * LANES:(s + 1) * LANES]


def _row_copy(src, src_row, dst, dst_row, sem):
    s0 = pl.multiple_of(src_row * ROW_TILE, ROW_TILE)
    d0 = pl.multiple_of(dst_row * ROW_TILE, ROW_TILE)
    return pltpu.make_async_copy(src.at[pl.ds(s0, ROW_TILE), :], dst.at[pl.ds(d0, ROW_TILE), :], sem)


DMA_UNROLL = 8


def _dispatch_kernel(last_ref, nblk_ref, nu_ref, dest_ref, x_ref, xs_hbm, tok_s, sem, zsem):
    def zero_block(b):
        return pltpu.make_async_copy(tok_s, xs_hbm.at[pl.ds(b * (BM * ROW_TILE), BM * ROW_TILE), :], zsem)

    @pl.when(pl.program_id(0) == 0)
    def _():
        tok_s[...] = jnp.zeros((BM * ROW_TILE, LANES), F32)
        for start in (True, False):
            for e in range(N_EXPERTS):
                tail = N_BLOCKS - 1 - e
                for cond, b in ((nblk_ref[e] > 0, last_ref[e]), (tail >= nu_ref[0], tail)):
                    cp = zero_block(b)
                    pl.when(cond)(cp.start if start else cp.wait)

    _to_token_major(tok_s, x_ref[...], TM)

    def issue(j, carry):
        for u in range(DMA_UNROLL):
            t = j * DMA_UNROLL + u
            _row_copy(tok_s, t, xs_hbm, dest_ref[0, 0, 2 * t], sem).start(priority=0)
            _row_copy(tok_s, t, xs_hbm, dest_ref[0, 0, 2 * t + 1], sem).start(priority=1)
        return carry

    lax.fori_loop(0, TM // DMA_UNROLL, issue, 0)

    def drain(j, carry):
        for _ in range(2 * DMA_UNROLL):
            _row_copy(tok_s, 0, xs_hbm, 0, sem).wait()
        return carry

    lax.fori_loop(0, TM // DMA_UNROLL, drain, 0)


def _dispatch(x, dest, last_block, n_blocks, n_used):
    return pl.pallas_call(
        _dispatch_kernel,
        grid_spec=pltpu.PrefetchScalarGridSpec(
            num_scalar_prefetch=3,
            grid=(N_TILES,),
            in_specs=[
                pl.BlockSpec((1, 1, 2 * TM), lambda i, *_: (i, 0, 0), memory_space=pltpu.SMEM),
                pl.BlockSpec((TM, D), lambda i, *_: (i, 0)),
            ],
            out_specs=pl.BlockSpec(memory_space=pl.ANY),
            scratch_shapes=[pltpu.VMEM((TM * ROW_TILE, LANES), F32), pltpu.SemaphoreType.DMA(()),
                            pltpu.SemaphoreType.DMA(())],
        ),
        out_shape=jax.ShapeDtypeStruct((N_ROWS * ROW_TILE, LANES), F32),
        compiler_params=pltpu.CompilerParams(dimension_semantics=("arbitrary",), vmem_limit_bytes=VMEM_LIMIT),
        name="moe_dispatch",
    )(last_block, n_blocks, n_used, dest.reshape(N_TILES, 1, 2 * TM), x)


def _gmm_kernel(be_ref, nu_ref, xs_ref, g_ref, wg_ref, wu_ref, wd_ref, ys_ref, hid_s):
    used = pl.program_id(0) < nu_ref[0]

    @pl.when(used)
    def _():
        h = _rmsnorm(_from_token_major(xs_ref, BM), g_ref[...]).astype(BF16)
        _swiglu_hidden(h, wg_ref, wu_ref, hid_s, lead=(0, 0))
        _to_token_major(ys_ref, _dot(hid_s[...], wd_ref[0, 0]), BM)

    @pl.when(jnp.logical_not(used))
    def _():
        ys_ref[...] = jnp.zeros((BM * ROW_TILE, LANES), F32)


def _gmm(xs, g, block_expert, n_used, layer, w_gate, w_up, w_down):
    row_map = lambda i, be, nu: (jnp.minimum(i, nu[0] - 1), 0)
    wspec = lambda shape: pl.BlockSpec((1, 1) + shape, lambda i, be, nu: (layer, be[i], 0, 0))
    return pl.pallas_call(
        _gmm_kernel,
        grid_spec=pltpu.PrefetchScalarGridSpec(
            num_scalar_prefetch=2,
            grid=(N_BLOCKS,),
            in_specs=[
                pl.BlockSpec((BM * ROW_TILE, LANES), row_map),
                pl.BlockSpec((1, D), lambda i, be, nu: (0, 0)),
                wspec((D, D_FF)),
                wspec((D, D_FF)),
                wspec((D_FF, D)),
            ],
            out_specs=pl.BlockSpec((BM * ROW_TILE, LANES), lambda i, be, nu: (i, 0)),
            scratch_shapes=[pltpu.VMEM((BM, D_FF), BF16)],
        ),
        out_shape=jax.ShapeDtypeStruct((N_ROWS * ROW_TILE, LANES), F32),
        compiler_params=pltpu.CompilerParams(dimension_semantics=("arbitrary",), vmem_limit_bytes=VMEM_LIMIT),
        name="moe_gmm",
    )(block_expert, n_used, xs, g.reshape(1, D), w_gate, w_up, w_down)


def _combine_kernel(dest_ref, x_ref, wts_ref, gf_ref, ys_hbm, *rest, final):
    out_refs, (buf0, buf1, sem) = rest[:-3], rest[-3:]

    def issue(j, carry):
        for u in range(DMA_UNROLL):
            t = j * DMA_UNROLL + u
            _row_copy(ys_hbm, dest_ref[0, 0, 2 * t], buf0, t, sem).start(priority=0)
            _row_copy(ys_hbm, dest_ref[0, 0, 2 * t + 1], buf1, t, sem).start(priority=1)
        return carry

    lax.fori_loop(0, TM // DMA_UNROLL, issue, 0)

    def drain(j, carry):
        for _ in range(DMA_UNROLL):
            _row_copy(ys_hbm, 0, buf0, 0, sem).wait()
            _row_copy(ys_hbm, 0, buf1, 0, sem).wait()
        return carry

    lax.fori_loop(0, TM // DMA_UNROLL, drain, 0)
    w = wts_ref[...]
    y = x_ref[...] + w[:, 0:1] * _from_token_major(buf0, TM) + w[:, 1:2] * _from_token_major(buf1, TM)
    if final:
        y = _rmsnorm(y, gf_ref[...])
        yp_ref, ysm_ref = out_refs
        is_sample = pl.program_id(0) >= N_PROMPT_TILES

        @pl.when(jnp.logical_not(is_sample))
        def _():
            yp_ref[...] = y

        @pl.when(is_sample)
        def _():
            ysm_ref[...] = y
    else:
        out_refs[0][...] = y


def _combine(x, dest, wts, ys, g_final, final):
    if final:
        out_specs = [pl.BlockSpec((TM, D), lambda i: (jnp.minimum(i, N_PROMPT_TILES - 1), 0)),
                     pl.BlockSpec((TM, D), lambda i: (0, 0))]
        out_shape = [jax.ShapeDtypeStruct((N_PROMPT_TOK, D), F32), jax.ShapeDtypeStruct((N_SAMPLE_TOK, D), F32)]
    else:
        out_specs = [pl.BlockSpec((TM, D), lambda i: (i, 0))]
        out_shape = [jax.ShapeDtypeStruct((N_TOK, D), F32)]
    outs = pl.pallas_call(
        functools.partial(_combine_kernel, final=final),
        grid=(N_TILES,),
        in_specs=[
            pl.BlockSpec((1, 1, 2 * TM), lambda i: (i, 0, 0), memory_space=pltpu.SMEM),
            pl.BlockSpec((TM, D), lambda i: (i, 0)),
            pl.BlockSpec((TM, LANES), lambda i: (i, 0)),
            _const_spec((1, D)),
            pl.BlockSpec(memory_space=pl.ANY),
        ],
        out_specs=out_specs,
        out_shape=out_shape,
        scratch_shapes=[pltpu.VMEM((TM * ROW_TILE, LANES), F32), pltpu.VMEM((TM * ROW_TILE, LANES), F32),
                        pltpu.SemaphoreType.DMA(())],
        compiler_params=pltpu.CompilerParams(dimension_semantics=("arbitrary",), vmem_limit_bytes=VMEM_LIMIT),
        name="moe_combine",
    )(dest.reshape(N_TILES, 1, 2 * TM), x, wts, g_final.reshape(1, D), ys)
    return outs if final else outs[0]


def _moe_ffn(x, g, w_router, layer, w_gate, w_up, w_down, g_final, final):
    meta, wts, counts = _router(x, g, w_router)
    counts = counts[0, :N_EXPERTS]
    blocks = (counts + BM - 1) // BM
    block_end = jnp.cumsum(blocks)
    seg_start = (block_end - blocks) * BM
    dest = seg_start[meta[:, 0:2]] + meta[:, 2:4]
    n_used = block_end[-1:]
    bidx = jnp.minimum(jnp.arange(N_BLOCKS, dtype=jnp.int32), n_used - 1)
    block_expert = jnp.sum(bidx[:, None] >= block_end[None, :], axis=1).astype(jnp.int32)
    n_used = n_used.astype(jnp.int32)
    xs = _dispatch(x, dest, (block_end - 1).astype(jnp.int32), blocks.astype(jnp.int32), n_used)
    ys = _gmm(xs, g, block_expert, n_used, layer, w_gate, w_up, w_down)
    return _combine(x, dest, wts, ys, g_final, final)


def kernel(x_prompt, x_sample, state_conv_b, state_conv_c, norm_mix, norm_ffn, norm_final, a_w_in, a_ln_g, a_ln_b, a_w_s, a_b_s, a_w_out, b_w_in, b_conv, b_w_out, c_w_pw1, c_b_pw1, c_dw, c_b_dw, c_ln_g, c_ln_b, c_w_pw2, c_b_pw2, f_w_gate, f_w_up, f_w_down, m_router, m_w_gate, m_w_up, m_w_down):
    last_tile = jnp.arange(N_SEQ) * TILES_PER_SEQ + TILES_PER_SEQ - 1
    fw = [w.astype(BF16) for w in (f_w_gate, f_w_up, f_w_down)]
    mw = [w.astype(BF16) for w in (m_w_gate, m_w_up, m_w_down)]

    x, v0 = _gmlp_mixer(x_prompt.reshape(N_PROMPT_TOK, D), x_sample.reshape(N_SAMPLE_TOK, D), 0,
                        norm_mix[0], a_w_in[0], a_ln_g[0], a_ln_b[0], a_w_s[0], a_b_s[0], a_w_out[0])
    x = _dense_ffn(x, norm_ffn[0], 0, *fw)
    x, tb_p, tb_s = _sconv_mixer(x, norm_mix[1], b_w_in[0], b_conv[0], b_w_out[0], state_conv_b[0])
    x = _moe_ffn(x, norm_ffn[1], m_router[0], 0, *mw, norm_final, False)
    x, tc_p, tc_s = _conf_mixer(x, norm_mix[2], c_w_pw1[0], c_b_pw1[0], c_dw[0], c_b_dw[0], c_ln_g[0], c_ln_b[0],
                                c_w_pw2[0], c_b_pw2[0], state_conv_c[0])
    x = _dense_ffn(x, norm_ffn[2], 1, *fw)
    x, v1 = _gmlp_mixer(x, x, N_PROMPT_TILES,
                        norm_mix[3], a_w_in[1], a_ln_g[1], a_ln_b[1], a_w_s[1], a_b_s[1], a_w_out[1])
    y_prompt, y_sample = _moe_ffn(x, norm_ffn[3], m_router[1], 1, *mw, norm_final, True)

    y_prompt = y_prompt.reshape(N_SEQ, SEQ, D)
    y_sample = y_sample.reshape(N_SEQ, SAMPLE_SEQ, D)
    nb = B_WIDTH - 1
    nc = C_WIDTH - 1
    new_b_p = tb_p[last_tile, B_HALO - nb:, :][None]
    new_b_s = tb_s[:, B_HALO - nb:, :][None]
    new_c_p = tc_p[last_tile, C_HALO - nc:, :][None]
    new_c_s = tc_s[:, C_HALO - nc:, :][None]
    new_v = jnp.stack([v0, v1]).reshape(2, N_SEQ, SAMPLE_SEQ, A_HALF)
    return (y_prompt, y_sample, new_b_p, new_b_s, new_c_p, new_c_s, new_v)
```

```python
import functools

import jax
import jax.numpy as jnp
from jax import lax
from jax.experimental import pallas as pl
from jax.experimental.pallas import tpu as pltpu

F32 = jnp.float32
BF16 = jnp.bfloat16

D = 1024
TM = 512
SEQ = 4096
SAMPLE_SEQ = 64
N_SEQ = 8
TILES_PER_SEQ = SEQ // TM
N_PROMPT_TOK = N_SEQ * SEQ
N_SAMPLE_TOK = N_SEQ * SAMPLE_SEQ
N_TOK = N_PROMPT_TOK + N_SAMPLE_TOK
N_PROMPT_TILES = N_PROMPT_TOK // TM
N_TILES = N_TOK // TM
assert N_SAMPLE_TOK == TM and N_TILES == N_PROMPT_TILES + 1

A_HALF = 2 * D
A_GROUPS = 8
A_HEAD = A_HALF // A_GROUPS
A_CHUNK = 128
B_WIDTH = 3
C_WIDTH = 31
B_HALO = 8
C_HALO = 32
D_FF = 2816
N_EXPERTS = 8
LANES = 128
SUBLANES = 8
MXU_COLS = 256
RMS_EPS = 1e-6
LN_EPS = 1e-5

BM = TM
N_ROWS = 2 * N_TOK + N_EXPERTS * BM
N_BLOCKS = N_ROWS // BM
FF_CHUNKS = ((0, 1024), (1024, 1024), (2048, 768))

VMEM_LIMIT = 56 * 1024 * 1024


def _const_spec(shape):
    return pl.BlockSpec(shape, lambda *_: (0,) * len(shape), pipeline_mode=pl.Buffered(1))


def _rmsnorm(x, g):
    return x * lax.rsqrt(jnp.mean(x * x, axis=-1, keepdims=True) + RMS_EPS) * g


def _layernorm(x, g, b):
    mu = jnp.mean(x, axis=-1, keepdims=True)
    xc = x - mu
    var = jnp.mean(xc * xc, axis=-1, keepdims=True)
    return xc * lax.rsqrt(var + LN_EPS) * g + b


def _dot(a, b):
    return jnp.dot(a, b, preferred_element_type=F32)


def _gmlp_kernel(xp_ref, xs_ref, g_ref, win_ref, lng_ref, lnb_ref, ws_ref, bs_ref, wout_ref,
                 o_ref, v_ref, vn_s, y_s):
    is_sample = pl.program_id(0) >= N_PROMPT_TILES
    x = jnp.where(is_sample, xs_ref[...], xp_ref[...])
    h = _rmsnorm(x, g_ref[...]).astype(BF16)
    v = jax.nn.gelu(_dot(h, win_ref[:, A_HALF:]))
    vn = _layernorm(v, lng_ref[...], lnb_ref[...])
    v_ref[...] = vn
    vn_s[...] = vn.astype(BF16)
    r = lax.broadcasted_iota(jnp.int32, (A_CHUNK, A_CHUNK), 0)
    c = lax.broadcasted_iota(jnp.int32, (A_CHUNK, A_CHUNK), 1)
    seg_shift = jnp.where(is_sample, 6, 7)
    mask = (r >= c) & ((r >> seg_shift) == (c >> seg_shift))
    for g in range(A_GROUPS):
        cols = slice(g * A_HEAD, (g + 1) * A_HEAD)
        u_g = jax.nn.gelu(_dot(h, win_ref[:, cols]))
        ws = jnp.where(mask, ws_ref[0, g], 0.0).astype(BF16)
        bias = jnp.concatenate([bs_ref[0, g]] * (A_HEAD // LANES), axis=1)
        for ch in range(TM // A_CHUNK):
            rows = slice(ch * A_CHUNK, (ch + 1) * A_CHUNK)
            s = _dot(ws, vn_s[rows, cols]) + bias
            y_s[rows, cols] = (u_g[rows] * s).astype(BF16)
    o_ref[...] = x + _dot(y_s[...], wout_ref[...])


def _gmlp_mixer(x_prompt, x_sample, sample_block, g, w_in, ln_g, ln_b, w_s, b_s, w_out):
    ws2 = jnp.stack([w_s, jnp.tile(w_s[:, :SAMPLE_SEQ, :SAMPLE_SEQ], (1, 2, 2))])
    b2 = jnp.stack([b_s, jnp.tile(b_s[:, :SAMPLE_SEQ], (1, 2))])
    b2 = jnp.broadcast_to(b2[..., None], (2, A_GROUPS, A_CHUNK, LANES))
    return pl.pallas_call(
        _gmlp_kernel,
        grid=(N_TILES,),
        in_specs=[
            pl.BlockSpec((TM, D), lambda i: (jnp.minimum(i, N_PROMPT_TILES - 1), 0)),
            pl.BlockSpec((TM, D), lambda i: (sample_block, 0)),
            _const_spec((1, D)),
            _const_spec((D, 2 * A_HALF)),
            _const_spec((1, A_HALF)),
            _const_spec((1, A_HALF)),
            pl.BlockSpec((1, A_GROUPS, A_CHUNK, A_CHUNK), lambda i: (i // N_PROMPT_TILES, 0, 0, 0)),
            pl.BlockSpec((1, A_GROUPS, A_CHUNK, LANES), lambda i: (i // N_PROMPT_TILES, 0, 0, 0)),
            _const_spec((A_HALF, D)),
        ],
        out_specs=[
            pl.BlockSpec((TM, D), lambda i: (i, 0)),
            pl.BlockSpec((TM, A_HALF), lambda i: (0, 0)),
        ],
        out_shape=[
            jax.ShapeDtypeStruct((N_TOK, D), F32),
            jax.ShapeDtypeStruct((N_SAMPLE_TOK, A_HALF), F32),
        ],
        scratch_shapes=[pltpu.VMEM((TM, A_HALF), BF16), pltpu.VMEM((TM, A_HALF), BF16)],
        compiler_params=pltpu.CompilerParams(dimension_semantics=("arbitrary",), vmem_limit_bytes=VMEM_LIMIT),
        name="gmlp_mixer",
    )(x_prompt, x_sample, g.reshape(1, D), w_in.astype(BF16), ln_g.reshape(1, A_HALF), ln_b.reshape(1, A_HALF),
      ws2, b2, w_out.astype(BF16))


def _sconv_kernel(x_ref, g_ref, win_ref, cw_ref, wout_ref, st_ref,
                  o_ref, tailp_ref, tails_ref, full_s, y_s):
    i = pl.program_id(0)
    is_sample = i >= N_PROMPT_TILES
    x = x_ref[...]
    h = _rmsnorm(x, g_ref[...]).astype(BF16)
    p = _dot(h, win_ref[...])
    bg = p[:, :D]
    u = p[:, D:2 * D] * p[:, 2 * D:]
    w = [cw_ref[k:k + 1, :] for k in range(B_WIDTH)]

    def conv(n):
        return sum(w[k] * full_s[pl.ds(B_HALO - (B_WIDTH - 1) + k, n), :] for k in range(B_WIDTH))

    @pl.when(jnp.logical_not(is_sample))
    def _():
        @pl.when(i % TILES_PER_SEQ == 0)
        def _():
            full_s[0:B_HALO, :] = jnp.zeros((B_HALO, D), F32)

        full_s[B_HALO:, :] = u
        y_s[...] = (bg * conv(TM)).astype(BF16)
        tail = full_s[TM:TM + B_HALO, :]
        tailp_ref[0] = tail
        full_s[0:B_HALO, :] = tail

    @pl.when(is_sample)
    def _():
        for b in range(N_SEQ):
            rows = slice(b * SAMPLE_SEQ, (b + 1) * SAMPLE_SEQ)
            full_s[0:B_HALO, :] = st_ref[b]
            full_s[B_HALO:B_HALO + SAMPLE_SEQ, :] = u[rows]
            y_s[rows, :] = (bg[rows] * conv(SAMPLE_SEQ)).astype(BF16)
            tails_ref[b] = full_s[SAMPLE_SEQ:SAMPLE_SEQ + B_HALO, :]
        tailp_ref[0] = jnp.zeros((B_HALO, D), F32)

    o_ref[...] = x + _dot(y_s[...], wout_ref[...])


def _sconv_mixer(x, g, w_in, w_conv, w_out, state):
    st = jnp.pad(state, ((0, 0), (B_HALO - (B_WIDTH - 1), 0), (0, 0)))
    return pl.pallas_call(
        _sconv_kernel,
        grid=(N_TILES,),
        in_specs=[
            pl.BlockSpec((TM, D), lambda i: (i, 0)),
            _const_spec((1, D)),
            _const_spec((D, 3 * D)),
            _const_spec((B_WIDTH, D)),
            _const_spec((D, D)),
            _const_spec((N_SEQ, B_HALO, D)),
        ],
        out_specs=[
            pl.BlockSpec((TM, D), lambda i: (i, 0)),
            pl.BlockSpec((1, B_HALO, D), lambda i: (i, 0, 0)),
            pl.BlockSpec((N_SEQ, B_HALO, D), lambda i: (0, 0, 0)),
        ],
        out_shape=[
            jax.ShapeDtypeStruct((N_TOK, D), F32),
            jax.ShapeDtypeStruct((N_TILES, B_HALO, D), F32),
            jax.ShapeDtypeStruct((N_SEQ, B_HALO, D), F32),
        ],
        scratch_shapes=[pltpu.VMEM((B_HALO + TM, D), F32), pltpu.VMEM((TM, D), BF16)],
        compiler_params=pltpu.CompilerParams(dimension_semantics=("arbitrary",), vmem_limit_bytes=VMEM_LIMIT),
        name="sconv_mixer",
    )(x, g.reshape(1, D), w_in.astype(BF16), w_conv, w_out.astype(BF16), st)


C_ROW_BLOCK = 32


def _conf_kernel(x_ref, g_ref, w1_ref, b1_ref, dw_ref, bdw_ref, lng_ref, lnb_ref, w2_ref, b2_ref, st_ref,
                 o_ref, tailp_ref, tails_ref, full_s, shift_s, conv_s, y_s):
    i = pl.program_id(0)
    is_sample = i >= N_PROMPT_TILES
    x = x_ref[...]
    h = _rmsnorm(x, g_ref[...]).astype(BF16)
    p = _dot(h, w1_ref[...]) + b1_ref[...]
    gl = p[:, :D] * jax.nn.sigmoid(p[:, D:])

    def conv(n, out_row0):
        n_in = C_HALO + n
        for r in range(1, SUBLANES):
            shift_s[r - 1, 0:n_in - SUBLANES, :] = full_s[r:r + n_in - SUBLANES, :]

        def block(rb, carry):
            base = pl.multiple_of(rb * C_ROW_BLOCK, C_ROW_BLOCK)
            acc = [jnp.zeros((SUBLANES, D), F32) for _ in range(C_ROW_BLOCK // SUBLANES)]
            for k in range(C_WIDTH):
                off = k + (C_HALO - (C_WIDTH - 1))
                row0 = base + (off // SUBLANES) * SUBLANES
                wk = dw_ref[k]
                for a in range(len(acc)):
                    rows = pl.ds(row0 + a * SUBLANES, SUBLANES)
                    tap = full_s[rows, :] if off % SUBLANES == 0 else shift_s[off % SUBLANES - 1, rows, :]
                    acc[a] = acc[a] + wk * tap
            conv_s[pl.ds(out_row0 + base, C_ROW_BLOCK), :] = jnp.concatenate(acc, axis=0)
            return carry

        lax.fori_loop(0, n // C_ROW_BLOCK, block, 0)

    @pl.when(jnp.logical_not(is_sample))
    def _():
        @pl.when(i % TILES_PER_SEQ == 0)
        def _():
            full_s[0:C_HALO, :] = jnp.zeros((C_HALO, D), F32)

        full_s[C_HALO:, :] = gl
        conv(TM, 0)
        tail = full_s[TM:TM + C_HALO, :]
        tailp_ref[0] = tail
        full_s[0:C_HALO, :] = tail

    @pl.when(is_sample)
    def _():
        for b in range(N_SEQ):
            full_s[0:C_HALO, :] = st_ref[b]
            full_s[C_HALO:C_HALO + SAMPLE_SEQ, :] = gl[b * SAMPLE_SEQ:(b + 1) * SAMPLE_SEQ]
            conv(SAMPLE_SEQ, b * SAMPLE_SEQ)
            tails_ref[b] = full_s[SAMPLE_SEQ:SAMPLE_SEQ + C_HALO, :]
        tailp_ref[0] = jnp.zeros((C_HALO, D), F32)

    y = _layernorm(conv_s[...] + bdw_ref[...], lng_ref[...], lnb_ref[...])
    y_s[...] = (y * jax.nn.sigmoid(y)).astype(BF16)
    o_ref[...] = x + _dot(y_s[...], w2_ref[...]) + b2_ref[...]


def _conf_mixer(x, g, w_pw1, b_pw1, w_dw, b_dw, ln_g, ln_b, w_pw2, b_pw2, state):
    st = jnp.pad(state, ((0, 0), (C_HALO - (C_WIDTH - 1), 0), (0, 0)))
    dw = jnp.broadcast_to(w_dw[:, None, :], (C_WIDTH, SUBLANES, D))
    return pl.pallas_call(
        _conf_kernel,
        grid=(N_TILES,),
        in_specs=[
            pl.BlockSpec((TM, D), lambda i: (i, 0)),
            _const_spec((1, D)),
            _const_spec((D, 2 * D)),
            _const_spec((1, 2 * D)),
            _const_spec((C_WIDTH, SUBLANES, D)),
            _const_spec((1, D)),
            _const_spec((1, D)),
            _const_spec((1, D)),
            _const_spec((D, D)),
            _const_spec((1, D)),
            _const_spec((N_SEQ, C_HALO, D)),
        ],
        out_specs=[
            pl.BlockSpec((TM, D), lambda i: (i, 0)),
            pl.BlockSpec((1, C_HALO, D), lambda i: (i, 0, 0)),
            pl.BlockSpec((N_SEQ, C_HALO, D), lambda i: (0, 0, 0)),
        ],
        out_shape=[
            jax.ShapeDtypeStruct((N_TOK, D), F32),
            jax.ShapeDtypeStruct((N_TILES, C_HALO, D), F32),
            jax.ShapeDtypeStruct((N_SEQ, C_HALO, D), F32),
        ],
        scratch_shapes=[
            pltpu.VMEM((C_HALO + TM, D), F32),
            pltpu.VMEM((SUBLANES - 1, C_HALO + TM, D), F32),
            pltpu.VMEM((TM, D), F32),
            pltpu.VMEM((TM, D), BF16),
        ],
        compiler_params=pltpu.CompilerParams(dimension_semantics=("arbitrary",), vmem_limit_bytes=VMEM_LIMIT),
        name="conf_mixer",
    )(x, g.reshape(1, D), w_pw1.astype(BF16), b_pw1.reshape(1, 2 * D), dw, b_dw.reshape(1, D),
      ln_g.reshape(1, D), ln_b.reshape(1, D), w_pw2.astype(BF16), b_pw2.reshape(1, D), st)


def _swiglu_hidden(h, wg_ref, wu_ref, hid_s, lead=()):
    for c0, cn in FF_CHUNKS:
        a = _dot(h, wg_ref[lead + (slice(None), slice(c0, c0 + cn))])
        b = _dot(h, wu_ref[lead + (slice(None), slice(c0, c0 + cn))])
        hid_s[:, c0:c0 + cn] = (a * jax.nn.sigmoid(a) * b).astype(BF16)


def _ffn_kernel(x_ref, g_ref, wg_ref, wu_ref, wd_ref, o_ref, hid_s):
    x = x_ref[...]
    h = _rmsnorm(x, g_ref[...]).astype(BF16)
    _swiglu_hidden(h, wg_ref, wu_ref, hid_s, lead=(0,))
    o_ref[...] = x + _dot(hid_s[...], wd_ref[0])


def _dense_ffn(x, g, layer, w_gate, w_up, w_down):
    wspec = lambda shape: pl.BlockSpec((1,) + shape, lambda i: (layer, 0, 0), pipeline_mode=pl.Buffered(1))
    return pl.pallas_call(
        _ffn_kernel,
        grid=(N_TILES,),
        in_specs=[
            pl.BlockSpec((TM, D), lambda i: (i, 0)),
            _const_spec((1, D)),
            wspec((D, D_FF)),
            wspec((D, D_FF)),
            wspec((D_FF, D)),
        ],
        out_specs=pl.BlockSpec((TM, D), lambda i: (i, 0)),
        out_shape=jax.ShapeDtypeStruct((N_TOK, D), F32),
        scratch_shapes=[pltpu.VMEM((TM, D_FF), BF16)],
        compiler_params=pltpu.CompilerParams(dimension_semantics=("arbitrary",), vmem_limit_bytes=VMEM_LIMIT),
        name="dense_ffn",
    )(x, g.reshape(1, D), w_gate, w_up, w_down)


def _router_kernel(x_ref, g_ref, wr_ref, meta_ref, wts_ref, cnt_ref, run_s):
    i = pl.program_id(0)

    @pl.when(i == 0)
    def _():
        run_s[...] = jnp.zeros((1, LANES), F32)

    h = _rmsnorm(x_ref[...], g_ref[...])
    logits = _dot(h.astype(BF16), wr_ref[...].astype(BF16))
    lane = lax.broadcasted_iota(jnp.int32, (TM, LANES), 1)
    lane_f = lane.astype(F32)
    neg = jnp.float32(-jnp.inf)
    logits = jnp.where(lane < N_EXPERTS, logits, neg)
    l1 = jnp.max(logits, axis=-1, keepdims=True)
    e1 = jnp.min(jnp.where(logits == l1, lane_f, float(LANES)), axis=-1, keepdims=True).astype(jnp.int32)
    rest = jnp.where(lane == e1, neg, logits)
    l2 = jnp.max(rest, axis=-1, keepdims=True)
    e2 = jnp.min(jnp.where(rest == l2, lane_f, float(LANES)), axis=-1, keepdims=True).astype(jnp.int32)
    t = jnp.exp(l2 - l1)
    w1 = 1.0 / (1.0 + t)
    w2 = t * w1
    sel = jnp.logical_or(lane == e1, lane == e2)
    onehot = jnp.where(sel, 1.0, 0.0)
    r = lax.broadcasted_iota(jnp.int32, (TM, TM), 0)
    c = lax.broadcasted_iota(jnp.int32, (TM, TM), 1)
    below = jnp.where(r > c, 1.0, 0.0).astype(BF16)
    before = _dot(below, onehot.astype(BF16)) + run_s[...]
    r1 = jnp.sum(jnp.where(lane == e1, before, 0.0), axis=-1, keepdims=True).astype(jnp.int32)
    r2 = jnp.sum(jnp.where(lane == e2, before, 0.0), axis=-1, keepdims=True).astype(jnp.int32)
    run_s[...] = run_s[...] + jnp.sum(onehot, axis=0, keepdims=True)
    meta = jnp.where(lane == 0, e1, jnp.where(lane == 1, e2, jnp.where(lane == 2, r1, jnp.where(lane == 3, r2, 0))))
    meta_ref[...] = meta
    wts_ref[...] = jnp.where(lane == 0, w1, jnp.where(lane == 1, w2, 0.0))
    cnt_ref[...] = run_s[...].astype(jnp.int32)


def _router(x, g, w_router):
    wr = jnp.pad(w_router, ((0, 0), (0, LANES - N_EXPERTS)))
    return pl.pallas_call(
        _router_kernel,
        grid=(N_TILES,),
        in_specs=[
            pl.BlockSpec((TM, D), lambda i: (i, 0)),
            _const_spec((1, D)),
            _const_spec((D, LANES)),
        ],
        out_specs=[
            pl.BlockSpec((TM, LANES), lambda i: (i, 0)),
            pl.BlockSpec((TM, LANES), lambda i: (i, 0)),
            pl.BlockSpec((1, LANES), lambda i: (0, 0)),
        ],
        out_shape=[
            jax.ShapeDtypeStruct((N_TOK, LANES), jnp.int32),
            jax.ShapeDtypeStruct((N_TOK, LANES), F32),
            jax.ShapeDtypeStruct((1, LANES), jnp.int32),
        ],
        scratch_shapes=[pltpu.VMEM((1, LANES), F32)],
        compiler_params=pltpu.CompilerParams(dimension_semantics=("arbitrary",), vmem_limit_bytes=VMEM_LIMIT),
        name="moe_router",
    )(x, g.reshape(1, D), wr)


ROW_TILE = D // LANES
assert ROW_TILE == SUBLANES


def _from_token_major(ref, n):
    return jnp.concatenate([ref[pl.ds(s, n, stride=ROW_TILE), :] for s in range(ROW_TILE)], axis=-1)


def _to_token_major(ref, val, n):
    for s in range(ROW_TILE):
        ref[pl.ds(s, n, stride=ROW_TILE), :] = val[:, s * LANES:(s + 1) * LANES]


def _row_copy(src, src_row, dst, dst_row, sem):
    s0 = pl.multiple_of(src_row * ROW_TILE, ROW_TILE)
    d0 = pl.multiple_of(dst_row * ROW_TILE, ROW_TILE)
    return pltpu.make_async_copy(src.at[pl.ds(s0, ROW_TILE), :], dst.at[pl.ds(d0, ROW_TILE), :], sem)


DMA_UNROLL = 8


def _dispatch_kernel(last_ref, nblk_ref, nu_ref, dest_ref, x_ref, xs_hbm, tok_s, sem, zsem):
    def zero_block(b):
        return pltpu.make_async_copy(tok_s, xs_hbm.at[pl.ds(b * (BM * ROW_TILE), BM * ROW_TILE), :], zsem)

    @pl.when(pl.program_id(0) == 0)
    def _():
        tok_s[...] = jnp.zeros((BM * ROW_TILE, LANES), F32)
        for start in (True, False):
            for e in range(N_EXPERTS):
                tail = N_BLOCKS - 1 - e
                for cond, b in ((nblk_ref[e] > 0, last_ref[e]), (tail >= nu_ref[0], tail)):
                    cp = zero_block(b)
                    pl.when(cond)(cp.start if start else cp.wait)

    _to_token_major(tok_s, x_ref[...], TM)

    def issue(j, carry):
        for u in range(DMA_UNROLL):
            t = j * DMA_UNROLL + u
            _row_copy(tok_s, t, xs_hbm, dest_ref[0, 0, 2 * t], sem).start(priority=0)
            _row_copy(tok_s, t, xs_hbm, dest_ref[0, 0, 2 * t + 1], sem).start(priority=1)
        return carry

    lax.fori_loop(0, TM // DMA_UNROLL, issue, 0)

    def drain(j, carry):
        for _ in range(2 * DMA_UNROLL):
            _row_copy(tok_s, 0, xs_hbm, 0, sem).wait()
        return carry

    lax.fori_loop(0, TM // DMA_UNROLL, drain, 0)


def _dispatch(x, dest, last_block, n_blocks, n_used):
    return pl.pallas_call(
        _dispatch_kernel,
        grid_spec=pltpu.PrefetchScalarGridSpec(
            num_scalar_prefetch=3,
            grid=(N_TILES,),
            in_specs=[
                pl.BlockSpec((1, 1, 2 * TM), lambda i, *_: (i, 0, 0), memory_space=pltpu.SMEM),
                pl.BlockSpec((TM, D), lambda i, *_: (i, 0)),
            ],
            out_specs=pl.BlockSpec(memory_space=pl.ANY),
            scratch_shapes=[pltpu.VMEM((TM * ROW_TILE, LANES), F32), pltpu.SemaphoreType.DMA(()),
                            pltpu.SemaphoreType.DMA(())],
        ),
        out_shape=jax.ShapeDtypeStruct((N_ROWS * ROW_TILE, LANES), F32),
        compiler_params=pltpu.CompilerParams(dimension_semantics=("arbitrary",), vmem_limit_bytes=VMEM_LIMIT),
        name="moe_dispatch",
    )(last_block, n_blocks, n_used, dest.reshape(N_TILES, 1, 2 * TM), x)


def _gmm_kernel(be_ref, nu_ref, xs_ref, g_ref, wg_ref, wu_ref, wd_ref, ys_ref, hid_s):
    used = pl.program_id(0) < nu_ref[0]

    @pl.when(used)
    def _():
        h = _rmsnorm(_from_token_major(xs_ref, BM), g_ref[...]).astype(BF16)
        _swiglu_hidden(h, wg_ref, wu_ref, hid_s, lead=(0, 0))
        _to_token_major(ys_ref, _dot(hid_s[...], wd_ref[0, 0]), BM)

    @pl.when(jnp.logical_not(used))
    def _():
        ys_ref[...] = jnp.zeros((BM * ROW_TILE, LANES), F32)


def _gmm(xs, g, block_expert, n_used, layer, w_gate, w_up, w_down):
    row_map = lambda i, be, nu: (jnp.minimum(i, nu[0] - 1), 0)
    wspec = lambda shape: pl.BlockSpec((1, 1) + shape, lambda i, be, nu: (layer, be[i], 0, 0))
    return pl.pallas_call(
        _gmm_kernel,
        grid_spec=pltpu.PrefetchScalarGridSpec(
            num_scalar_prefetch=2,
            grid=(N_BLOCKS,),
            in_specs=[
                pl.BlockSpec((BM * ROW_TILE, LANES), row_map),
                pl.BlockSpec((1, D), lambda i, be, nu: (0, 0)),
                wspec((D, D_FF)),
                wspec((D, D_FF)),
                wspec((D_FF, D)),
            ],
            out_specs=pl.BlockSpec((BM * ROW_TILE, LANES), lambda i, be, nu: (i, 0)),
            scratch_shapes=[pltpu.VMEM((BM, D_FF), BF16)],
        ),
        out_shape=jax.ShapeDtypeStruct((N_ROWS * ROW_TILE, LANES), F32),
        compiler_params=pltpu.CompilerParams(dimension_semantics=("arbitrary",), vmem_limit_bytes=VMEM_LIMIT),
        name="moe_gmm",
    )(block_expert, n_used, xs, g.reshape(1, D), w_gate, w_up, w_down)


def _combine_kernel(dest_ref, next_ref, x_ref, wts_ref, gf_ref, ys_hbm, *rest, final):
    out_refs, (buf0, buf1, sem) = rest[:-3], rest[-3:]
    i = pl.program_id(0)
    slot = i % 2

    def gather(idx_ref, s):
        def issue(j, carry):
            for u in range(DMA_UNROLL):
                t = j * DMA_UNROLL + u
                _row_copy(ys_hbm, idx_ref[0, 0, 2 * t], buf0.at[s], t, sem.at[s]).start(priority=0)
                _row_copy(ys_hbm, idx_ref[0, 0, 2 * t + 1], buf1.at[s], t, sem.at[s]).start(priority=1)
            return carry

        lax.fori_loop(0, TM // DMA_UNROLL, issue, 0)

    pl.when(i == 0)(lambda: gather(dest_ref, slot))
    pl.when(i + 1 < N_TILES)(lambda: gather(next_ref, 1 - slot))

    def drain(j, carry):
        for _ in range(DMA_UNROLL):
            _row_copy(ys_hbm, 0, buf0.at[slot], 0, sem.at[slot]).wait()
            _row_copy(ys_hbm, 0, buf1.at[slot], 0, sem.at[slot]).wait()
        return carry

    lax.fori_loop(0, TM // DMA_UNROLL, drain, 0)
    w = wts_ref[...]
    y = (x_ref[...] + w[:, 0:1] * _from_token_major(buf0.at[slot], TM)
         + w[:, 1:2] * _from_token_major(buf1.at[slot], TM))
    if final:
        y = _rmsnorm(y, gf_ref[...])
        yp_ref, ysm_ref = out_refs
        is_sample = pl.program_id(0) >= N_PROMPT_TILES

        @pl.when(jnp.logical_not(is_sample))
        def _():
            yp_ref[...] = y

        @pl.when(is_sample)
        def _():
            ysm_ref[...] = y
    else:
        out_refs[0][...] = y


def _combine(x, dest, wts, ys, g_final, final):
    if final:
        out_specs = [pl.BlockSpec((TM, D), lambda i: (jnp.minimum(i, N_PROMPT_TILES - 1), 0)),
                     pl.BlockSpec((TM, D), lambda i: (0, 0))]
        out_shape = [jax.ShapeDtypeStruct((N_PROMPT_TOK, D), F32), jax.ShapeDtypeStruct((N_SAMPLE_TOK, D), F32)]
    else:
        out_specs = [pl.BlockSpec((TM, D), lambda i: (i, 0))]
        out_shape = [jax.ShapeDtypeStruct((N_TOK, D), F32)]
    dest3 = dest.reshape(N_TILES, 1, 2 * TM)
    outs = pl.pallas_call(
        functools.partial(_combine_kernel, final=final),
        grid=(N_TILES,),
        in_specs=[
            pl.BlockSpec((1, 1, 2 * TM), lambda i: (i, 0, 0), memory_space=pltpu.SMEM),
            pl.BlockSpec((1, 1, 2 * TM), lambda i: (jnp.minimum(i + 1, N_TILES - 1), 0, 0), memory_space=pltpu.SMEM),
            pl.BlockSpec((TM, D), lambda i: (i, 0)),
            pl.BlockSpec((TM, LANES), lambda i: (i, 0)),
            _const_spec((1, D)),
            pl.BlockSpec(memory_space=pl.ANY),
        ],
        out_specs=out_specs,
        out_shape=out_shape,
        scratch_shapes=[pltpu.VMEM((2, TM * ROW_TILE, LANES), F32), pltpu.VMEM((2, TM * ROW_TILE, LANES), F32),
                        pltpu.SemaphoreType.DMA((2,))],
        compiler_params=pltpu.CompilerParams(dimension_semantics=("arbitrary",), vmem_limit_bytes=VMEM_LIMIT),
        name="moe_combine",
    )(dest3, dest3, x, wts, g_final.reshape(1, D), ys)
    return outs if final else outs[0]


def _moe_ffn(x, g, w_router, layer, w_gate, w_up, w_down, g_final, final):
    meta, wts, counts = _router(x, g, w_router)
    counts = counts[0, :N_EXPERTS]
    blocks = (counts + BM - 1) // BM
    block_end = jnp.cumsum(blocks)
    seg_start = (block_end - blocks) * BM
    dest = seg_start[meta[:, 0:2]] + meta[:, 2:4]
    n_used = block_end[-1:]
    bidx = jnp.minimum(jnp.arange(N_BLOCKS, dtype=jnp.int32), n_used - 1)
    block_expert = jnp.sum(bidx[:, None] >= block_end[None, :], axis=1).astype(jnp.int32)
    n_used = n_used.astype(jnp.int32)
    xs = _dispatch(x, dest, (block_end - 1).astype(jnp.int32), blocks.astype(jnp.int32), n_used)
    ys = _gmm(xs, g, block_expert, n_used, layer, w_gate, w_up, w_down)
    return _combine(x, dest, wts, ys, g_final, final)


def kernel(x_prompt, x_sample, state_conv_b, state_conv_c, norm_mix, norm_ffn, norm_final, a_w_in, a_ln_g, a_ln_b, a_w_s, a_b_s, a_w_out, b_w_in, b_conv, b_w_out, c_w_pw1, c_b_pw1, c_dw, c_b_dw, c_ln_g, c_ln_b, c_w_pw2, c_b_pw2, f_w_gate, f_w_up, f_w_down, m_router, m_w_gate, m_w_up, m_w_down):
    last_tile = jnp.arange(N_SEQ) * TILES_PER_SEQ + TILES_PER_SEQ - 1
    fw = [w.astype(BF16) for w in (f_w_gate, f_w_up, f_w_down)]
    mw = [w.astype(BF16) for w in (m_w_gate, m_w_up, m_w_down)]

    x, v0 = _gmlp_mixer(x_prompt.reshape(N_PROMPT_TOK, D), x_sample.reshape(N_SAMPLE_TOK, D), 0,
                        norm_mix[0], a_w_in[0], a_ln_g[0], a_ln_b[0], a_w_s[0], a_b_s[0], a_w_out[0])
    x = _dense_ffn(x, norm_ffn[0], 0, *fw)
    x, tb_p, tb_s = _sconv_mixer(x, norm_mix[1], b_w_in[0], b_conv[0], b_w_out[0], state_conv_b[0])
    x = _moe_ffn(x, norm_ffn[1], m_router[0], 0, *mw, norm_final, False)
    x, tc_p, tc_s = _conf_mixer(x, norm_mix[2], c_w_pw1[0], c_b_pw1[0], c_dw[0], c_b_dw[0], c_ln_g[0], c_ln_b[0],
                                c_w_pw2[0], c_b_pw2[0], state_conv_c[0])
    x = _dense_ffn(x, norm_ffn[2], 1, *fw)
    x, v1 = _gmlp_mixer(x, x, N_PROMPT_TILES,
                        norm_mix[3], a_w_in[1], a_ln_g[1], a_ln_b[1], a_w_s[1], a_b_s[1], a_w_out[1])
    y_prompt, y_sample = _moe_ffn(x, norm_ffn[3], m_router[1], 1, *mw, norm_final, True)

    y_prompt = y_prompt.reshape(N_SEQ, SEQ, D)
    y_sample = y_sample.reshape(N_SEQ, SAMPLE_SEQ, D)
    nb = B_WIDTH - 1
    nc = C_WIDTH - 1
    new_b_p = tb_p[last_tile, B_HALO - nb:, :][None]
    new_b_s = tb_s[:, B_HALO - nb:, :][None]
    new_c_p = tc_p[last_tile, C_HALO - nc:, :][None]
    new_c_s = tc_s[:, C_HALO - nc:, :][None]
    new_v = jnp.stack([v0, v1]).reshape(2, N_SEQ, SAMPLE_SEQ, A_HALF)
    return (y_prompt, y_sample, new_b_p, new_b_s, new_c_p, new_c_s, new_v)
```

```python
import functools

import jax
import jax.numpy as jnp
from jax import lax
from jax.experimental import pallas as pl
from jax.experimental.pallas import tpu as pltpu

F32 = jnp.float32
BF16 = jnp.bfloat16

D = 1024
TM = 512
SEQ = 4096
SAMPLE_SEQ = 64
N_SEQ = 8
TILES_PER_SEQ = SEQ // TM
N_PROMPT_TOK = N_SEQ * SEQ
N_SAMPLE_TOK = N_SEQ * SAMPLE_SEQ
N_TOK = N_PROMPT_TOK + N_SAMPLE_TOK
N_PROMPT_TILES = N_PROMPT_TOK // TM
N_TILES = N_TOK // TM
assert N_SAMPLE_TOK == TM and N_TILES == N_PROMPT_TILES + 1

A_HALF = 2 * D
A_GROUPS = 8
A_HEAD = A_HALF // A_GROUPS
A_CHUNK = 128
B_WIDTH = 3
C_WIDTH = 31
B_HALO = 8
C_HALO = 32
D_FF = 2816
N_EXPERTS = 8
LANES = 128
SUBLANES = 8
MXU_COLS = 256
RMS_EPS = 1e-6
LN_EPS = 1e-5

BM = TM
N_ROWS = 2 * N_TOK + N_EXPERTS * BM
N_BLOCKS = N_ROWS // BM
FF_CHUNKS = ((0, 1024), (1024, 1024), (2048, 768))

VMEM_LIMIT = 56 * 1024 * 1024


def _const_spec(shape):
    return pl.BlockSpec(shape, lambda *_: (0,) * len(shape), pipeline_mode=pl.Buffered(1))


def _rmsnorm(x, g):
    return x * lax.rsqrt(jnp.mean(x * x, axis=-1, keepdims=True) + RMS_EPS) * g


def _layernorm(x, g, b):
    mu = jnp.mean(x, axis=-1, keepdims=True)
    xc = x - mu
    var = jnp.mean(xc * xc, axis=-1, keepdims=True)
    return xc * lax.rsqrt(var + LN_EPS) * g + b


def _dot(a, b):
    return jnp.dot(a, b, preferred_element_type=F32)


def _gmlp_kernel(xp_ref, xs_ref, g_ref, win_ref, lng_ref, lnb_ref, ws_ref, bs_ref, wout_ref,
                 o_ref, v_ref, vn_s, y_s):
    is_sample = pl.program_id(0) >= N_PROMPT_TILES
    x = jnp.where(is_sample, xs_ref[...], xp_ref[...])
    h = _rmsnorm(x, g_ref[...]).astype(BF16)
    v = jax.nn.gelu(_dot(h, win_ref[:, A_HALF:]))
    vn = _layernorm(v, lng_ref[...], lnb_ref[...])
    v_ref[...] = vn
    vn_s[...] = vn.astype(BF16)
    r = lax.broadcasted_iota(jnp.int32, (A_CHUNK, A_CHUNK), 0)
    c = lax.broadcasted_iota(jnp.int32, (A_CHUNK, A_CHUNK), 1)
    seg_shift = jnp.where(is_sample, 6, 7)
    mask = (r >= c) & ((r >> seg_shift) == (c >> seg_shift))
    for g in range(A_GROUPS):
        cols = slice(g * A_HEAD, (g + 1) * A_HEAD)
        u_g = jax.nn.gelu(_dot(h, win_ref[:, cols]))
        ws = jnp.where(mask, ws_ref[0, g], 0.0).astype(BF16)
        bias = jnp.concatenate([bs_ref[0, g]] * (A_HEAD // LANES), axis=1)
        for ch in range(TM // A_CHUNK):
            rows = slice(ch * A_CHUNK, (ch + 1) * A_CHUNK)
            s = _dot(ws, vn_s[rows, cols]) + bias
            y_s[rows, cols] = (u_g[rows] * s).astype(BF16)
    o_ref[...] = x + _dot(y_s[...], wout_ref[...])


def _gmlp_mixer(x_prompt, x_sample, sample_block, g, w_in, ln_g, ln_b, w_s, b_s, w_out):
    ws2 = jnp.stack([w_s, jnp.tile(w_s[:, :SAMPLE_SEQ, :SAMPLE_SEQ], (1, 2, 2))])
    b2 = jnp.stack([b_s, jnp.tile(b_s[:, :SAMPLE_SEQ], (1, 2))])
    b2 = jnp.broadcast_to(b2[..., None], (2, A_GROUPS, A_CHUNK, LANES))
    return pl.pallas_call(
        _gmlp_kernel,
        grid=(N_TILES,),
        in_specs=[
            pl.BlockSpec((TM, D), lambda i: (jnp.minimum(i, N_PROMPT_TILES - 1), 0)),
            pl.BlockSpec((TM, D), lambda i: (sample_block, 0)),
            _const_spec((1, D)),
            _const_spec((D, 2 * A_HALF)),
            _const_spec((1, A_HALF)),
            _const_spec((1, A_HALF)),
            pl.BlockSpec((1, A_GROUPS, A_CHUNK, A_CHUNK), lambda i: (i // N_PROMPT_TILES, 0, 0, 0)),
            pl.BlockSpec((1, A_GROUPS, A_CHUNK, LANES), lambda i: (i // N_PROMPT_TILES, 0, 0, 0)),
            _const_spec((A_HALF, D)),
        ],
        out_specs=[
            pl.BlockSpec((TM, D), lambda i: (i, 0)),
            pl.BlockSpec((TM, A_HALF), lambda i: (0, 0)),
        ],
        out_shape=[
            jax.ShapeDtypeStruct((N_TOK, D), F32),
            jax.ShapeDtypeStruct((N_SAMPLE_TOK, A_HALF), F32),
        ],
        scratch_shapes=[pltpu.VMEM((TM, A_HALF), BF16), pltpu.VMEM((TM, A_HALF), BF16)],
        compiler_params=pltpu.CompilerParams(dimension_semantics=("arbitrary",), vmem_limit_bytes=VMEM_LIMIT),
        name="gmlp_mixer",
    )(x_prompt, x_sample, g.reshape(1, D), w_in.astype(BF16), ln_g.reshape(1, A_HALF), ln_b.reshape(1, A_HALF),
      ws2, b2, w_out.astype(BF16))


def _sconv_kernel(x_ref, g_ref, win_ref, cw_ref, wout_ref, st_ref,
                  o_ref, tailp_ref, tails_ref, full_s, y_s):
    i = pl.program_id(0)
    is_sample = i >= N_PROMPT_TILES
    x = x_ref[...]
    h = _rmsnorm(x, g_ref[...]).astype(BF16)
    p = _dot(h, win_ref[...])
    bg = p[:, :D]
    u = p[:, D:2 * D] * p[:, 2 * D:]
    w = [cw_ref[k:k + 1, :] for k in range(B_WIDTH)]

    def conv(n):
        return sum(w[k] * full_s[pl.ds(B_HALO - (B_WIDTH - 1) + k, n), :] for k in range(B_WIDTH))

    @pl.when(jnp.logical_not(is_sample))
    def _():
        @pl.when(i % TILES_PER_SEQ == 0)
        def _():
            full_s[0:B_HALO, :] = jnp.zeros((B_HALO, D), F32)

        full_s[B_HALO:, :] = u
        y_s[...] = (bg * conv(TM)).astype(BF16)
        tail = full_s[TM:TM + B_HALO, :]
        tailp_ref[0] = tail
        full_s[0:B_HALO, :] = tail

    @pl.when(is_sample)
    def _():
        for b in range(N_SEQ):
            rows = slice(b * SAMPLE_SEQ, (b + 1) * SAMPLE_SEQ)
            full_s[0:B_HALO, :] = st_ref[b]
            full_s[B_HALO:B_HALO + SAMPLE_SEQ, :] = u[rows]
            y_s[rows, :] = (bg[rows] * conv(SAMPLE_SEQ)).astype(BF16)
            tails_ref[b] = full_s[SAMPLE_SEQ:SAMPLE_SEQ + B_HALO, :]
        tailp_ref[0] = jnp.zeros((B_HALO, D), F32)

    o_ref[...] = x + _dot(y_s[...], wout_ref[...])


def _sconv_mixer(x, g, w_in, w_conv, w_out, state):
    st = jnp.pad(state, ((0, 0), (B_HALO - (B_WIDTH - 1), 0), (0, 0)))
    return pl.pallas_call(
        _sconv_kernel,
        grid=(N_TILES,),
        in_specs=[
            pl.BlockSpec((TM, D), lambda i: (i, 0)),
            _const_spec((1, D)),
            _const_spec((D, 3 * D)),
            _const_spec((B_WIDTH, D)),
            _const_spec((D, D)),
            _const_spec((N_SEQ, B_HALO, D)),
        ],
        out_specs=[
            pl.BlockSpec((TM, D), lambda i: (i, 0)),
            pl.BlockSpec((1, B_HALO, D), lambda i: (i, 0, 0)),
            pl.BlockSpec((N_SEQ, B_HALO, D), lambda i: (0, 0, 0)),
        ],
        out_shape=[
            jax.ShapeDtypeStruct((N_TOK, D), F32),
            jax.ShapeDtypeStruct((N_TILES, B_HALO, D), F32),
            jax.ShapeDtypeStruct((N_SEQ, B_HALO, D), F32),
        ],
        scratch_shapes=[pltpu.VMEM((B_HALO + TM, D), F32), pltpu.VMEM((TM, D), BF16)],
        compiler_params=pltpu.CompilerParams(dimension_semantics=("arbitrary",), vmem_limit_bytes=VMEM_LIMIT),
        name="sconv_mixer",
    )(x, g.reshape(1, D), w_in.astype(BF16), w_conv, w_out.astype(BF16), st)


C_ROW_BLOCK = 32


def _conf_kernel(x_ref, g_ref, w1_ref, b1_ref, dw_ref, bdw_ref, lng_ref, lnb_ref, w2_ref, b2_ref, st_ref,
                 o_ref, tailp_ref, tails_ref, full_s, shift_s, conv_s, y_s):
    i = pl.program_id(0)
    is_sample = i >= N_PROMPT_TILES
    x = x_ref[...]
    h = _rmsnorm(x, g_ref[...]).astype(BF16)
    p = _dot(h, w1_ref[...]) + b1_ref[...]
    gl = p[:, :D] * jax.nn.sigmoid(p[:, D:])

    def conv(n, out_row0):
        n_in = C_HALO + n
        for r in range(1, SUBLANES):
            shift_s[r - 1, 0:n_in - SUBLANES, :] = full_s[r:r + n_in - SUBLANES, :]

        def block(rb, carry):
            base = pl.multiple_of(rb * C_ROW_BLOCK, C_ROW_BLOCK)
            acc = [jnp.zeros((SUBLANES, D), F32) for _ in range(C_ROW_BLOCK // SUBLANES)]
            for k in range(C_WIDTH):
                off = k + (C_HALO - (C_WIDTH - 1))
                row0 = base + (off // SUBLANES) * SUBLANES
                wk = dw_ref[k]
                for a in range(len(acc)):
                    rows = pl.ds(row0 + a * SUBLANES, SUBLANES)
                    tap = full_s[rows, :] if off % SUBLANES == 0 else shift_s[off % SUBLANES - 1, rows, :]
                    acc[a] = acc[a] + wk * tap
            conv_s[pl.ds(out_row0 + base, C_ROW_BLOCK), :] = jnp.concatenate(acc, axis=0)
            return carry

        lax.fori_loop(0, n // C_ROW_BLOCK, block, 0)

    @pl.when(jnp.logical_not(is_sample))
    def _():
        @pl.when(i % TILES_PER_SEQ == 0)
        def _():
            full_s[0:C_HALO, :] = jnp.zeros((C_HALO, D), F32)

        full_s[C_HALO:, :] = gl
        conv(TM, 0)
        tail = full_s[TM:TM + C_HALO, :]
        tailp_ref[0] = tail
        full_s[0:C_HALO, :] = tail

    @pl.when(is_sample)
    def _():
        for b in range(N_SEQ):
            full_s[0:C_HALO, :] = st_ref[b]
            full_s[C_HALO:C_HALO + SAMPLE_SEQ, :] = gl[b * SAMPLE_SEQ:(b + 1) * SAMPLE_SEQ]
            conv(SAMPLE_SEQ, b * SAMPLE_SEQ)
            tails_ref[b] = full_s[SAMPLE_SEQ:SAMPLE_SEQ + C_HALO, :]
        tailp_ref[0] = jnp.zeros((C_HALO, D), F32)

    y = _layernorm(conv_s[...] + bdw_ref[...], lng_ref[...], lnb_ref[...])
    y_s[...] = (y * jax.nn.sigmoid(y)).astype(BF16)
    o_ref[...] = x + _dot(y_s[...], w2_ref[...]) + b2_ref[...]


def _conf_mixer(x, g, w_pw1, b_pw1, w_dw, b_dw, ln_g, ln_b, w_pw2, b_pw2, state):
    st = jnp.pad(state, ((0, 0), (C_HALO - (C_WIDTH - 1), 0), (0, 0)))
    dw = jnp.broadcast_to(w_dw[:, None, :], (C_WIDTH, SUBLANES, D))
    return pl.pallas_call(
        _conf_kernel,
        grid=(N_TILES,),
        in_specs=[
            pl.BlockSpec((TM, D), lambda i: (i, 0)),
            _const_spec((1, D)),
            _const_spec((D, 2 * D)),
            _const_spec((1, 2 * D)),
            _const_spec((C_WIDTH, SUBLANES, D)),
            _const_spec((1, D)),
            _const_spec((1, D)),
            _const_spec((1, D)),
            _const_spec((D, D)),
            _const_spec((1, D)),
            _const_spec((N_SEQ, C_HALO, D)),
        ],
        out_specs=[
            pl.BlockSpec((TM, D), lambda i: (i, 0)),
            pl.BlockSpec((1, C_HALO, D), lambda i: (i, 0, 0)),
            pl.BlockSpec((N_SEQ, C_HALO, D), lambda i: (0, 0, 0)),
        ],
        out_shape=[
            jax.ShapeDtypeStruct((N_TOK, D), F32),
            jax.ShapeDtypeStruct((N_TILES, C_HALO, D), F32),
            jax.ShapeDtypeStruct((N_SEQ, C_HALO, D), F32),
        ],
        scratch_shapes=[
            pltpu.VMEM((C_HALO + TM, D), F32),
            pltpu.VMEM((SUBLANES - 1, C_HALO + TM, D), F32),
            pltpu.VMEM((TM, D), F32),
            pltpu.VMEM((TM, D), BF16),
        ],
        compiler_params=pltpu.CompilerParams(dimension_semantics=("arbitrary",), vmem_limit_bytes=VMEM_LIMIT),
        name="conf_mixer",
    )(x, g.reshape(1, D), w_pw1.astype(BF16), b_pw1.reshape(1, 2 * D), dw, b_dw.reshape(1, D),
      ln_g.reshape(1, D), ln_b.reshape(1, D), w_pw2.astype(BF16), b_pw2.reshape(1, D), st)


def _swiglu_hidden(h, wg_ref, wu_ref, hid_s, lead=()):
    for c0, cn in FF_CHUNKS:
        a = _dot(h, wg_ref[lead + (slice(None), slice(c0, c0 + cn))])
        b = _dot(h, wu_ref[lead + (slice(None), slice(c0, c0 + cn))])
        hid_s[:, c0:c0 + cn] = (a * jax.nn.sigmoid(a) * b).astype(BF16)


def _ffn_kernel(x_ref, g_ref, wg_ref, wu_ref, wd_ref, o_ref, hid_s):
    x = x_ref[...]
    h = _rmsnorm(x, g_ref[...]).astype(BF16)
    _swiglu_hidden(h, wg_ref, wu_ref, hid_s, lead=(0,))
    o_ref[...] = x + _dot(hid_s[...], wd_ref[0])


def _dense_ffn(x, g, layer, w_gate, w_up, w_down):
    wspec = lambda shape: pl.BlockSpec((1,) + shape, lambda i: (layer, 0, 0), pipeline_mode=pl.Buffered(1))
    return pl.pallas_call(
        _ffn_kernel,
        grid=(N_TILES,),
        in_specs=[
            pl.BlockSpec((TM, D), lambda i: (i, 0)),
            _const_spec((1, D)),
            wspec((D, D_FF)),
            wspec((D, D_FF)),
            wspec((D_FF, D)),
        ],
        out_specs=pl.BlockSpec((TM, D), lambda i: (i, 0)),
        out_shape=jax.ShapeDtypeStruct((N_TOK, D), F32),
        scratch_shapes=[pltpu.VMEM((TM, D_FF), BF16)],
        compiler_params=pltpu.CompilerParams(dimension_semantics=("arbitrary",), vmem_limit_bytes=VMEM_LIMIT),
        name="dense_ffn",
    )(x, g.reshape(1, D), w_gate, w_up, w_down)


def _router_kernel(x_ref, g_ref, wr_ref, meta_ref, wts_ref, cnt_ref, run_s):
    i = pl.program_id(0)

    @pl.when(i == 0)
    def _():
        run_s[...] = jnp.zeros((1, LANES), F32)

    h = _rmsnorm(x_ref[...], g_ref[...])
    logits = _dot(h.astype(BF16), wr_ref[...].astype(BF16))
    lane = lax.broadcasted_iota(jnp.int32, (TM, LANES), 1)
    lane_f = lane.astype(F32)
    neg = jnp.float32(-jnp.inf)
    logits = jnp.where(lane < N_EXPERTS, logits, neg)
    l1 = jnp.max(logits, axis=-1, keepdims=True)
    e1 = jnp.min(jnp.where(logits == l1, lane_f, float(LANES)), axis=-1, keepdims=True).astype(jnp.int32)
    rest = jnp.where(lane == e1, neg, logits)
    l2 = jnp.max(rest, axis=-1, keepdims=True)
    e2 = jnp.min(jnp.where(rest == l2, lane_f, float(LANES)), axis=-1, keepdims=True).astype(jnp.int32)
    t = jnp.exp(l2 - l1)
    w1 = 1.0 / (1.0 + t)
    w2 = t * w1
    sel = jnp.logical_or(lane == e1, lane == e2)
    onehot = jnp.where(sel, 1.0, 0.0)
    r = lax.broadcasted_iota(jnp.int32, (TM, TM), 0)
    c = lax.broadcasted_iota(jnp.int32, (TM, TM), 1)
    below = jnp.where(r > c, 1.0, 0.0).astype(BF16)
    before = _dot(below, onehot.astype(BF16)) + run_s[...]
    r1 = jnp.sum(jnp.where(lane == e1, before, 0.0), axis=-1, keepdims=True).astype(jnp.int32)
    r2 = jnp.sum(jnp.where(lane == e2, before, 0.0), axis=-1, keepdims=True).astype(jnp.int32)
    run_s[...] = run_s[...] + jnp.sum(onehot, axis=0, keepdims=True)
    meta = jnp.where(lane == 0, e1, jnp.where(lane == 1, e2, jnp.where(lane == 2, r1, jnp.where(lane == 3, r2, 0))))
    meta_ref[...] = meta
    wts_ref[...] = jnp.where(lane == 0, w1, jnp.where(lane == 1, w2, 0.0))
    cnt_ref[...] = run_s[...].astype(jnp.int32)


def _router(x, g, w_router):
    wr = jnp.pad(w_router, ((0, 0), (0, LANES - N_EXPERTS)))
    return pl.pallas_call(
        _router_kernel,
        grid=(N_TILES,),
        in_specs=[
            pl.BlockSpec((TM, D), lambda i: (i, 0)),
            _const_spec((1, D)),
            _const_spec((D, LANES)),
        ],
        out_specs=[
            pl.BlockSpec((TM, LANES), lambda i: (i, 0)),
            pl.BlockSpec((TM, LANES), lambda i: (i, 0)),
            pl.BlockSpec((1, LANES), lambda i: (0, 0)),
        ],
        out_shape=[
            jax.ShapeDtypeStruct((N_TOK, LANES), jnp.int32),
            jax.ShapeDtypeStruct((N_TOK, LANES), F32),
            jax.ShapeDtypeStruct((1, LANES), jnp.int32),
        ],
        scratch_shapes=[pltpu.VMEM((1, LANES), F32)],
        compiler_params=pltpu.CompilerParams(dimension_semantics=("arbitrary",), vmem_limit_bytes=VMEM_LIMIT),
        name="moe_router",
    )(x, g.reshape(1, D), wr)


ROW_TILE = D // LANES
assert ROW_TILE == SUBLANES


def _from_token_major(ref, n):
    return jnp.concatenate([ref[pl.ds(s, n, stride=ROW_TILE), :] for s in range(ROW_TILE)], axis=-1)


def _to_token_major(ref, val, n):
    for s in range(ROW_TILE):
        ref[pl.ds(s, n, stride=ROW_TILE), :] = val[:, s * LANES:(s + 1) * LANES]


def _row_copy(src, src_row, dst, dst_row, sem):
    s0 = pl.multiple_of(src_row * ROW_TILE, ROW_TILE)
    d0 = pl.multiple_of(dst_row * ROW_TILE, ROW_TILE)
    return pltpu.make_async_copy(src.at[pl.ds(s0, ROW_TILE), :], dst.at[pl.ds(d0, ROW_TILE), :], sem)


DMA_UNROLL = 8


def _dispatch_kernel(last_ref, nblk_ref, nu_ref, dest_ref, x_ref, xs_hbm, tok_s, sem, zsem):
    i = pl.program_id(0)
    slot = i % 2
    stage = tok_s.at[slot]

    def zero_block(b):
        return pltpu.make_async_copy(tok_s.at[0], xs_hbm.at[pl.ds(b * (BM * ROW_TILE), BM * ROW_TILE), :], zsem)

    @pl.when(i == 0)
    def _():
        tok_s[0] = jnp.zeros((BM * ROW_TILE, LANES), F32)
        for start in (True, False):
            for e in range(N_EXPERTS):
                tail = N_BLOCKS - 1 - e
                for cond, b in ((nblk_ref[e] > 0, last_ref[e]), (tail >= nu_ref[0], tail)):
                    cp = zero_block(b)
                    pl.when(cond)(cp.start if start else cp.wait)

    _to_token_major(stage, x_ref[...], TM)

    def issue(j, carry):
        for u in range(DMA_UNROLL):
            t = j * DMA_UNROLL + u
            _row_copy(stage, t, xs_hbm, dest_ref[0, 0, 2 * t], sem.at[slot]).start(priority=0)
            _row_copy(stage, t, xs_hbm, dest_ref[0, 0, 2 * t + 1], sem.at[slot]).start(priority=1)
        return carry

    lax.fori_loop(0, TM // DMA_UNROLL, issue, 0)

    def drain(s):
        def body(j, carry):
            for _ in range(2 * DMA_UNROLL):
                _row_copy(tok_s.at[s], 0, xs_hbm, 0, sem.at[s]).wait()
            return carry

        lax.fori_loop(0, TM // DMA_UNROLL, body, 0)

    pl.when(i >= 1)(lambda: drain(1 - slot))
    pl.when(i == N_TILES - 1)(lambda: drain(slot))


def _dispatch(x, dest, last_block, n_blocks, n_used):
    return pl.pallas_call(
        _dispatch_kernel,
        grid_spec=pltpu.PrefetchScalarGridSpec(
            num_scalar_prefetch=3,
            grid=(N_TILES,),
            in_specs=[
                pl.BlockSpec((1, 1, 2 * TM), lambda i, *_: (i, 0, 0), memory_space=pltpu.SMEM),
                pl.BlockSpec((TM, D), lambda i, *_: (i, 0)),
            ],
            out_specs=pl.BlockSpec(memory_space=pl.ANY),
            scratch_shapes=[pltpu.VMEM((2, TM * ROW_TILE, LANES), F32), pltpu.SemaphoreType.DMA((2,)),
                            pltpu.SemaphoreType.DMA(())],
        ),
        out_shape=jax.ShapeDtypeStruct((N_ROWS * ROW_TILE, LANES), F32),
        compiler_params=pltpu.CompilerParams(dimension_semantics=("arbitrary",), vmem_limit_bytes=VMEM_LIMIT),
        name="moe_dispatch",
    )(last_block, n_blocks, n_used, dest.reshape(N_TILES, 1, 2 * TM), x)


def _gmm_kernel(be_ref, nu_ref, xs_ref, g_ref, wg_ref, wu_ref, wd_ref, ys_ref, hid_s):
    used = pl.program_id(0) < nu_ref[0]

    @pl.when(used)
    def _():
        h = _rmsnorm(_from_token_major(xs_ref, BM), g_ref[...]).astype(BF16)
        _swiglu_hidden(h, wg_ref, wu_ref, hid_s, lead=(0, 0))
        _to_token_major(ys_ref, _dot(hid_s[...], wd_ref[0, 0]), BM)

    @pl.when(jnp.logical_not(used))
    def _():
        ys_ref[...] = jnp.zeros((BM * ROW_TILE, LANES), F32)


def _gmm(xs, g, block_expert, n_used, layer, w_gate, w_up, w_down):
    row_map = lambda i, be, nu: (jnp.minimum(i, nu[0] - 1), 0)
    wspec = lambda shape: pl.BlockSpec((1, 1) + shape, lambda i, be, nu: (layer, be[i], 0, 0))
    return pl.pallas_call(
        _gmm_kernel,
        grid_spec=pltpu.PrefetchScalarGridSpec(
            num_scalar_prefetch=2,
            grid=(N_BLOCKS,),
            in_specs=[
                pl.BlockSpec((BM * ROW_TILE, LANES), row_map),
                pl.BlockSpec((1, D), lambda i, be, nu: (0, 0)),
                wspec((D, D_FF)),
                wspec((D, D_FF)),
                wspec((D_FF, D)),
            ],
            out_specs=pl.BlockSpec((BM * ROW_TILE, LANES), lambda i, be, nu: (i, 0)),
            scratch_shapes=[pltpu.VMEM((BM, D_FF), BF16)],
        ),
        out_shape=jax.ShapeDtypeStruct((N_ROWS * ROW_TILE, LANES), F32),
        compiler_params=pltpu.CompilerParams(dimension_semantics=("arbitrary",), vmem_limit_bytes=VMEM_LIMIT),
        name="moe_gmm",
    )(block_expert, n_used, xs, g.reshape(1, D), w_gate, w_up, w_down)


def _combine_kernel(dest_ref, next_ref, x_ref, wts_ref, gf_ref, ys_hbm, *rest, final):
    out_refs, (buf0, buf1, sem) = rest[:-3], rest[-3:]
    i = pl.program_id(0)
    slot = i % 2

    def gather(idx_ref, s):
        def issue(j, carry):
            for u in range(DMA_UNROLL):
                t = j * DMA_UNROLL + u
                _row_copy(ys_hbm, idx_ref[0, 0, 2 * t], buf0.at[s], t, sem.at[s]).start(priority=0)
                _row_copy(ys_hbm, idx_ref[0, 0, 2 * t + 1], buf1.at[s], t, sem.at[s]).start(priority=1)
            return carry

        lax.fori_loop(0, TM // DMA_UNROLL, issue, 0)

    pl.when(i == 0)(lambda: gather(dest_ref, slot))
    pl.when(i + 1 < N_TILES)(lambda: gather(next_ref, 1 - slot))

    def drain(j, carry):
        for _ in range(DMA_UNROLL):
            _row_copy(ys_hbm, 0, buf0.at[slot], 0, sem.at[slot]).wait()
            _row_copy(ys_hbm, 0, buf1.at[slot], 0, sem.at[slot]).wait()
        return carry

    lax.fori_loop(0, TM // DMA_UNROLL, drain, 0)
    w = wts_ref[...]
    y = (x_ref[...] + w[:, 0:1] * _from_token_major(buf0.at[slot], TM)
         + w[:, 1:2] * _from_token_major(buf1.at[slot], TM))
    if final:
        y = _rmsnorm(y, gf_ref[...])
        yp_ref, ysm_ref = out_refs
        is_sample = pl.program_id(0) >= N_PROMPT_TILES

        @pl.when(jnp.logical_not(is_sample))
        def _():
            yp_ref[...] = y

        @pl.when(is_sample)
        def _():
            ysm_ref[...] = y
    else:
        out_refs[0][...] = y


def _combine(x, dest, wts, ys, g_final, final):
    if final:
        out_specs = [pl.BlockSpec((TM, D), lambda i: (jnp.minimum(i, N_PROMPT_TILES - 1), 0)),
                     pl.BlockSpec((TM, D), lambda i: (0, 0))]
        out_shape = [jax.ShapeDtypeStruct((N_PROMPT_TOK, D), F32), jax.ShapeDtypeStruct((N_SAMPLE_TOK, D), F32)]
    else:
        out_specs = [pl.BlockSpec((TM, D), lambda i: (i, 0))]
        out_shape = [jax.ShapeDtypeStruct((N_TOK, D), F32)]
    dest3 = dest.reshape(N_TILES, 1, 2 * TM)
    outs = pl.pallas_call(
        functools.partial(_combine_kernel, final=final),
        grid=(N_TILES,),
        in_specs=[
            pl.BlockSpec((1, 1, 2 * TM), lambda i: (i, 0, 0), memory_space=pltpu.SMEM),
            pl.BlockSpec((1, 1, 2 * TM), lambda i: (jnp.minimum(i + 1, N_TILES - 1), 0, 0), memory_space=pltpu.SMEM),
            pl.BlockSpec((TM, D), lambda i: (i, 0)),
            pl.BlockSpec((TM, LANES), lambda i: (i, 0)),
            _const_spec((1, D)),
            pl.BlockSpec(memory_space=pl.ANY),
        ],
        out_specs=out_specs,
        out_shape=out_shape,
        scratch_shapes=[pltpu.VMEM((2, TM * ROW_TILE, LANES), F32), pltpu.VMEM((2, TM * ROW_TILE, LANES), F32),
                        pltpu.SemaphoreType.DMA((2,))],
        compiler_params=pltpu.CompilerParams(dimension_semantics=("arbitrary",), vmem_limit_bytes=VMEM_LIMIT),
        name="moe_combine",
    )(dest3, dest3, x, wts, g_final.reshape(1, D), ys)
    return outs if final else outs[0]


def _moe_ffn(x, g, w_router, layer, w_gate, w_up, w_down, g_final, final):
    meta, wts, counts = _router(x, g, w_router)
    counts = counts[0, :N_EXPERTS]
    blocks = (counts + BM - 1) // BM
    block_end = jnp.cumsum(blocks)
    seg_start = (block_end - blocks) * BM
    dest = seg_start[meta[:, 0:2]] + meta[:, 2:4]
    n_used = block_end[-1:]
    bidx = jnp.minimum(jnp.arange(N_BLOCKS, dtype=jnp.int32), n_used - 1)
    block_expert = jnp.sum(bidx[:, None] >= block_end[None, :], axis=1).astype(jnp.int32)
    n_used = n_used.astype(jnp.int32)
    xs = _dispatch(x, dest, (block_end - 1).astype(jnp.int32), blocks.astype(jnp.int32), n_used)
    ys = _gmm(xs, g, block_expert, n_used, layer, w_gate, w_up, w_down)
    return _combine(x, dest, wts, ys, g_final, final)


def kernel(x_prompt, x_sample, state_conv_b, state_conv_c, norm_mix, norm_ffn, norm_final, a_w_in, a_ln_g, a_ln_b, a_w_s, a_b_s, a_w_out, b_w_in, b_conv, b_w_out, c_w_pw1, c_b_pw1, c_dw, c_b_dw, c_ln_g, c_ln_b, c_w_pw2, c_b_pw2, f_w_gate, f_w_up, f_w_down, m_router, m_w_gate, m_w_up, m_w_down):
    last_tile = jnp.arange(N_SEQ) * TILES_PER_SEQ + TILES_PER_SEQ - 1
    fw = [w.astype(BF16) for w in (f_w_gate, f_w_up, f_w_down)]
    mw = [w.astype(BF16) for w in (m_w_gate, m_w_up, m_w_down)]

    x, v0 = _gmlp_mixer(x_prompt.reshape(N_PROMPT_TOK, D), x_sample.reshape(N_SAMPLE_TOK, D), 0,
                        norm_mix[0], a_w_in[0], a_ln_g[0], a_ln_b[0], a_w_s[0], a_b_s[0], a_w_out[0])
    x = _dense_ffn(x, norm_ffn[0], 0, *fw)
    x, tb_p, tb_s = _sconv_mixer(x, norm_mix[1], b_w_in[0], b_conv[0], b_w_out[0], state_conv_b[0])
    x = _moe_ffn(x, norm_ffn[1], m_router[0], 0, *mw, norm_final, False)
    x, tc_p, tc_s = _conf_mixer(x, norm_mix[2], c_w_pw1[0], c_b_pw1[0], c_dw[0], c_b_dw[0], c_ln_g[0], c_ln_b[0],
                                c_w_pw2[0], c_b_pw2[0], state_conv_c[0])
    x = _dense_ffn(x, norm_ffn[2], 1, *fw)
    x, v1 = _gmlp_mixer(x, x, N_PROMPT_TILES,
                        norm_mix[3], a_w_in[1], a_ln_g[1], a_ln_b[1], a_w_s[1], a_b_s[1], a_w_out[1])
    y_prompt, y_sample = _moe_ffn(x, norm_ffn[3], m_router[1], 1, *mw, norm_final, True)

    y_prompt = y_prompt.reshape(N_SEQ, SEQ, D)
    y_sample = y_sample.reshape(N_SEQ, SAMPLE_SEQ, D)
    nb = B_WIDTH - 1
    nc = C_WIDTH - 1
    new_b_p = tb_p[last_tile, B_HALO - nb:, :][None]
    new_b_s = tb_s[:, B_HALO - nb:, :][None]
    new_c_p = tc_p[last_tile, C_HALO - nc:, :][None]
    new_c_s = tc_s[:, C_HALO - nc:, :][None]
    new_v = jnp.stack([v0, v1]).reshape(2, N_SEQ, SAMPLE_SEQ, A_HALF)
    return (y_prompt, y_sample, new_b_p, new_b_s, new_c_p, new_c_s, new_v)
```

```python
import functools

import jax
import jax.numpy as jnp
from jax import lax
from jax.experimental import pallas as pl
from jax.experimental.pallas import tpu as pltpu

F32 = jnp.float32
BF16 = jnp.bfloat16

D = 1024
TM = 512
SEQ = 4096
SAMPLE_SEQ = 64
N_SEQ = 8
TILES_PER_SEQ = SEQ // TM
N_PROMPT_TOK = N_SEQ * SEQ
N_SAMPLE_TOK = N_SEQ * SAMPLE_SEQ
N_TOK = N_PROMPT_TOK + N_SAMPLE_TOK
N_PROMPT_TILES = N_PROMPT_TOK // TM
N_TILES = N_TOK // TM
assert N_SAMPLE_TOK == TM and N_TILES == N_PROMPT_TILES + 1

A_HALF = 2 * D
A_GROUPS = 8
A_HEAD = A_HALF // A_GROUPS
A_CHUNK = 128
B_WIDTH = 3
C_WIDTH = 31
B_HALO = 8
C_HALO = 32
D_FF = 2816
N_EXPERTS = 8
LANES = 128
SUBLANES = 8
MXU_COLS = 256
RMS_EPS = 1e-6
LN_EPS = 1e-5

BM = TM
N_ROWS = 2 * N_TOK + N_EXPERTS * BM
N_BLOCKS = N_ROWS // BM
FF_CHUNKS = ((0, 1024), (1024, 1024), (2048, 768))

VMEM_LIMIT = 56 * 1024 * 1024


def _const_spec(shape):
    return pl.BlockSpec(shape, lambda *_: (0,) * len(shape), pipeline_mode=pl.Buffered(1))


def _rmsnorm(x, g):
    return x * lax.rsqrt(jnp.mean(x * x, axis=-1, keepdims=True) + RMS_EPS) * g


def _layernorm(x, g, b):
    mu = jnp.mean(x, axis=-1, keepdims=True)
    xc = x - mu
    var = jnp.mean(xc * xc, axis=-1, keepdims=True)
    return xc * lax.rsqrt(var + LN_EPS) * g + b


def _dot(a, b):
    return jnp.dot(a, b, preferred_element_type=F32)


def _gmlp_kernel(xp_ref, xs_ref, g_ref, win_ref, lng_ref, lnb_ref, ws_ref, bs_ref, wout_ref, *rest, route):
    if route:
        gr_ref, wr_ref, o_ref, v_ref, meta_ref, wts_ref, cnt_ref, vn_s, y_s, run_s = rest
    else:
        o_ref, v_ref, vn_s, y_s = rest
    is_sample = pl.program_id(0) >= N_PROMPT_TILES
    x = jnp.where(is_sample, xs_ref[...], xp_ref[...])
    h = _rmsnorm(x, g_ref[...]).astype(BF16)
    v = jax.nn.gelu(_dot(h, win_ref[:, A_HALF:]))
    vn = _layernorm(v, lng_ref[...], lnb_ref[...])
    v_ref[...] = vn
    vn_s[...] = vn.astype(BF16)
    r = lax.broadcasted_iota(jnp.int32, (A_CHUNK, A_CHUNK), 0)
    c = lax.broadcasted_iota(jnp.int32, (A_CHUNK, A_CHUNK), 1)
    seg_shift = jnp.where(is_sample, SAMPLE_SEQ.bit_length() - 1, A_CHUNK.bit_length() - 1)
    mask = (r >= c) & ((r >> seg_shift) == (c >> seg_shift))
    for g in range(A_GROUPS):
        cols = slice(g * A_HEAD, (g + 1) * A_HEAD)
        u_g = jax.nn.gelu(_dot(h, win_ref[:, cols]))
        ws = jnp.where(mask, ws_ref[0, g], 0.0).astype(BF16)
        bias = jnp.concatenate([bs_ref[0, g]] * (A_HEAD // LANES), axis=1)
        for ch in range(TM // A_CHUNK):
            rows = slice(ch * A_CHUNK, (ch + 1) * A_CHUNK)
            s = _dot(ws, vn_s[rows, cols]) + bias
            y_s[rows, cols] = (u_g[rows] * s).astype(BF16)
    x_new = x + _dot(y_s[...], wout_ref[...])
    o_ref[...] = x_new
    if route:
        _route(x_new, gr_ref, wr_ref, meta_ref, wts_ref, cnt_ref, run_s)


def _gmlp_mixer(x_prompt, x_sample, sample_block, g, w_in, ln_g, ln_b, w_s, b_s, w_out, routing=None):
    r_ops, r_in, r_out, r_shape, r_scratch = _route_io(*routing) if routing else ([], [], [], [], [])
    ws2 = jnp.stack([w_s, jnp.tile(w_s[:, :SAMPLE_SEQ, :SAMPLE_SEQ], (1, 2, 2))])
    b2 = jnp.stack([b_s, jnp.tile(b_s[:, :SAMPLE_SEQ], (1, 2))])
    b2 = jnp.broadcast_to(b2[..., None], (2, A_GROUPS, A_CHUNK, LANES))
    return pl.pallas_call(
        functools.partial(_gmlp_kernel, route=bool(routing)),
        grid=(N_TILES,),
        in_specs=[
            pl.BlockSpec((TM, D), lambda i: (jnp.minimum(i, N_PROMPT_TILES - 1), 0)),
            pl.BlockSpec((TM, D), lambda i: (sample_block, 0)),
            _const_spec((1, D)),
            _const_spec((D, 2 * A_HALF)),
            _const_spec((1, A_HALF)),
            _const_spec((1, A_HALF)),
            pl.BlockSpec((1, A_GROUPS, A_CHUNK, A_CHUNK), lambda i: (i // N_PROMPT_TILES, 0, 0, 0)),
            pl.BlockSpec((1, A_GROUPS, A_CHUNK, LANES), lambda i: (i // N_PROMPT_TILES, 0, 0, 0)),
            _const_spec((A_HALF, D)),
        ] + r_in,
        out_specs=[
            pl.BlockSpec((TM, D), lambda i: (i, 0)),
            pl.BlockSpec((TM, A_HALF), lambda i: (0, 0)),
        ] + r_out,
        out_shape=[
            jax.ShapeDtypeStruct((N_TOK, D), F32),
            jax.ShapeDtypeStruct((N_SAMPLE_TOK, A_HALF), F32),
        ] + r_shape,
        scratch_shapes=[pltpu.VMEM((TM, A_HALF), BF16), pltpu.VMEM((TM, A_HALF), BF16)] + r_scratch,
        compiler_params=pltpu.CompilerParams(dimension_semantics=("arbitrary",), vmem_limit_bytes=VMEM_LIMIT),
        name="gmlp_mixer",
    )(x_prompt, x_sample, g.reshape(1, D), w_in.astype(BF16), ln_g.reshape(1, A_HALF), ln_b.reshape(1, A_HALF),
      ws2, b2, w_out.astype(BF16), *r_ops)


def _sconv_kernel(x_ref, g_ref, win_ref, cw_ref, wout_ref, st_ref, gr_ref, wr_ref,
                  o_ref, tailp_ref, tails_ref, meta_ref, wts_ref, cnt_ref, full_s, y_s, run_s):
    i = pl.program_id(0)
    is_sample = i >= N_PROMPT_TILES
    x = x_ref[...]
    h = _rmsnorm(x, g_ref[...]).astype(BF16)
    p = _dot(h, win_ref[...])
    bg = p[:, :D]
    u = p[:, D:2 * D] * p[:, 2 * D:]
    w = [cw_ref[k:k + 1, :] for k in range(B_WIDTH)]

    def conv(n):
        return sum(w[k] * full_s[pl.ds(B_HALO - (B_WIDTH - 1) + k, n), :] for k in range(B_WIDTH))

    @pl.when(jnp.logical_not(is_sample))
    def _():
        @pl.when(i % TILES_PER_SEQ == 0)
        def _():
            full_s[0:B_HALO, :] = jnp.zeros((B_HALO, D), F32)

        full_s[B_HALO:, :] = u
        y_s[...] = (bg * conv(TM)).astype(BF16)
        tail = full_s[TM:TM + B_HALO, :]
        tailp_ref[0] = tail
        full_s[0:B_HALO, :] = tail

    @pl.when(is_sample)
    def _():
        for b in range(N_SEQ):
            rows = slice(b * SAMPLE_SEQ, (b + 1) * SAMPLE_SEQ)
            full_s[0:B_HALO, :] = st_ref[b]
            full_s[B_HALO:B_HALO + SAMPLE_SEQ, :] = u[rows]
            y_s[rows, :] = (bg[rows] * conv(SAMPLE_SEQ)).astype(BF16)
            tails_ref[b] = full_s[SAMPLE_SEQ:SAMPLE_SEQ + B_HALO, :]
        tailp_ref[0] = jnp.zeros((B_HALO, D), F32)

    x_new = x + _dot(y_s[...], wout_ref[...])
    o_ref[...] = x_new
    _route(x_new, gr_ref, wr_ref, meta_ref, wts_ref, cnt_ref, run_s)


def _sconv_mixer(x, g, w_in, w_conv, w_out, state, g_route, w_router):
    st = jnp.pad(state, ((0, 0), (B_HALO - (B_WIDTH - 1), 0), (0, 0)))
    r_ops, r_in, r_out, r_shape, r_scratch = _route_io(g_route, w_router)
    return pl.pallas_call(
        _sconv_kernel,
        grid=(N_TILES,),
        in_specs=[
            pl.BlockSpec((TM, D), lambda i: (i, 0)),
            _const_spec((1, D)),
            _const_spec((D, 3 * D)),
            _const_spec((B_WIDTH, D)),
            _const_spec((D, D)),
            _const_spec((N_SEQ, B_HALO, D)),
        ] + r_in,
        out_specs=[
            pl.BlockSpec((TM, D), lambda i: (i, 0)),
            pl.BlockSpec((1, B_HALO, D), lambda i: (i, 0, 0)),
            pl.BlockSpec((N_SEQ, B_HALO, D), lambda i: (0, 0, 0)),
        ] + r_out,
        out_shape=[
            jax.ShapeDtypeStruct((N_TOK, D), F32),
            jax.ShapeDtypeStruct((N_TILES, B_HALO, D), F32),
            jax.ShapeDtypeStruct((N_SEQ, B_HALO, D), F32),
        ] + r_shape,
        scratch_shapes=[pltpu.VMEM((B_HALO + TM, D), F32), pltpu.VMEM((TM, D), BF16)] + r_scratch,
        compiler_params=pltpu.CompilerParams(dimension_semantics=("arbitrary",), vmem_limit_bytes=VMEM_LIMIT),
        name="sconv_mixer",
    )(x, g.reshape(1, D), w_in.astype(BF16), w_conv, w_out.astype(BF16), st, *r_ops)


C_ROW_BLOCK = 32


def _conf_kernel(x_ref, g_ref, w1_ref, b1_ref, dw_ref, bdw_ref, lng_ref, lnb_ref, w2_ref, b2_ref, st_ref,
                 o_ref, tailp_ref, tails_ref, full_s, shift_s, conv_s, y_s):
    i = pl.program_id(0)
    is_sample = i >= N_PROMPT_TILES
    x = x_ref[...]
    h = _rmsnorm(x, g_ref[...]).astype(BF16)
    p = _dot(h, w1_ref[...]) + b1_ref[...]
    gl = p[:, :D] * jax.nn.sigmoid(p[:, D:])

    def conv(n, out_row0):
        n_in = C_HALO + n
        for r in range(1, SUBLANES):
            shift_s[r - 1, 0:n_in - SUBLANES, :] = full_s[r:r + n_in - SUBLANES, :]

        def block(rb, carry):
            base = pl.multiple_of(rb * C_ROW_BLOCK, C_ROW_BLOCK)
            acc = [jnp.zeros((SUBLANES, D), F32) for _ in range(C_ROW_BLOCK // SUBLANES)]
            for k in range(C_WIDTH):
                off = k + (C_HALO - (C_WIDTH - 1))
                row0 = base + (off // SUBLANES) * SUBLANES
                wk = dw_ref[k]
                for a in range(len(acc)):
                    rows = pl.ds(row0 + a * SUBLANES, SUBLANES)
                    tap = full_s[rows, :] if off % SUBLANES == 0 else shift_s[off % SUBLANES - 1, rows, :]
                    acc[a] = acc[a] + wk * tap
            conv_s[pl.ds(out_row0 + base, C_ROW_BLOCK), :] = jnp.concatenate(acc, axis=0)
            return carry

        lax.fori_loop(0, n // C_ROW_BLOCK, block, 0)

    @pl.when(jnp.logical_not(is_sample))
    def _():
        @pl.when(i % TILES_PER_SEQ == 0)
        def _():
            full_s[0:C_HALO, :] = jnp.zeros((C_HALO, D), F32)

        full_s[C_HALO:, :] = gl
        conv(TM, 0)
        tail = full_s[TM:TM + C_HALO, :]
        tailp_ref[0] = tail
        full_s[0:C_HALO, :] = tail

    @pl.when(is_sample)
    def _():
        for b in range(N_SEQ):
            full_s[0:C_HALO, :] = st_ref[b]
            full_s[C_HALO:C_HALO + SAMPLE_SEQ, :] = gl[b * SAMPLE_SEQ:(b + 1) * SAMPLE_SEQ]
            conv(SAMPLE_SEQ, b * SAMPLE_SEQ)
            tails_ref[b] = full_s[SAMPLE_SEQ:SAMPLE_SEQ + C_HALO, :]
        tailp_ref[0] = jnp.zeros((C_HALO, D), F32)

    y = _layernorm(conv_s[...] + bdw_ref[...], lng_ref[...], lnb_ref[...])
    y_s[...] = (y * jax.nn.sigmoid(y)).astype(BF16)
    o_ref[...] = x + _dot(y_s[...], w2_ref[...]) + b2_ref[...]


def _conf_mixer(x, g, w_pw1, b_pw1, w_dw, b_dw, ln_g, ln_b, w_pw2, b_pw2, state):
    st = jnp.pad(state, ((0, 0), (C_HALO - (C_WIDTH - 1), 0), (0, 0)))
    dw = jnp.broadcast_to(w_dw[:, None, :], (C_WIDTH, SUBLANES, D))
    return pl.pallas_call(
        _conf_kernel,
        grid=(N_TILES,),
        in_specs=[
            pl.BlockSpec((TM, D), lambda i: (i, 0)),
            _const_spec((1, D)),
            _const_spec((D, 2 * D)),
            _const_spec((1, 2 * D)),
            _const_spec((C_WIDTH, SUBLANES, D)),
            _const_spec((1, D)),
            _const_spec((1, D)),
            _const_spec((1, D)),
            _const_spec((D, D)),
            _const_spec((1, D)),
            _const_spec((N_SEQ, C_HALO, D)),
        ],
        out_specs=[
            pl.BlockSpec((TM, D), lambda i: (i, 0)),
            pl.BlockSpec((1, C_HALO, D), lambda i: (i, 0, 0)),
            pl.BlockSpec((N_SEQ, C_HALO, D), lambda i: (0, 0, 0)),
        ],
        out_shape=[
            jax.ShapeDtypeStruct((N_TOK, D), F32),
            jax.ShapeDtypeStruct((N_TILES, C_HALO, D), F32),
            jax.ShapeDtypeStruct((N_SEQ, C_HALO, D), F32),
        ],
        scratch_shapes=[
            pltpu.VMEM((C_HALO + TM, D), F32),
            pltpu.VMEM((SUBLANES - 1, C_HALO + TM, D), F32),
            pltpu.VMEM((TM, D), F32),
            pltpu.VMEM((TM, D), BF16),
        ],
        compiler_params=pltpu.CompilerParams(dimension_semantics=("arbitrary",), vmem_limit_bytes=VMEM_LIMIT),
        name="conf_mixer",
    )(x, g.reshape(1, D), w_pw1.astype(BF16), b_pw1.reshape(1, 2 * D), dw, b_dw.reshape(1, D),
      ln_g.reshape(1, D), ln_b.reshape(1, D), w_pw2.astype(BF16), b_pw2.reshape(1, D), st)


def _swiglu_hidden(h, wg_ref, wu_ref, hid_s, lead=()):
    for c0, cn in FF_CHUNKS:
        a = _dot(h, wg_ref[lead + (slice(None), slice(c0, c0 + cn))])
        b = _dot(h, wu_ref[lead + (slice(None), slice(c0, c0 + cn))])
        hid_s[:, c0:c0 + cn] = (a * jax.nn.sigmoid(a) * b).astype(BF16)


def _ffn_kernel(x_ref, g_ref, wg_ref, wu_ref, wd_ref, o_ref, hid_s):
    x = x_ref[...]
    h = _rmsnorm(x, g_ref[...]).astype(BF16)
    _swiglu_hidden(h, wg_ref, wu_ref, hid_s, lead=(0,))
    o_ref[...] = x + _dot(hid_s[...], wd_ref[0])


def _dense_ffn(x, g, layer, w_gate, w_up, w_down):
    wspec = lambda shape: pl.BlockSpec((1,) + shape, lambda i: (layer, 0, 0), pipeline_mode=pl.Buffered(1))
    return pl.pallas_call(
        _ffn_kernel,
        grid=(N_TILES,),
        in_specs=[
            pl.BlockSpec((TM, D), lambda i: (i, 0)),
            _const_spec((1, D)),
            wspec((D, D_FF)),
            wspec((D, D_FF)),
            wspec((D_FF, D)),
        ],
        out_specs=pl.BlockSpec((TM, D), lambda i: (i, 0)),
        out_shape=jax.ShapeDtypeStruct((N_TOK, D), F32),
        scratch_shapes=[pltpu.VMEM((TM, D_FF), BF16)],
        compiler_params=pltpu.CompilerParams(dimension_semantics=("arbitrary",), vmem_limit_bytes=VMEM_LIMIT),
        name="dense_ffn",
    )(x, g.reshape(1, D), w_gate, w_up, w_down)


def _route(x, g_ref, wr_ref, meta_ref, wts_ref, cnt_ref, run_s):
    @pl.when(pl.program_id(0) == 0)
    def _():
        run_s[...] = jnp.zeros((1, LANES), F32)

    h = _rmsnorm(x, g_ref[...])
    logits = _dot(h.astype(BF16), wr_ref[...].astype(BF16))
    lane = lax.broadcasted_iota(jnp.int32, (TM, LANES), 1)
    lane_f = lane.astype(F32)
    neg = jnp.float32(-jnp.inf)
    logits = jnp.where(lane < N_EXPERTS, logits, neg)
    l1 = jnp.max(logits, axis=-1, keepdims=True)
    e1 = jnp.min(jnp.where(logits == l1, lane_f, float(LANES)), axis=-1, keepdims=True).astype(jnp.int32)
    rest = jnp.where(lane == e1, neg, logits)
    l2 = jnp.max(rest, axis=-1, keepdims=True)
    e2 = jnp.min(jnp.where(rest == l2, lane_f, float(LANES)), axis=-1, keepdims=True).astype(jnp.int32)
    t = jnp.exp(l2 - l1)
    w1 = 1.0 / (1.0 + t)
    w2 = t * w1
    sel = jnp.logical_or(lane == e1, lane == e2)
    onehot = jnp.where(sel, 1.0, 0.0)
    r = lax.broadcasted_iota(jnp.int32, (TM, TM), 0)
    c = lax.broadcasted_iota(jnp.int32, (TM, TM), 1)
    below = jnp.where(r > c, 1.0, 0.0).astype(BF16)
    before = _dot(below, onehot.astype(BF16)) + run_s[...]
    r1 = jnp.sum(jnp.where(lane == e1, before, 0.0), axis=-1, keepdims=True).astype(jnp.int32)
    r2 = jnp.sum(jnp.where(lane == e2, before, 0.0), axis=-1, keepdims=True).astype(jnp.int32)
    run_s[...] = run_s[...] + jnp.sum(onehot, axis=0, keepdims=True)
    meta = jnp.where(lane == 0, e1, jnp.where(lane == 1, e2, jnp.where(lane == 2, r1, jnp.where(lane == 3, r2, 0))))
    meta_ref[...] = meta
    wts_ref[...] = jnp.where(lane == 0, w1, jnp.where(lane == 1, w2, 0.0))
    cnt_ref[...] = run_s[...].astype(jnp.int32)


def _route_io(g, w_router):
    operands = [g.reshape(1, D), jnp.pad(w_router, ((0, 0), (0, LANES - N_EXPERTS)))]
    in_specs = [_const_spec((1, D)), _const_spec((D, LANES))]
    out_specs = [
        pl.BlockSpec((TM, LANES), lambda i: (i, 0)),
        pl.BlockSpec((TM, LANES), lambda i: (i, 0)),
        pl.BlockSpec((1, LANES), lambda i: (0, 0)),
    ]
    out_shape = [
        jax.ShapeDtypeStruct((N_TOK, LANES), jnp.int32),
        jax.ShapeDtypeStruct((N_TOK, LANES), F32),
        jax.ShapeDtypeStruct((1, LANES), jnp.int32),
    ]
    return operands, in_specs, out_specs, out_shape, [pltpu.VMEM((1, LANES), F32)]


ROW_TILE = D // LANES
assert ROW_TILE == SUBLANES


def _from_token_major(ref, n):
    return jnp.concatenate([ref[pl.ds(s, n, stride=ROW_TILE), :] for s in range(ROW_TILE)], axis=-1)


def _to_token_major(ref, val, n):
    for s in range(ROW_TILE):
        ref[pl.ds(s, n, stride=ROW_TILE), :] = val[:, s * LANES:(s + 1) * LANES]


def _row_copy(src, src_row, dst, dst_row, sem):
    s0 = pl.multiple_of(src_row * ROW_TILE, ROW_TILE)
    d0 = pl.multiple_of(dst_row * ROW_TILE, ROW_TILE)
    return pltpu.make_async_copy(src.at[pl.ds(s0, ROW_TILE), :], dst.at[pl.ds(d0, ROW_TILE), :], sem)


DMA_UNROLL = 16


def _dispatch_kernel(last_ref, nblk_ref, nu_ref, dest_ref, x_ref, xs_hbm, tok_s, sem, zsem):
    i = pl.program_id(0)
    slot = i % 2
    stage = tok_s.at[slot]

    def zero_block(b):
        return pltpu.make_async_copy(tok_s.at[0], xs_hbm.at[pl.ds(b * (BM * ROW_TILE), BM * ROW_TILE), :], zsem)

    @pl.when(i == 0)
    def _():
        tok_s[0] = jnp.zeros((BM * ROW_TILE, LANES), F32)
        for start in (True, False):
            for e in range(N_EXPERTS):
                tail = N_BLOCKS - 1 - e
                for cond, b in ((nblk_ref[e] > 0, last_ref[e]), (tail >= nu_ref[0], tail)):
                    cp = zero_block(b)
                    pl.when(cond)(cp.start if start else cp.wait)

    _to_token_major(stage, x_ref[...], TM)

    def issue(j, carry):
        for u in range(DMA_UNROLL):
            t = j * DMA_UNROLL + u
            _row_copy(stage, t, xs_hbm, dest_ref[0, 0, 2 * t], sem.at[slot]).start(priority=0)
            _row_copy(stage, t, xs_hbm, dest_ref[0, 0, 2 * t + 1], sem.at[slot]).start(priority=1)
        return carry

    lax.fori_loop(0, TM // DMA_UNROLL, issue, 0)

    def drain(s):
        def body(j, carry):
            for _ in range(2 * DMA_UNROLL):
                _row_copy(tok_s.at[s], 0, xs_hbm, 0, sem.at[s]).wait()
            return carry

        lax.fori_loop(0, TM // DMA_UNROLL, body, 0)

    pl.when(i >= 1)(lambda: drain(1 - slot))
    pl.when(i == N_TILES - 1)(lambda: drain(slot))


def _dispatch(x, dest, last_block, n_blocks, n_used):
    return pl.pallas_call(
        _dispatch_kernel,
        grid_spec=pltpu.PrefetchScalarGridSpec(
            num_scalar_prefetch=3,
            grid=(N_TILES,),
            in_specs=[
                pl.BlockSpec((1, 1, 2 * TM), lambda i, *_: (i, 0, 0), memory_space=pltpu.SMEM),
                pl.BlockSpec((TM, D), lambda i, *_: (i, 0)),
            ],
            out_specs=pl.BlockSpec(memory_space=pl.ANY),
            scratch_shapes=[pltpu.VMEM((2, TM * ROW_TILE, LANES), F32), pltpu.SemaphoreType.DMA((2,)),
                            pltpu.SemaphoreType.DMA(())],
        ),
        out_shape=jax.ShapeDtypeStruct((N_ROWS * ROW_TILE, LANES), F32),
        compiler_params=pltpu.CompilerParams(dimension_semantics=("arbitrary",), vmem_limit_bytes=VMEM_LIMIT),
        name="moe_dispatch",
    )(last_block, n_blocks, n_used, dest.reshape(N_TILES, 1, 2 * TM), x)


def _gmm_kernel(be_ref, nu_ref, xs_ref, g_ref, wg_ref, wu_ref, wd_ref, ys_ref, hid_s):
    used = pl.program_id(0) < nu_ref[0]

    @pl.when(used)
    def _():
        h = _rmsnorm(_from_token_major(xs_ref, BM), g_ref[...]).astype(BF16)
        _swiglu_hidden(h, wg_ref, wu_ref, hid_s, lead=(0, 0))
        _to_token_major(ys_ref, _dot(hid_s[...], wd_ref[0, 0]), BM)

    @pl.when(jnp.logical_not(used))
    def _():
        ys_ref[...] = jnp.zeros((BM * ROW_TILE, LANES), F32)


def _gmm(xs, g, block_expert, n_used, layer, w_gate, w_up, w_down):
    row_map = lambda i, be, nu: (jnp.minimum(i, nu[0] - 1), 0)
    wspec = lambda shape: pl.BlockSpec((1, 1) + shape, lambda i, be, nu: (layer, be[i], 0, 0))
    return pl.pallas_call(
        _gmm_kernel,
        grid_spec=pltpu.PrefetchScalarGridSpec(
            num_scalar_prefetch=2,
            grid=(N_BLOCKS,),
            in_specs=[
                pl.BlockSpec((BM * ROW_TILE, LANES), row_map),
                pl.BlockSpec((1, D), lambda i, be, nu: (0, 0)),
                wspec((D, D_FF)),
                wspec((D, D_FF)),
                wspec((D_FF, D)),
            ],
            out_specs=pl.BlockSpec((BM * ROW_TILE, LANES), lambda i, be, nu: (i, 0)),
            scratch_shapes=[pltpu.VMEM((BM, D_FF), BF16)],
        ),
        out_shape=jax.ShapeDtypeStruct((N_ROWS * ROW_TILE, LANES), F32),
        compiler_params=pltpu.CompilerParams(dimension_semantics=("arbitrary",), vmem_limit_bytes=VMEM_LIMIT),
        name="moe_gmm",
    )(block_expert, n_used, xs, g.reshape(1, D), w_gate, w_up, w_down)


def _combine_kernel(dest_ref, next_ref, x_ref, wts_ref, gf_ref, ys_hbm, *rest, final):
    out_refs, (buf0, buf1, sem) = rest[:-3], rest[-3:]
    i = pl.program_id(0)
    slot = i % 2

    def gather(idx_ref, s):
        def issue(j, carry):
            for u in range(DMA_UNROLL):
                t = j * DMA_UNROLL + u
                _row_copy(ys_hbm, idx_ref[0, 0, 2 * t], buf0.at[s], t, sem.at[s]).start(priority=0)
                _row_copy(ys_hbm, idx_ref[0, 0, 2 * t + 1], buf1.at[s], t, sem.at[s]).start(priority=1)
            return carry

        lax.fori_loop(0, TM // DMA_UNROLL, issue, 0)

    pl.when(i == 0)(lambda: gather(dest_ref, slot))
    pl.when(i + 1 < N_TILES)(lambda: gather(next_ref, 1 - slot))

    def drain(j, carry):
        for _ in range(DMA_UNROLL):
            _row_copy(ys_hbm, 0, buf0.at[slot], 0, sem.at[slot]).wait()
            _row_copy(ys_hbm, 0, buf1.at[slot], 0, sem.at[slot]).wait()
        return carry

    lax.fori_loop(0, TM // DMA_UNROLL, drain, 0)
    w = wts_ref[...]
    y = (x_ref[...] + w[:, 0:1] * _from_token_major(buf0.at[slot], TM)
         + w[:, 1:2] * _from_token_major(buf1.at[slot], TM))
    if final:
        y = _rmsnorm(y, gf_ref[...])
        yp_ref, ysm_ref = out_refs
        is_sample = pl.program_id(0) >= N_PROMPT_TILES

        @pl.when(jnp.logical_not(is_sample))
        def _():
            yp_ref[...] = y

        @pl.when(is_sample)
        def _():
            ysm_ref[...] = y
    else:
        out_refs[0][...] = y


def _combine(x, dest, wts, ys, g_final, final):
    if final:
        out_specs = [pl.BlockSpec((TM, D), lambda i: (jnp.minimum(i, N_PROMPT_TILES - 1), 0)),
                     pl.BlockSpec((TM, D), lambda i: (0, 0))]
        out_shape = [jax.ShapeDtypeStruct((N_PROMPT_TOK, D), F32), jax.ShapeDtypeStruct((N_SAMPLE_TOK, D), F32)]
    else:
        out_specs = [pl.BlockSpec((TM, D), lambda i: (i, 0))]
        out_shape = [jax.ShapeDtypeStruct((N_TOK, D), F32)]
    dest3 = dest.reshape(N_TILES, 1, 2 * TM)
    outs = pl.pallas_call(
        functools.partial(_combine_kernel, final=final),
        grid=(N_TILES,),
        in_specs=[
            pl.BlockSpec((1, 1, 2 * TM), lambda i: (i, 0, 0), memory_space=pltpu.SMEM),
            pl.BlockSpec((1, 1, 2 * TM), lambda i: (jnp.minimum(i + 1, N_TILES - 1), 0, 0), memory_space=pltpu.SMEM),
            pl.BlockSpec((TM, D), lambda i: (i, 0)),
            pl.BlockSpec((TM, LANES), lambda i: (i, 0)),
            _const_spec((1, D)),
            pl.BlockSpec(memory_space=pl.ANY),
        ],
        out_specs=out_specs,
        out_shape=out_shape,
        scratch_shapes=[pltpu.VMEM((2, TM * ROW_TILE, LANES), F32), pltpu.VMEM((2, TM * ROW_TILE, LANES), F32),
                        pltpu.SemaphoreType.DMA((2,))],
        compiler_params=pltpu.CompilerParams(dimension_semantics=("arbitrary",), vmem_limit_bytes=VMEM_LIMIT),
        name="moe_combine",
    )(dest3, dest3, x, wts, g_final.reshape(1, D), ys)
    return outs if final else outs[0]


def _moe_ffn(x, routing, g, layer, w_gate, w_up, w_down, g_final, final):
    meta, wts, counts = routing
    counts = counts[0, :N_EXPERTS]
    blocks = (counts + BM - 1) // BM
    block_end = jnp.cumsum(blocks)
    seg_start = (block_end - blocks) * BM
    dest = seg_start[meta[:, 0:2]] + meta[:, 2:4]
    n_used = block_end[-1:]
    bidx = jnp.minimum(jnp.arange(N_BLOCKS, dtype=jnp.int32), n_used - 1)
    block_expert = jnp.sum(bidx[:, None] >= block_end[None, :], axis=1).astype(jnp.int32)
    n_used = n_used.astype(jnp.int32)
    xs = _dispatch(x, dest, (block_end - 1).astype(jnp.int32), blocks.astype(jnp.int32), n_used)
    ys = _gmm(xs, g, block_expert, n_used, layer, w_gate, w_up, w_down)
    return _combine(x, dest, wts, ys, g_final, final)


def kernel(x_prompt, x_sample, state_conv_b, state_conv_c, norm_mix, norm_ffn, norm_final, a_w_in, a_ln_g, a_ln_b, a_w_s, a_b_s, a_w_out, b_w_in, b_conv, b_w_out, c_w_pw1, c_b_pw1, c_dw, c_b_dw, c_ln_g, c_ln_b, c_w_pw2, c_b_pw2, f_w_gate, f_w_up, f_w_down, m_router, m_w_gate, m_w_up, m_w_down):
    last_tile = jnp.arange(N_SEQ) * TILES_PER_SEQ + TILES_PER_SEQ - 1
    fw = [w.astype(BF16) for w in (f_w_gate, f_w_up, f_w_down)]
    mw = [w.astype(BF16) for w in (m_w_gate, m_w_up, m_w_down)]

    x, v0 = _gmlp_mixer(x_prompt.reshape(N_PROMPT_TOK, D), x_sample.reshape(N_SAMPLE_TOK, D), 0,
                        norm_mix[0], a_w_in[0], a_ln_g[0], a_ln_b[0], a_w_s[0], a_b_s[0], a_w_out[0])
    x = _dense_ffn(x, norm_ffn[0], 0, *fw)
    x, tb_p, tb_s, *routing = _sconv_mixer(x, norm_mix[1], b_w_in[0], b_conv[0], b_w_out[0], state_conv_b[0],
                                           norm_ffn[1], m_router[0])
    x = _moe_ffn(x, routing, norm_ffn[1], 0, *mw, norm_final, False)
    x, tc_p, tc_s = _conf_mixer(x, norm_mix[2], c_w_pw1[0], c_b_pw1[0], c_dw[0], c_b_dw[0], c_ln_g[0], c_ln_b[0],
                                c_w_pw2[0], c_b_pw2[0], state_conv_c[0])
    x = _dense_ffn(x, norm_ffn[2], 1, *fw)
    x, v1, *routing = _gmlp_mixer(x, x, N_PROMPT_TILES,
                                  norm_mix[3], a_w_in[1], a_ln_g[1], a_ln_b[1], a_w_s[1], a_b_s[1], a_w_out[1],
                                  routing=(norm_ffn[3], m_router[1]))
    y_prompt, y_sample = _moe_ffn(x, routing, norm_ffn[3], 1, *mw, norm_final, True)

    y_prompt = y_prompt.reshape(N_SEQ, SEQ, D)
    y_sample = y_sample.reshape(N_SEQ, SAMPLE_SEQ, D)
    nb = B_WIDTH - 1
    nc = C_WIDTH - 1
    new_b_p = tb_p[last_tile, B_HALO - nb:, :][None]
    new_b_s = tb_s[:, B_HALO - nb:, :][None]
    new_c_p = tc_p[last_tile, C_HALO - nc:, :][None]
    new_c_s = tc_s[:, C_HALO - nc:, :][None]
    new_v = jnp.stack([v0, v1]).reshape(2, N_SEQ, SAMPLE_SEQ, A_HALF)
    return (y_prompt, y_sample, new_b_p, new_b_s, new_c_p, new_c_s, new_v)
```

```python
import functools

import jax
import jax.numpy as jnp
from jax import lax
from jax.experimental import pallas as pl
from jax.experimental.pallas import tpu as pltpu

F32 = jnp.float32
BF16 = jnp.bfloat16

D = 1024
TM = 512
SEQ = 4096
SAMPLE_SEQ = 64
N_SEQ = 8
TILES_PER_SEQ = SEQ // TM
N_PROMPT_TOK = N_SEQ * SEQ
N_SAMPLE_TOK = N_SEQ * SAMPLE_SEQ
N_TOK = N_PROMPT_TOK + N_SAMPLE_TOK
N_PROMPT_TILES = N_PROMPT_TOK // TM
N_TILES = N_TOK // TM
assert N_SAMPLE_TOK == TM and N_TILES == N_PROMPT_TILES + 1

A_HALF = 2 * D
A_GROUPS = 8
A_HEAD = A_HALF // A_GROUPS
A_CHUNK = 128
B_WIDTH = 3
C_WIDTH = 31
B_HALO = 8
C_HALO = 32
D_FF = 2816
N_EXPERTS = 8
LANES = 128
SUBLANES = 8
MXU_COLS = 256
RMS_EPS = 1e-6
LN_EPS = 1e-5

BM = TM
N_ROWS = 2 * N_TOK + N_EXPERTS * BM
N_BLOCKS = N_ROWS // BM
FF_CHUNKS = ((0, 1024), (1024, 1024), (2048, 768))

VMEM_LIMIT = 56 * 1024 * 1024


def _const_spec(shape):
    return pl.BlockSpec(shape, lambda *_: (0,) * len(shape), pipeline_mode=pl.Buffered(1))


def _rmsnorm(x, g):
    return x * lax.rsqrt(jnp.mean(x * x, axis=-1, keepdims=True) + RMS_EPS) * g


def _layernorm(x, g, b):
    mu = jnp.mean(x, axis=-1, keepdims=True)
    xc = x - mu
    var = jnp.mean(xc * xc, axis=-1, keepdims=True)
    return xc * lax.rsqrt(var + LN_EPS) * g + b


def _dot(a, b):
    return jnp.dot(a, b, preferred_element_type=F32)


def _gmlp_kernel(xp_ref, xs_ref, g_ref, win_ref, lng_ref, lnb_ref, ws_ref, bs_ref, wout_ref, *rest, route):
    if route:
        gr_ref, wr_ref, o_ref, v_ref, meta_ref, wts_ref, cnt_ref, vn_s, y_s, run_s = rest
    else:
        o_ref, v_ref, vn_s, y_s = rest
    is_sample = pl.program_id(0) >= N_PROMPT_TILES
    x = jnp.where(is_sample, xs_ref[...], xp_ref[...])
    h = _rmsnorm(x, g_ref[...]).astype(BF16)
    v = jax.nn.gelu(_dot(h, win_ref[:, A_HALF:]))
    vn = _layernorm(v, lng_ref[...], lnb_ref[...])
    v_ref[...] = vn
    vn_s[...] = vn.astype(BF16)
    r = lax.broadcasted_iota(jnp.int32, (A_CHUNK, A_CHUNK), 0)
    c = lax.broadcasted_iota(jnp.int32, (A_CHUNK, A_CHUNK), 1)
    seg_shift = jnp.where(is_sample, SAMPLE_SEQ.bit_length() - 1, A_CHUNK.bit_length() - 1)
    mask = (r >= c) & ((r >> seg_shift) == (c >> seg_shift))
    for g in range(A_GROUPS):
        cols = slice(g * A_HEAD, (g + 1) * A_HEAD)
        u_g = jax.nn.gelu(_dot(h, win_ref[:, cols]))
        ws = jnp.where(mask, ws_ref[0, g], 0.0).astype(BF16)
        bias = jnp.concatenate([bs_ref[0, g]] * (A_HEAD // LANES), axis=1)
        for ch in range(TM // A_CHUNK):
            rows = slice(ch * A_CHUNK, (ch + 1) * A_CHUNK)
            s = _dot(ws, vn_s[rows, cols]) + bias
            y_s[rows, cols] = (u_g[rows] * s).astype(BF16)
    x_new = x + _dot(y_s[...], wout_ref[...])
    o_ref[...] = x_new
    if route:
        _route(x_new, gr_ref, wr_ref, meta_ref, wts_ref, cnt_ref, run_s)


def _gmlp_mixer(x_prompt, x_sample, sample_block, g, w_in, ln_g, ln_b, w_s, b_s, w_out, routing=None):
    r_ops, r_in, r_out, r_shape, r_scratch = _route_io(*routing) if routing else ([], [], [], [], [])
    ws2 = jnp.stack([w_s, jnp.tile(w_s[:, :SAMPLE_SEQ, :SAMPLE_SEQ], (1, 2, 2))])
    b2 = jnp.stack([b_s, jnp.tile(b_s[:, :SAMPLE_SEQ], (1, 2))])
    b2 = jnp.broadcast_to(b2[..., None], (2, A_GROUPS, A_CHUNK, LANES))
    return pl.pallas_call(
        functools.partial(_gmlp_kernel, route=bool(routing)),
        grid=(N_TILES,),
        in_specs=[
            pl.BlockSpec((TM, D), lambda i: (jnp.minimum(i, N_PROMPT_TILES - 1), 0)),
            pl.BlockSpec((TM, D), lambda i: (sample_block, 0)),
            _const_spec((1, D)),
            _const_spec((D, 2 * A_HALF)),
            _const_spec((1, A_HALF)),
            _const_spec((1, A_HALF)),
            pl.BlockSpec((1, A_GROUPS, A_CHUNK, A_CHUNK), lambda i: (i // N_PROMPT_TILES, 0, 0, 0)),
            pl.BlockSpec((1, A_GROUPS, A_CHUNK, LANES), lambda i: (i // N_PROMPT_TILES, 0, 0, 0)),
            _const_spec((A_HALF, D)),
        ] + r_in,
        out_specs=[
            pl.BlockSpec((TM, D), lambda i: (i, 0)),
            pl.BlockSpec((TM, A_HALF), lambda i: (0, 0)),
        ] + r_out,
        out_shape=[
            jax.ShapeDtypeStruct((N_TOK, D), F32),
            jax.ShapeDtypeStruct((N_SAMPLE_TOK, A_HALF), F32),
        ] + r_shape,
        scratch_shapes=[pltpu.VMEM((TM, A_HALF), BF16), pltpu.VMEM((TM, A_HALF), BF16)] + r_scratch,
        compiler_params=pltpu.CompilerParams(dimension_semantics=("arbitrary",), vmem_limit_bytes=VMEM_LIMIT),
        name="gmlp_mixer",
    )(x_prompt, x_sample, g.reshape(1, D), w_in.astype(BF16), ln_g.reshape(1, A_HALF), ln_b.reshape(1, A_HALF),
      ws2, b2, w_out.astype(BF16), *r_ops)


def _sconv_kernel(x_ref, g_ref, win_ref, cw_ref, wout_ref, st_ref, gr_ref, wr_ref,
                  o_ref, tailp_ref, tails_ref, meta_ref, wts_ref, cnt_ref, full_s, y_s, run_s):
    i = pl.program_id(0)
    is_sample = i >= N_PROMPT_TILES
    x = x_ref[...]
    h = _rmsnorm(x, g_ref[...]).astype(BF16)
    u = _dot(h, win_ref[:, D:2 * D]) * _dot(h, win_ref[:, 2 * D:])
    w = [cw_ref[k:k + 1, :] for k in range(B_WIDTH)]

    def conv(n):
        return sum(w[k] * full_s[pl.ds(B_HALO - (B_WIDTH - 1) + k, n), :] for k in range(B_WIDTH))

    @pl.when(jnp.logical_not(is_sample))
    def _():
        @pl.when(i % TILES_PER_SEQ == 0)
        def _():
            full_s[0:B_HALO, :] = jnp.zeros((B_HALO, D), F32)

        full_s[B_HALO:, :] = u
        y_s[...] = (_dot(h, win_ref[:, :D]) * conv(TM)).astype(BF16)
        tail = full_s[TM:TM + B_HALO, :]
        tailp_ref[0] = tail
        full_s[0:B_HALO, :] = tail

    @pl.when(is_sample)
    def _():
        bg = _dot(h, win_ref[:, :D])
        for b in range(N_SEQ):
            rows = slice(b * SAMPLE_SEQ, (b + 1) * SAMPLE_SEQ)
            full_s[0:B_HALO, :] = st_ref[b]
            full_s[B_HALO:B_HALO + SAMPLE_SEQ, :] = u[rows]
            y_s[rows, :] = (bg[rows] * conv(SAMPLE_SEQ)).astype(BF16)
            tails_ref[b] = full_s[SAMPLE_SEQ:SAMPLE_SEQ + B_HALO, :]
        tailp_ref[0] = jnp.zeros((B_HALO, D), F32)

    x_new = x + _dot(y_s[...], wout_ref[...])
    o_ref[...] = x_new
    _route(x_new, gr_ref, wr_ref, meta_ref, wts_ref, cnt_ref, run_s)


def _sconv_mixer(x, g, w_in, w_conv, w_out, state, g_route, w_router):
    st = jnp.pad(state, ((0, 0), (B_HALO - (B_WIDTH - 1), 0), (0, 0)))
    r_ops, r_in, r_out, r_shape, r_scratch = _route_io(g_route, w_router)
    return pl.pallas_call(
        _sconv_kernel,
        grid=(N_TILES,),
        in_specs=[
            pl.BlockSpec((TM, D), lambda i: (i, 0)),
            _const_spec((1, D)),
            _const_spec((D, 3 * D)),
            _const_spec((B_WIDTH, D)),
            _const_spec((D, D)),
            _const_spec((N_SEQ, B_HALO, D)),
        ] + r_in,
        out_specs=[
            pl.BlockSpec((TM, D), lambda i: (i, 0)),
            pl.BlockSpec((1, B_HALO, D), lambda i: (i, 0, 0)),
            pl.BlockSpec((N_SEQ, B_HALO, D), lambda i: (0, 0, 0)),
        ] + r_out,
        out_shape=[
            jax.ShapeDtypeStruct((N_TOK, D), F32),
            jax.ShapeDtypeStruct((N_TILES, B_HALO, D), F32),
            jax.ShapeDtypeStruct((N_SEQ, B_HALO, D), F32),
        ] + r_shape,
        scratch_shapes=[pltpu.VMEM((B_HALO + TM, D), F32), pltpu.VMEM((TM, D), BF16)] + r_scratch,
        compiler_params=pltpu.CompilerParams(dimension_semantics=("arbitrary",), vmem_limit_bytes=VMEM_LIMIT),
        name="sconv_mixer",
    )(x, g.reshape(1, D), w_in.astype(BF16), w_conv, w_out.astype(BF16), st, *r_ops)


C_ROW_BLOCK = 32


def _conf_kernel(x_ref, g_ref, w1_ref, b1_ref, dw_ref, bdw_ref, lng_ref, lnb_ref, w2_ref, b2_ref, st_ref,
                 o_ref, tailp_ref, tails_ref, full_s, shift_s, conv_s, y_s):
    i = pl.program_id(0)
    is_sample = i >= N_PROMPT_TILES
    x = x_ref[...]
    h = _rmsnorm(x, g_ref[...]).astype(BF16)
    p = _dot(h, w1_ref[...]) + b1_ref[...]
    gl = p[:, :D] * jax.nn.sigmoid(p[:, D:])

    def conv(n, out_row0):
        n_in = C_HALO + n
        for r in range(1, SUBLANES):
            shift_s[r - 1, 0:n_in - SUBLANES, :] = full_s[r:r + n_in - SUBLANES, :]

        def block(rb, carry):
            base = pl.multiple_of(rb * C_ROW_BLOCK, C_ROW_BLOCK)
            acc = [jnp.zeros((SUBLANES, D), F32) for _ in range(C_ROW_BLOCK // SUBLANES)]
            for k in range(C_WIDTH):
                off = k + (C_HALO - (C_WIDTH - 1))
                row0 = base + (off // SUBLANES) * SUBLANES
                wk = dw_ref[k]
                for a in range(len(acc)):
                    rows = pl.ds(row0 + a * SUBLANES, SUBLANES)
                    tap = full_s[rows, :] if off % SUBLANES == 0 else shift_s[off % SUBLANES - 1, rows, :]
                    acc[a] = acc[a] + wk * tap
            conv_s[pl.ds(out_row0 + base, C_ROW_BLOCK), :] = jnp.concatenate(acc, axis=0)
            return carry

        lax.fori_loop(0, n // C_ROW_BLOCK, block, 0)

    @pl.when(jnp.logical_not(is_sample))
    def _():
        @pl.when(i % TILES_PER_SEQ == 0)
        def _():
            full_s[0:C_HALO, :] = jnp.zeros((C_HALO, D), F32)

        full_s[C_HALO:, :] = gl
        conv(TM, 0)
        tail = full_s[TM:TM + C_HALO, :]
        tailp_ref[0] = tail
        full_s[0:C_HALO, :] = tail

    @pl.when(is_sample)
    def _():
        for b in range(N_SEQ):
            full_s[0:C_HALO, :] = st_ref[b]
            full_s[C_HALO:C_HALO + SAMPLE_SEQ, :] = gl[b * SAMPLE_SEQ:(b + 1) * SAMPLE_SEQ]
            conv(SAMPLE_SEQ, b * SAMPLE_SEQ)
            tails_ref[b] = full_s[SAMPLE_SEQ:SAMPLE_SEQ + C_HALO, :]
        tailp_ref[0] = jnp.zeros((C_HALO, D), F32)

    y = _layernorm(conv_s[...] + bdw_ref[...], lng_ref[...], lnb_ref[...])
    y_s[...] = (y * jax.nn.sigmoid(y)).astype(BF16)
    o_ref[...] = x + _dot(y_s[...], w2_ref[...]) + b2_ref[...]


def _conf_mixer(x, g, w_pw1, b_pw1, w_dw, b_dw, ln_g, ln_b, w_pw2, b_pw2, state):
    st = jnp.pad(state, ((0, 0), (C_HALO - (C_WIDTH - 1), 0), (0, 0)))
    dw = jnp.broadcast_to(w_dw[:, None, :], (C_WIDTH, SUBLANES, D))
    return pl.pallas_call(
        _conf_kernel,
        grid=(N_TILES,),
        in_specs=[
            pl.BlockSpec((TM, D), lambda i: (i, 0)),
            _const_spec((1, D)),
            _const_spec((D, 2 * D)),
            _const_spec((1, 2 * D)),
            _const_spec((C_WIDTH, SUBLANES, D)),
            _const_spec((1, D)),
            _const_spec((1, D)),
            _const_spec((1, D)),
            _const_spec((D, D)),
            _const_spec((1, D)),
            _const_spec((N_SEQ, C_HALO, D)),
        ],
        out_specs=[
            pl.BlockSpec((TM, D), lambda i: (i, 0)),
            pl.BlockSpec((1, C_HALO, D), lambda i: (i, 0, 0)),
            pl.BlockSpec((N_SEQ, C_HALO, D), lambda i: (0, 0, 0)),
        ],
        out_shape=[
            jax.ShapeDtypeStruct((N_TOK, D), F32),
            jax.ShapeDtypeStruct((N_TILES, C_HALO, D), F32),
            jax.ShapeDtypeStruct((N_SEQ, C_HALO, D), F32),
        ],
        scratch_shapes=[
            pltpu.VMEM((C_HALO + TM, D), F32),
            pltpu.VMEM((SUBLANES - 1, C_HALO + TM, D), F32),
            pltpu.VMEM((TM, D), F32),
            pltpu.VMEM((TM, D), BF16),
        ],
        compiler_params=pltpu.CompilerParams(dimension_semantics=("arbitrary",), vmem_limit_bytes=VMEM_LIMIT),
        name="conf_mixer",
    )(x, g.reshape(1, D), w_pw1.astype(BF16), b_pw1.reshape(1, 2 * D), dw, b_dw.reshape(1, D),
      ln_g.reshape(1, D), ln_b.reshape(1, D), w_pw2.astype(BF16), b_pw2.reshape(1, D), st)


def _swiglu_hidden(h, wg_ref, wu_ref, hid_s, lead=()):
    for c0, cn in FF_CHUNKS:
        a = _dot(h, wg_ref[lead + (slice(None), slice(c0, c0 + cn))])
        b = _dot(h, wu_ref[lead + (slice(None), slice(c0, c0 + cn))])
        hid_s[:, c0:c0 + cn] = (a * jax.nn.sigmoid(a) * b).astype(BF16)


def _ffn_kernel(x_ref, g_ref, wg_ref, wu_ref, wd_ref, o_ref, hid_s):
    x = x_ref[...]
    h = _rmsnorm(x, g_ref[...]).astype(BF16)
    _swiglu_hidden(h, wg_ref, wu_ref, hid_s, lead=(0,))
    o_ref[...] = x + _dot(hid_s[...], wd_ref[0])


def _dense_ffn(x, g, layer, w_gate, w_up, w_down):
    wspec = lambda shape: pl.BlockSpec((1,) + shape, lambda i: (layer, 0, 0), pipeline_mode=pl.Buffered(1))
    return pl.pallas_call(
        _ffn_kernel,
        grid=(N_TILES,),
        in_specs=[
            pl.BlockSpec((TM, D), lambda i: (i, 0)),
            _const_spec((1, D)),
            wspec((D, D_FF)),
            wspec((D, D_FF)),
            wspec((D_FF, D)),
        ],
        out_specs=pl.BlockSpec((TM, D), lambda i: (i, 0)),
        out_shape=jax.ShapeDtypeStruct((N_TOK, D), F32),
        scratch_shapes=[pltpu.VMEM((TM, D_FF), BF16)],
        compiler_params=pltpu.CompilerParams(dimension_semantics=("arbitrary",), vmem_limit_bytes=VMEM_LIMIT),
        name="dense_ffn",
    )(x, g.reshape(1, D), w_gate, w_up, w_down)


def _route(x, g_ref, wr_ref, meta_ref, wts_ref, cnt_ref, run_s):
    @pl.when(pl.program_id(0) == 0)
    def _():
        run_s[...] = jnp.zeros((1, LANES), F32)

    h = _rmsnorm(x, g_ref[...])
    logits = _dot(h.astype(BF16), wr_ref[...].astype(BF16))
    lane = lax.broadcasted_iota(jnp.int32, (TM, LANES), 1)
    lane_f = lane.astype(F32)
    neg = jnp.float32(-jnp.inf)
    logits = jnp.where(lane < N_EXPERTS, logits, neg)
    l1 = jnp.max(logits, axis=-1, keepdims=True)
    e1 = jnp.min(jnp.where(logits == l1, lane_f, float(LANES)), axis=-1, keepdims=True).astype(jnp.int32)
    rest = jnp.where(lane == e1, neg, logits)
    l2 = jnp.max(rest, axis=-1, keepdims=True)
    e2 = jnp.min(jnp.where(rest == l2, lane_f, float(LANES)), axis=-1, keepdims=True).astype(jnp.int32)
    t = jnp.exp(l2 - l1)
    w1 = 1.0 / (1.0 + t)
    w2 = t * w1
    sel = jnp.logical_or(lane == e1, lane == e2)
    onehot = jnp.where(sel, 1.0, 0.0)
    r = lax.broadcasted_iota(jnp.int32, (TM, TM), 0)
    c = lax.broadcasted_iota(jnp.int32, (TM, TM), 1)
    below = jnp.where(r > c, 1.0, 0.0).astype(BF16)
    before = _dot(below, onehot.astype(BF16)) + run_s[...]
    r1 = jnp.sum(jnp.where(lane == e1, before, 0.0), axis=-1, keepdims=True).astype(jnp.int32)
    r2 = jnp.sum(jnp.where(lane == e2, before, 0.0), axis=-1, keepdims=True).astype(jnp.int32)
    run_s[...] = run_s[...] + jnp.sum(onehot, axis=0, keepdims=True)
    meta = jnp.where(lane == 0, e1, jnp.where(lane == 1, e2, jnp.where(lane == 2, r1, jnp.where(lane == 3, r2, 0))))
    meta_ref[...] = meta
    wts_ref[...] = jnp.where(lane == 0, w1, jnp.where(lane == 1, w2, 0.0))
    cnt_ref[...] = run_s[...].astype(jnp.int32)


def _route_io(g, w_router):
    operands = [g.reshape(1, D), jnp.pad(w_router, ((0, 0), (0, LANES - N_EXPERTS)))]
    in_specs = [_const_spec((1, D)), _const_spec((D, LANES))]
    out_specs = [
        pl.BlockSpec((TM, LANES), lambda i: (i, 0)),
        pl.BlockSpec((TM, LANES), lambda i: (i, 0)),
        pl.BlockSpec((1, LANES), lambda i: (0, 0)),
    ]
    out_shape = [
        jax.ShapeDtypeStruct((N_TOK, LANES), jnp.int32),
        jax.ShapeDtypeStruct((N_TOK, LANES), F32),
        jax.ShapeDtypeStruct((1, LANES), jnp.int32),
    ]
    return operands, in_specs, out_specs, out_shape, [pltpu.VMEM((1, LANES), F32)]


ROW_TILE = D // LANES
assert ROW_TILE == SUBLANES


def _from_token_major(ref, n):
    return jnp.concatenate([ref[pl.ds(s, n, stride=ROW_TILE), :] for s in range(ROW_TILE)], axis=-1)


def _to_token_major(ref, val, n):
    for s in range(ROW_TILE):
        ref[pl.ds(s, n, stride=ROW_TILE), :] = val[:, s * LANES:(s + 1) * LANES]


def _row_copy(src, src_row, dst, dst_row, sem):
    s0 = pl.multiple_of(src_row * ROW_TILE, ROW_TILE)
    d0 = pl.multiple_of(dst_row * ROW_TILE, ROW_TILE)
    return pltpu.make_async_copy(src.at[pl.ds(s0, ROW_TILE), :], dst.at[pl.ds(d0, ROW_TILE), :], sem)


DMA_UNROLL = 16


def _dispatch_kernel(last_ref, nblk_ref, nu_ref, dest_ref, x_ref, xs_hbm, tok_s, sem, zsem):
    i = pl.program_id(0)
    slot = i % 2
    stage = tok_s.at[slot]

    def zero_block(b):
        return pltpu.make_async_copy(tok_s.at[0], xs_hbm.at[pl.ds(b * (BM * ROW_TILE), BM * ROW_TILE), :], zsem)

    @pl.when(i == 0)
    def _():
        tok_s[0] = jnp.zeros((BM * ROW_TILE, LANES), F32)
        for start in (True, False):
            for e in range(N_EXPERTS):
                tail = N_BLOCKS - 1 - e
                for cond, b in ((nblk_ref[e] > 0, last_ref[e]), (tail >= nu_ref[0], tail)):
                    cp = zero_block(b)
                    pl.when(cond)(cp.start if start else cp.wait)

    _to_token_major(stage, x_ref[...], TM)

    def issue(j, carry):
        for u in range(DMA_UNROLL):
            t = j * DMA_UNROLL + u
            _row_copy(stage, t, xs_hbm, dest_ref[0, 0, 2 * t], sem.at[slot]).start(priority=0)
            _row_copy(stage, t, xs_hbm, dest_ref[0, 0, 2 * t + 1], sem.at[slot]).start(priority=1)
        return carry

    lax.fori_loop(0, TM // DMA_UNROLL, issue, 0)

    def drain(s):
        def body(j, carry):
            for _ in range(2 * DMA_UNROLL):
                _row_copy(tok_s.at[s], 0, xs_hbm, 0, sem.at[s]).wait()
            return carry

        lax.fori_loop(0, TM // DMA_UNROLL, body, 0)

    pl.when(i >= 1)(lambda: drain(1 - slot))
    pl.when(i == N_TILES - 1)(lambda: drain(slot))


def _dispatch(x, dest, last_block, n_blocks, n_used):
    return pl.pallas_call(
        _dispatch_kernel,
        grid_spec=pltpu.PrefetchScalarGridSpec(
            num_scalar_prefetch=3,
            grid=(N_TILES,),
            in_specs=[
                pl.BlockSpec((1, 1, 2 * TM), lambda i, *_: (i, 0, 0), memory_space=pltpu.SMEM),
                pl.BlockSpec((TM, D), lambda i, *_: (i, 0)),
            ],
            out_specs=pl.BlockSpec(memory_space=pl.ANY),
            scratch_shapes=[pltpu.VMEM((2, TM * ROW_TILE, LANES), F32), pltpu.SemaphoreType.DMA((2,)),
                            pltpu.SemaphoreType.DMA(())],
        ),
        out_shape=jax.ShapeDtypeStruct((N_ROWS * ROW_TILE, LANES), F32),
        compiler_params=pltpu.CompilerParams(dimension_semantics=("arbitrary",), vmem_limit_bytes=VMEM_LIMIT),
        name="moe_dispatch",
    )(last_block, n_blocks, n_used, dest.reshape(N_TILES, 1, 2 * TM), x)


def _gmm_kernel(be_ref, nu_ref, xs_ref, g_ref, wg_ref, wu_ref, wd_ref, ys_ref, hid_s):
    used = pl.program_id(0) < nu_ref[0]

    @pl.when(used)
    def _():
        h = _rmsnorm(_from_token_major(xs_ref, BM), g_ref[...]).astype(BF16)
        _swiglu_hidden(h, wg_ref, wu_ref, hid_s, lead=(0, 0))
        _to_token_major(ys_ref, _dot(hid_s[...], wd_ref[0, 0]), BM)

    @pl.when(jnp.logical_not(used))
    def _():
        ys_ref[...] = jnp.zeros((BM * ROW_TILE, LANES), F32)


def _gmm(xs, g, block_expert, n_used, layer, w_gate, w_up, w_down):
    row_map = lambda i, be, nu: (jnp.minimum(i, nu[0] - 1), 0)
    wspec = lambda shape: pl.BlockSpec((1, 1) + shape, lambda i, be, nu: (layer, be[i], 0, 0))
    return pl.pallas_call(
        _gmm_kernel,
        grid_spec=pltpu.PrefetchScalarGridSpec(
            num_scalar_prefetch=2,
            grid=(N_BLOCKS,),
            in_specs=[
                pl.BlockSpec((BM * ROW_TILE, LANES), row_map),
                pl.BlockSpec((1, D), lambda i, be, nu: (0, 0)),
                wspec((D, D_FF)),
                wspec((D, D_FF)),
                wspec((D_FF, D)),
            ],
            out_specs=pl.BlockSpec((BM * ROW_TILE, LANES), lambda i, be, nu: (i, 0)),
            scratch_shapes=[pltpu.VMEM((BM, D_FF), BF16)],
        ),
        out_shape=jax.ShapeDtypeStruct((N_ROWS * ROW_TILE, LANES), F32),
        compiler_params=pltpu.CompilerParams(dimension_semantics=("arbitrary",), vmem_limit_bytes=VMEM_LIMIT),
        name="moe_gmm",
    )(block_expert, n_used, xs, g.reshape(1, D), w_gate, w_up, w_down)


def _combine_kernel(dest_ref, next_ref, x_ref, wts_ref, gf_ref, ys_hbm, *rest, final):
    out_refs, (buf0, buf1, sem) = rest[:-3], rest[-3:]
    i = pl.program_id(0)
    slot = i % 2

    def gather(idx_ref, s):
        def issue(j, carry):
            for u in range(DMA_UNROLL):
                t = j * DMA_UNROLL + u
                _row_copy(ys_hbm, idx_ref[0, 0, 2 * t], buf0.at[s], t, sem.at[s]).start(priority=0)
                _row_copy(ys_hbm, idx_ref[0, 0, 2 * t + 1], buf1.at[s], t, sem.at[s]).start(priority=1)
            return carry

        lax.fori_loop(0, TM // DMA_UNROLL, issue, 0)

    pl.when(i == 0)(lambda: gather(dest_ref, slot))
    pl.when(i + 1 < N_TILES)(lambda: gather(next_ref, 1 - slot))

    def drain(j, carry):
        for _ in range(DMA_UNROLL):
            _row_copy(ys_hbm, 0, buf0.at[slot], 0, sem.at[slot]).wait()
            _row_copy(ys_hbm, 0, buf1.at[slot], 0, sem.at[slot]).wait()
        return carry

    lax.fori_loop(0, TM // DMA_UNROLL, drain, 0)
    w = wts_ref[...]
    y = (x_ref[...] + w[:, 0:1] * _from_token_major(buf0.at[slot], TM)
         + w[:, 1:2] * _from_token_major(buf1.at[slot], TM))
    if final:
        y = _rmsnorm(y, gf_ref[...])
        yp_ref, ysm_ref = out_refs
        is_sample = pl.program_id(0) >= N_PROMPT_TILES

        @pl.when(jnp.logical_not(is_sample))
        def _():
            yp_ref[...] = y

        @pl.when(is_sample)
        def _():
            ysm_ref[...] = y
    else:
        out_refs[0][...] = y


def _combine(x, dest, wts, ys, g_final, final):
    if final:
        out_specs = [pl.BlockSpec((TM, D), lambda i: (jnp.minimum(i, N_PROMPT_TILES - 1), 0)),
                     pl.BlockSpec((TM, D), lambda i: (0, 0))]
        out_shape = [jax.ShapeDtypeStruct((N_PROMPT_TOK, D), F32), jax.ShapeDtypeStruct((N_SAMPLE_TOK, D), F32)]
    else:
        out_specs = [pl.BlockSpec((TM, D), lambda i: (i, 0))]
        out_shape = [jax.ShapeDtypeStruct((N_TOK, D), F32)]
    dest3 = dest.reshape(N_TILES, 1, 2 * TM)
    outs = pl.pallas_call(
        functools.partial(_combine_kernel, final=final),
        grid=(N_TILES,),
        in_specs=[
            pl.BlockSpec((1, 1, 2 * TM), lambda i: (i, 0, 0), memory_space=pltpu.SMEM),
            pl.BlockSpec((1, 1, 2 * TM), lambda i: (jnp.minimum(i + 1, N_TILES - 1), 0, 0), memory_space=pltpu.SMEM),
            pl.BlockSpec((TM, D), lambda i: (i, 0)),
            pl.BlockSpec((TM, LANES), lambda i: (i, 0)),
            _const_spec((1, D)),
            pl.BlockSpec(memory_space=pl.ANY),
        ],
        out_specs=out_specs,
        out_shape=out_shape,
        scratch_shapes=[pltpu.VMEM((2, TM * ROW_TILE, LANES), F32), pltpu.VMEM((2, TM * ROW_TILE, LANES), F32),
                        pltpu.SemaphoreType.DMA((2,))],
        compiler_params=pltpu.CompilerParams(dimension_semantics=("arbitrary",), vmem_limit_bytes=VMEM_LIMIT),
        name="moe_combine",
    )(dest3, dest3, x, wts, g_final.reshape(1, D), ys)
    return outs if final else outs[0]


def _moe_ffn(x, routing, g, layer, w_gate, w_up, w_down, g_final, final):
    meta, wts, counts = routing
    counts = counts[0, :N_EXPERTS]
    blocks = (counts + BM - 1) // BM
    block_end = jnp.cumsum(blocks)
    seg_start = (block_end - blocks) * BM
    dest = seg_start[meta[:, 0:2]] + meta[:, 2:4]
    n_used = block_end[-1:]
    bidx = jnp.minimum(jnp.arange(N_BLOCKS, dtype=jnp.int32), n_used - 1)
    block_expert = jnp.sum(bidx[:, None] >= block_end[None, :], axis=1).astype(jnp.int32)
    n_used = n_used.astype(jnp.int32)
    xs = _dispatch(x, dest, (block_end - 1).astype(jnp.int32), blocks.astype(jnp.int32), n_used)
    ys = _gmm(xs, g, block_expert, n_used, layer, w_gate, w_up, w_down)
    return _combine(x, dest, wts, ys, g_final, final)


def kernel(x_prompt, x_sample, state_conv_b, state_conv_c, norm_mix, norm_ffn, norm_final, a_w_in, a_ln_g, a_ln_b, a_w_s, a_b_s, a_w_out, b_w_in, b_conv, b_w_out, c_w_pw1, c_b_pw1, c_dw, c_b_dw, c_ln_g, c_ln_b, c_w_pw2, c_b_pw2, f_w_gate, f_w_up, f_w_down, m_router, m_w_gate, m_w_up, m_w_down):
    last_tile = jnp.arange(N_SEQ) * TILES_PER_SEQ + TILES_PER_SEQ - 1
    fw = [w.astype(BF16) for w in (f_w_gate, f_w_up, f_w_down)]
    mw = [w.astype(BF16) for w in (m_w_gate, m_w_up, m_w_down)]

    x, v0 = _gmlp_mixer(x_prompt.reshape(N_PROMPT_TOK, D), x_sample.reshape(N_SAMPLE_TOK, D), 0,
                        norm_mix[0], a_w_in[0], a_ln_g[0], a_ln_b[0], a_w_s[0], a_b_s[0], a_w_out[0])
    x = _dense_ffn(x, norm_ffn[0], 0, *fw)
    x, tb_p, tb_s, *routing = _sconv_mixer(x, norm_mix[1], b_w_in[0], b_conv[0], b_w_out[0], state_conv_b[0],
                                           norm_ffn[1], m_router[0])
    x = _moe_ffn(x, routing, norm_ffn[1], 0, *mw, norm_final, False)
    x, tc_p, tc_s = _conf_mixer(x, norm_mix[2], c_w_pw1[0], c_b_pw1[0], c_dw[0], c_b_dw[0], c_ln_g[0], c_ln_b[0],
                                c_w_pw2[0], c_b_pw2[0], state_conv_c[0])
    x = _dense_ffn(x, norm_ffn[2], 1, *fw)
    x, v1, *routing = _gmlp_mixer(x, x, N_PROMPT_TILES,
                                  norm_mix[3], a_w_in[1], a_ln_g[1], a_ln_b[1], a_w_s[1], a_b_s[1], a_w_out[1],
                                  routing=(norm_ffn[3], m_router[1]))
    y_prompt, y_sample = _moe_ffn(x, routing, norm_ffn[3], 1, *mw, norm_final, True)

    y_prompt = y_prompt.reshape(N_SEQ, SEQ, D)
    y_sample = y_sample.reshape(N_SEQ, SAMPLE_SEQ, D)
    nb = B_WIDTH - 1
    nc = C_WIDTH - 1
    new_b_p = tb_p[last_tile, B_HALO - nb:, :][None]
    new_b_s = tb_s[:, B_HALO - nb:, :][None]
    new_c_p = tc_p[last_tile, C_HALO - nc:, :][None]
    new_c_s = tc_s[:, C_HALO - nc:, :][None]
    new_v = jnp.stack([v0, v1]).reshape(2, N_SEQ, SAMPLE_SEQ, A_HALF)
    return (y_prompt, y_sample, new_b_p, new_b_s, new_c_p, new_c_s, new_v)
```
